```python
import jax, jax.numpy as jnp
from jax import lax
import numpy as np

D_MODEL = 1024
BATCH = 8
SEQ = 2048
DEPTH = 4
DEC_BATCH = 128
DEC_SEQ = 1
PAST_LEN = 16384
PAGE_SIZE = 128

N_MIXERS = 4
W_GROUP = D_MODEL // N_MIXERS
CONV_A = 3
CHUNK = 128
N_HEADS_B = 4
HEAD_B = W_GROUP // N_HEADS_B
CONV_C = 31
POOL_WINDOWS = (2, 4, 8, 16)
POOL_MAX = 16
N_POOL_GROUPS = 4
GROUP_D = W_GROUP // N_POOL_GROUPS
IN_COLS = 3 * W_GROUP + 2 * W_GROUP + 2 * W_GROUP + W_GROUP
D_FF = 2816
N_EXPERTS = 8
TOP_K = 2
D_FF_E = 1408
N_DENSE = (DEPTH + 1) // 2
N_MOE = DEPTH // 2
EPS = 1e-6

kernel_name = 'hybrid_conv_gmlp_conformer_pool_decoder_step'


def rmsnorm(x, g):
    xf = x.astype(jnp.float32)
    y = xf * lax.rsqrt(jnp.mean(xf * xf, axis=-1, keepdims=True) + EPS)
    return (y * g.astype(jnp.float32)).astype(x.dtype)


def layernorm(x, g, b):
    xf = x.astype(jnp.float32)
    xc = xf - jnp.mean(xf, axis=-1, keepdims=True)
    var = jnp.mean(xc * xc, axis=-1, keepdims=True)
    y = xc * lax.rsqrt(var + EPS) * g.astype(jnp.float32) + b.astype(jnp.float32)
    return y.astype(x.dtype)


def causal_dwconv(x, buf, w):
    width = w.shape[0]
    xp = jnp.concatenate([buf.astype(x.dtype), x], axis=1)
    y = lax.conv_general_dilated(xp, w[:, None, :].astype(x.dtype), window_strides=(1,),
                                 padding='VALID', dimension_numbers=('NWC', 'WIO', 'NWC'),
                                 feature_group_count=x.shape[-1])
    return y, xp[:, xp.shape[1] - (width - 1):]


def chunk_spatial_mix(v, w_s, b_s):
    n_b, length, _ = v.shape
    n_chunks = -(-length // CHUNK)
    pad = n_chunks * CHUNK - length
    vp = jnp.pad(v, ((0, 0), (0, pad), (0, 0))).reshape(n_b, n_chunks, CHUNK, N_HEADS_B, HEAD_B)
    causal = jnp.tril(jnp.ones((CHUNK, CHUNK), dtype=bool))
    w_masked = jnp.where(causal[None], w_s, jnp.zeros_like(w_s)).astype(v.dtype)
    mixed = jnp.einsum('hts,bnshd->bnthd', w_masked, vp)
    mixed = mixed + jnp.swapaxes(b_s, 0, 1).astype(v.dtype)[None, None, :, :, None]
    return mixed.reshape(n_b, n_chunks * CHUNK, W_GROUP)[:, :length]


def causal_multiscale_pool(p_in, buf, start_pos):
    length = p_in.shape[1]
    pcat = jnp.concatenate([buf.astype(p_in.dtype), p_in], axis=1)
    cs = jnp.cumsum(pcat.astype(jnp.float32), axis=1)
    cs = jnp.pad(cs, ((0, 0), (1, 0), (0, 0)))
    pos = start_pos + jnp.arange(length)
    means = []
    for gi, win in enumerate(POOL_WINDOWS):
        sl = slice(gi * GROUP_D, (gi + 1) * GROUP_D)
        s = cs[:, POOL_MAX:POOL_MAX + length, sl] - cs[:, POOL_MAX - win:POOL_MAX - win + length, sl]
        cnt = jnp.minimum(pos + 1, win).astype(jnp.float32)[None, :, None]
        means.append(s / cnt)
    mean = jnp.concatenate(means, axis=-1)
    pooled = (mean - p_in.astype(jnp.float32)).astype(p_in.dtype)
    return pooled, pcat[:, pcat.shape[1] - (POOL_MAX - 1):]


def swiglu(h, w1, w3, w2):
    return (jax.nn.silu(h @ w1) * (h @ w3)) @ w2


def moe_swiglu(h, w_router, b_router, w1, w3, w2):
    logits = h.astype(jnp.float32) @ w_router.astype(jnp.float32) + b_router.astype(jnp.float32)
    top_v, top_i = lax.top_k(logits, TOP_K)
    gates = jax.nn.softmax(top_v, axis=-1)
    comb = jnp.sum(jax.nn.one_hot(top_i, N_EXPERTS, dtype=jnp.float32) * gates[..., None], axis=-2)
    comb = comb.astype(h.dtype)
    y = jnp.zeros_like(h)
    for e in range(N_EXPERTS):
        y = y + comb[..., e:e + 1] * swiglu(h, w1[e], w3[e], w2[e])
    return y


def token_mixers(h, buf_a, buf_c, buf_d, start_pos, l, p):
    n_b, length, _ = h.shape
    proj = h @ p['w_in'][l]
    a_b, a_c, a_h, b_u, b_v, c_a, c_g, d_p = jnp.split(proj, 8, axis=-1)
    y_a, nb_a = causal_dwconv(a_c * a_h, buf_a, p['conv_a_w'][l])
    out_a = a_b * y_a
    v_n = layernorm(b_v, p['ln_v_g'][l], p['ln_v_b'][l])
    out_b = b_u * chunk_spatial_mix(v_n, p['w_s'][l], p['b_s'][l])
    glu = c_a * jax.nn.sigmoid(c_g)
    y_c, nb_c = causal_dwconv(glu, buf_c, p['conv_c_w'][l])
    out_c = jax.nn.silu(layernorm(y_c + p['conv_c_b'][l], p['ln_c_g'][l], p['ln_c_b'][l]))
    pooled, nb_d = causal_multiscale_pool(d_p, buf_d, start_pos)
    out_d = jnp.einsum('blgc,gcd->blgd', pooled.reshape(n_b, length, N_POOL_GROUPS, GROUP_D),
                       p['pool_w'][l]).reshape(n_b, length, W_GROUP) * p['pool_scale'][l]
    groups = jnp.stack([out_a, out_b, out_c, out_d], axis=2)
    merged = rmsnorm(groups, p['g_out'][l]).reshape(n_b, length, D_MODEL)
    return merged @ p['w_out'][l], nb_a, nb_c, nb_d, v_n


def trunk(x, c, buf_a, buf_c, buf_d, start_pos, p):
    length = x.shape[1]
    keep_from = CHUNK * ((length - 1) // CHUNK)
    new_a, new_c, new_d, new_v = [], [], [], []
    for l in range(DEPTH):
        mod = (jax.nn.silu(c) @ p['w_ada'][l] + p['b_ada'][l])[:, None, :]
        shift1, scale1, gate1, shift2, scale2, gate2 = jnp.split(mod, 6, axis=-1)
        h = rmsnorm(x, p['g_mix'][l]) * (1 + scale1) + shift1
        mix, nb_a, nb_c, nb_d, v_n = token_mixers(h, buf_a[l], buf_c[l], buf_d[l], start_pos, l, p)
        x = x + gate1 * mix
        h = rmsnorm(x, p['g_ffn'][l]) * (1 + scale2) + shift2
        i = l // 2
        if l % 2 == 0:
            f = swiglu(h, p['w1_dense'][i], p['w3_dense'][i], p['w2_dense'][i])
        else:
            f = moe_swiglu(h, p['w_router'][i], p['b_router'][i],
                           p['w1_moe'][i], p['w3_moe'][i], p['w2_moe'][i])
        x = x + gate2 * f
        new_a.append(nb_a)
        new_c.append(nb_c)
        new_d.append(nb_d)
        new_v.append(v_n[:, keep_from:])
    y = rmsnorm(x, p['g_final'])
    return y, jnp.stack(new_a), jnp.stack(new_c), jnp.stack(new_d), jnp.stack(new_v)


def setup_inputs(seed: int = 0) -> dict:
    key = jax.random.key(seed)
    ks = iter(jax.random.split(key, 64))

    def nrm(shape, s):
        return jax.random.normal(next(ks), shape, jnp.float32) * s

    def gain(shape):
        return 1.0 + 0.1 * jax.random.normal(next(ks), shape, jnp.float32)

    d = D_MODEL
    return {
        'x_prompt': nrm((BATCH, SEQ, d), 1.0),
        'x_sample': nrm((DEC_BATCH, DEC_SEQ, d), 1.0),
        'state_conv_a': nrm((DEPTH, DEC_BATCH, CONV_A - 1, W_GROUP), 1.0),
        'state_conv_c': nrm((DEPTH, DEC_BATCH, CONV_C - 1, W_GROUP), 0.5),
        'state_pool_d': nrm((DEPTH, DEC_BATCH, POOL_MAX - 1, W_GROUP), 1.0),
        'c_prompt': nrm((BATCH, d), 1.0),
        'c_sample': nrm((DEC_BATCH, d), 1.0),
        'w_ada': nrm((DEPTH, d, 6 * d), 0.5 * d ** -0.5),
        'b_ada': nrm((DEPTH, 6 * d), 0.02),
        'g_mix': gain((DEPTH, d)),
        'w_in': nrm((DEPTH, d, IN_COLS), d ** -0.5),
        'conv_a_w': nrm((DEPTH, CONV_A, W_GROUP), CONV_A ** -0.5),
        'ln_v_g': gain((DEPTH, W_GROUP)),
        'ln_v_b': nrm((DEPTH, W_GROUP), 0.02),
        'w_s': nrm((DEPTH, N_HEADS_B, CHUNK, CHUNK), CHUNK ** -0.5),
        'b_s': gain((DEPTH, N_HEADS_B, CHUNK)),
        'conv_c_w': nrm((DEPTH, CONV_C, W_GROUP), CONV_C ** -0.5),
        'conv_c_b': nrm((DEPTH, W_GROUP), 0.02),
        'ln_c_g': gain((DEPTH, W_GROUP)),
        'ln_c_b': nrm((DEPTH, W_GROUP), 0.02),
        'pool_w': nrm((DEPTH, N_POOL_GROUPS, GROUP_D, GROUP_D), GROUP_D ** -0.5),
        'pool_scale': gain((DEPTH, W_GROUP)),
        'g_out': gain((DEPTH, N_MIXERS, W_GROUP)),
        'w_out': nrm((DEPTH, d, d), d ** -0.5),
        'g_ffn': gain((DEPTH, d)),
        'w1_dense': nrm((N_DENSE, d, D_FF), d ** -0.5),
        'w3_dense': nrm((N_DENSE, d, D_FF), d ** -0.5),
        'w2_dense': nrm((N_DENSE, D_FF, d), D_FF ** -0.5),
        'w_router': nrm((N_MOE, d, N_EXPERTS), d ** -0.5),
        'b_router': nrm((N_MOE, N_EXPERTS), 0.01),
        'w1_moe': nrm((N_MOE, N_EXPERTS, d, D_FF_E), d ** -0.5),
        'w3_moe': nrm((N_MOE, N_EXPERTS, d, D_FF_E), d ** -0.5),
        'w2_moe': nrm((N_MOE, N_EXPERTS, D_FF_E, d), D_FF_E ** -0.5),
        'g_final': gain((d,)),
    }


def reference(x_prompt, x_sample, state_conv_a, state_conv_c, state_pool_d, c_prompt, c_sample,
              w_ada, b_ada, g_mix, w_in, conv_a_w, ln_v_g, ln_v_b, w_s, b_s, conv_c_w, conv_c_b,
              ln_c_g, ln_c_b, pool_w, pool_scale, g_out, w_out, g_ffn, w1_dense, w3_dense, w2_dense,
              w_router, b_router, w1_moe, w3_moe, w2_moe, g_final):
    p = dict(w_ada=w_ada, b_ada=b_ada, g_mix=g_mix, w_in=w_in, conv_a_w=conv_a_w,
             ln_v_g=ln_v_g, ln_v_b=ln_v_b, w_s=w_s, b_s=b_s, conv_c_w=conv_c_w, conv_c_b=conv_c_b,
             ln_c_g=ln_c_g, ln_c_b=ln_c_b, pool_w=pool_w, pool_scale=pool_scale, g_out=g_out,
             w_out=w_out, g_ffn=g_ffn, w1_dense=w1_dense, w3_dense=w3_dense, w2_dense=w2_dense,
             w_router=w_router, b_router=b_router, w1_moe=w1_moe, w3_moe=w3_moe, w2_moe=w2_moe,
             g_final=g_final)
    n_p = x_prompt.shape[0]
    zeros_a = jnp.zeros((DEPTH, n_p, CONV_A - 1, W_GROUP), x_prompt.dtype)
    zeros_c = jnp.zeros((DEPTH, n_p, CONV_C - 1, W_GROUP), x_prompt.dtype)
    zeros_d = jnp.zeros((DEPTH, n_p, POOL_MAX - 1, W_GROUP), x_prompt.dtype)
    y_prompt, a_p, c_p, d_p, v_p = trunk(x_prompt, c_prompt, zeros_a, zeros_c, zeros_d, 0, p)
    y_sample, a_s, c_s, d_s, v_s = trunk(x_sample, c_sample, state_conv_a, state_conv_c,
                                         state_pool_d, PAST_LEN, p)
    return (y_prompt, y_sample, a_p, c_p, d_p, v_p, a_s, c_s, d_s, v_s)
```

```python
import functools

import jax
import jax.numpy as jnp
from jax import lax
from jax.experimental import pallas as pl
from jax.experimental.pallas import tpu as pltpu

D_MODEL = 1024
DEPTH = 4
W_GROUP = 256
CONV_A = 3
CHUNK = 128
N_HEADS_B = 4
HEAD_B = 64
CONV_C = 31
POOL_WINDOWS = (2, 4, 8, 16)
POOL_MAX = 16
GROUP_D = 64
IN_COLS = 2048
N_EXPERTS = 8
PAST_LEN = 16384
EPS = 1e-6

LANES = 128
HIST_A = 8
HIST_C = 32
HIST_D = 32
TL_MIX = 512
ROW_CHUNK = 64
TM_FFN = 512
VMEM_LIMIT = 56 * 1024 * 1024

F32 = jnp.float32
BF16 = jnp.bfloat16


def _dot(a, b):
    return jnp.dot(a, b, preferred_element_type=F32)


def _rms(x, g):
    return x * lax.rsqrt(jnp.mean(x * x, axis=-1, keepdims=True) + EPS) * g


def _ln(x, g, b):
    xc = x - jnp.mean(x, axis=-1, keepdims=True)
    var = jnp.mean(xc * xc, axis=-1, keepdims=True)
    return xc * lax.rsqrt(var + EPS) * g + b


def _silu(x):
    return x * jax.nn.sigmoid(x)


def _lane_group_select(vals, shape):
    lane = lax.broadcasted_iota(jnp.int32, shape, 1)
    out = vals[3]
    for g in (2, 1, 0):
        out = jnp.where(lane < (g + 1) * GROUP_D, vals[g], out)
    return out


def _ada_kernel(cp_ref, cs_ref, w_ref, b_ref, op_ref, os_ref):
    w = w_ref[...].astype(BF16)
    b = b_ref[...]
    op_ref[...] = _dot(_silu(cp_ref[...]).astype(BF16), w) + b
    os_ref[...] = _dot(_silu(cs_ref[...]).astype(BF16), w) + b


def _ada(c_prompt, c_sample, w_ada, b_ada):
    n_p, n_s = c_prompt.shape[0], c_sample.shape[0]
    d = D_MODEL
    return pl.pallas_call(
        _ada_kernel,
        grid=(DEPTH, 6),
        in_specs=[
            pl.BlockSpec((n_p, d), lambda l, k: (0, 0)),
            pl.BlockSpec((n_s, d), lambda l, k: (0, 0)),
            pl.BlockSpec((None, d, d), lambda l, k: (l, 0, k)),
            pl.BlockSpec((None, 1, d), lambda l, k: (l, 0, k)),
        ],
        out_specs=[
            pl.BlockSpec((None, None, n_p, d), lambda l, k: (l, k, 0, 0)),
            pl.BlockSpec((None, None, n_s, d), lambda l, k: (l, k, 0, 0)),
        ],
        out_shape=[
            jax.ShapeDtypeStruct((DEPTH, 6, n_p, d), F32),
            jax.ShapeDtypeStruct((DEPTH, 6, n_s, d), F32),
        ],
        compiler_params=pltpu.CompilerParams(
            dimension_semantics=("arbitrary", "arbitrary"), vmem_limit_bytes=VMEM_LIMIT),
        name="ada",
    )(c_prompt, c_sample, w_ada, b_ada.reshape(DEPTH, 1, 6 * d))


def _masked_ws(ws_ref):
    r = lax.broadcasted_iota(jnp.int32, (CHUNK, CHUNK), 0)
    c = lax.broadcasted_iota(jnp.int32, (CHUNK, CHUNK), 1)
    return [jnp.where(c <= r, ws_ref[h], 0.0).astype(BF16) for h in range(N_HEADS_B)]


def _merge_and_project(x, gate, outs, gout_ref, wout_bf):
    merged = jnp.concatenate(
        [_rms(o, gout_ref[i:i + 1, :]) for i, o in enumerate(outs)], axis=1).astype(BF16)
    return x + gate * _dot(merged, wout_bf)


def _mix_prompt_kernel(x_ref, shift_ref, scale_ref, gate_ref, gmix_ref, win_ref, wout_ref,
                       convaw_ref, lnvg_ref, lnvb_ref, ws_ref, bsfull_ref, convcw_ref,
                       convcb_ref, lncg_ref, lncb_ref, poolw_ref, poolscale_ref, gout_ref,
                       xo_ref, na_ref, nc_ref, nd_ref, nv_ref,
                       win_bf, wout_bf, exta, extc, extd, ext2, ext4, ext8,
                       *, tl, start_pos):
    b = pl.program_id(0)
    t = pl.program_id(1)
    last_t = pl.num_programs(1) - 1

    @pl.when(jnp.logical_and(b == 0, t == 0))
    def _cast_weights():
        rows = 128
        def body(i, carry):
            r0 = pl.multiple_of(i * rows, rows)
            win_bf[pl.ds(r0, rows), :] = win_ref[pl.ds(r0, rows), :].astype(BF16)
            wout_bf[pl.ds(r0, rows), :] = wout_ref[pl.ds(r0, rows), :].astype(BF16)
            return carry
        lax.fori_loop(0, D_MODEL // rows, body, 0)

    @pl.when(t == 0)
    def _zero_history():
        exta[0:HIST_A, :] = jnp.zeros((HIST_A, W_GROUP), F32)
        extc[0:HIST_C, :] = jnp.zeros((HIST_C, W_GROUP), F32)
        extd[0:HIST_D, :] = jnp.zeros((HIST_D, W_GROUP), F32)

    x = x_ref[...]
    shift = shift_ref[pl.ds(b, 1), :]
    scale = scale_ref[pl.ds(b, 1), :]
    gate = gate_ref[pl.ds(b, 1), :]
    h = _rms(x, gmix_ref[...]) * (1.0 + scale) + shift
    proj = _dot(h.astype(BF16), win_bf[...])
    a_b, a_c, a_h, b_u, b_v, c_a, c_g, d_p = [
        proj[:, i * W_GROUP:(i + 1) * W_GROUP] for i in range(8)]

    def dwconv(ext_ref, w_ref, hist, width):
        off = hist - (width - 1)
        chunks = []
        for c0 in range(0, tl, ROW_CHUNK):
            acc = None
            for k in range(width):
                s = off + c0 + k
                term = ext_ref[s:s + ROW_CHUNK, :] * w_ref[k:k + 1, :]
                acc = term if acc is None else acc + term
            chunks.append(acc)
        return jnp.concatenate(chunks, axis=0)

    exta[HIST_A:HIST_A + tl, :] = a_c * a_h
    out_a = a_b * dwconv(exta, convaw_ref, HIST_A, CONV_A)

    v_n = _ln(b_v, lnvg_ref[...], lnvb_ref[...])
    v_bf = v_n.astype(BF16)
    wm = _masked_ws(ws_ref)
    lane = lax.broadcasted_iota(jnp.int32, (CHUNK, W_GROUP), 1)
    mixed_chunks = []
    for j in range(tl // CHUNK):
        vc = v_bf[j * CHUNK:(j + 1) * CHUNK, :]
        mixed = _dot(wm[3], vc)
        for hd in (2, 1, 0):
            mixed = jnp.where(lane < (hd + 1) * HEAD_B, _dot(wm[hd], vc), mixed)
        mixed_chunks.append(mixed + bsfull_ref[...])
    out_b = b_u * jnp.concatenate(mixed_chunks, axis=0)

    extc[HIST_C:HIST_C + tl, :] = c_a * jax.nn.sigmoid(c_g)
    y_c = dwconv(extc, convcw_ref, HIST_C, CONV_C) + convcb_ref[...]
    out_c = _silu(_ln(y_c, lncg_ref[...], lncb_ref[...]))

    n = HIST_D + tl
    extd[HIST_D:n, :] = d_p
    ext2[8:n, :] = extd[8:n, :] + extd[7:n - 1, :]
    ext4[16:n, :] = ext2[16:n, :] + ext2[14:n - 2, :]
    ext8[24:n, :] = ext4[24:n, :] + ext4[20:n - 4, :]
    s16 = ext8[HIST_D:n, :] + ext8[HIST_D - 8:n - 8, :]
    sums = [ext2[HIST_D:n, :], ext4[HIST_D:n, :], ext8[HIST_D:n, :], s16]
    shape = (tl, W_GROUP)
    pos1 = start_pos + 1 + t * tl + lax.broadcasted_iota(jnp.int32, shape, 0)
    cnt = _lane_group_select([jnp.minimum(pos1, w) for w in POOL_WINDOWS], shape).astype(F32)
    pooled = _lane_group_select(sums, shape) / cnt - d_p
    out_d = _dot(pooled.astype(BF16), poolw_ref[...].astype(BF16)) * poolscale_ref[...]

    xo_ref[...] = _merge_and_project(x, gate, [out_a, out_b, out_c, out_d], gout_ref, wout_bf[...])

    @pl.when(t == last_t)
    def _emit_state():
        na_ref[...] = exta[HIST_A + tl - (CONV_A - 1):HIST_A + tl, :]
        nc_ref[...] = extc[HIST_C + tl - (CONV_C - 1):HIST_C + tl, :]
        nd_ref[...] = extd[HIST_D + tl - (POOL_MAX - 1):HIST_D + tl, :]
        nv_ref[...] = v_n[tl - CHUNK:tl, :]

    exta[0:HIST_A, :] = exta[tl:tl + HIST_A, :]
    extc[0:HIST_C, :] = extc[tl:tl + HIST_C, :]
    extd[0:HIST_D, :] = extd[tl:tl + HIST_D, :]


def _mix_weights(l, p):
    row = lambda a: a.reshape(1, -1)
    eye = jnp.eye(4, dtype=F32)
    pool_bd = (eye[:, None, :, None] * p['pool_w'][l][:, :, None, :]).reshape(W_GROUP, W_GROUP)
    return dict(
        gmix=row(p['g_mix'][l]), conva=p['conv_a_w'][l], lnvg=row(p['ln_v_g'][l]),
        lnvb=row(p['ln_v_b'][l]), ws=p['w_s'][l], convc=p['conv_c_w'][l],
        convcb=row(p['conv_c_b'][l]), lncg=row(p['ln_c_g'][l]), lncb=row(p['ln_c_b'][l]),
        poolw=pool_bd, poolscale=row(p['pool_scale'][l]), gout=p['g_out'][l])


def _const_spec(shape):
    nd = len(shape)
    return pl.BlockSpec(shape, lambda *_: (0,) * nd, pipeline_mode=pl.Buffered(1))


def _mix_prompt(x, mod_p, l, p, start_pos):
    n_b, seq, d = x.shape
    tl = TL_MIX
    assert seq % tl == 0 and tl % CHUNK == 0 and seq >= CHUNK
    w = _mix_weights(l, p)
    bs_full = jnp.repeat(p['b_s'][l].T, HEAD_B, axis=1)
    mod_spec = lambda k: pl.BlockSpec((None, None, n_b, d), lambda b, t: (l, k, 0, 0))
    weights = [w['gmix'], p['w_in'][l], p['w_out'][l], w['conva'], w['lnvg'], w['lnvb'], w['ws'],
               bs_full, w['convc'], w['convcb'], w['lncg'], w['lncb'], w['poolw'],
               w['poolscale'], w['gout']]
    state_spec = lambda r: pl.BlockSpec((None, r, W_GROUP), lambda b, t: (b, 0, 0))
    ext = lambda hist: pltpu.VMEM((hist + tl, W_GROUP), F32)
    return pl.pallas_call(
        functools.partial(_mix_prompt_kernel, tl=tl, start_pos=start_pos),
        grid=(n_b, seq // tl),
        in_specs=[pl.BlockSpec((None, tl, d), lambda b, t: (b, t, 0)),
                  mod_spec(0), mod_spec(1), mod_spec(2)] + [_const_spec(a.shape) for a in weights],
        out_specs=[pl.BlockSpec((None, tl, d), lambda b, t: (b, t, 0)),
                   state_spec(CONV_A - 1), state_spec(CONV_C - 1), state_spec(POOL_MAX - 1),
                   state_spec(CHUNK)],
        out_shape=[jax.ShapeDtypeStruct(x.shape, F32),
                   jax.ShapeDtypeStruct((n_b, CONV_A - 1, W_GROUP), F32),
                   jax.ShapeDtypeStruct((n_b, CONV_C - 1, W_GROUP), F32),
                   jax.ShapeDtypeStruct((n_b, POOL_MAX - 1, W_GROUP), F32),
                   jax.ShapeDtypeStruct((n_b, CHUNK, W_GROUP), F32)],
        scratch_shapes=[pltpu.VMEM((d, IN_COLS), BF16), pltpu.VMEM((d, d), BF16),
                        ext(HIST_A), ext(HIST_C), ext(HIST_D), ext(HIST_D), ext(HIST_D),
                        ext(HIST_D)],
        compiler_params=pltpu.CompilerParams(
            dimension_semantics=("arbitrary", "arbitrary"), vmem_limit_bytes=VMEM_LIMIT),
        name=f"mix_prompt_{l}",
    )(x, mod_p, mod_p, mod_p, *weights)


def _mix_sample_kernel(x_ref, shift_ref, scale_ref, gate_ref, sa_ref, sc_ref, sd_ref, gmix_ref,
                       win_ref, wout_ref, convaw_ref, lnvg_ref, lnvb_ref, ws0_ref, bs0_ref,
                       convcw_ref, convcb_ref, lncg_ref, lncb_ref, poolw_ref, poolscale_ref,
                       gout_ref, xo_ref, na_ref, nc_ref, nd_ref, nv_ref, *, start_pos):
    x = x_ref[...]
    h = _rms(x, gmix_ref[...]) * (1.0 + scale_ref[...]) + shift_ref[...]
    proj = _dot(h.astype(BF16), win_ref[...].astype(BF16))
    a_b, a_c, a_h, b_u, b_v, c_a, c_g, d_p = [
        proj[:, i * W_GROUP:(i + 1) * W_GROUP] for i in range(8)]

    ch = a_c * a_h
    y_a = convaw_ref[CONV_A - 1:CONV_A, :] * ch
    for k in range(CONV_A - 1):
        y_a = y_a + convaw_ref[k:k + 1, :] * sa_ref[k]
    out_a = a_b * y_a
    for k in range(CONV_A - 2):
        na_ref[k] = sa_ref[k + 1]
    na_ref[CONV_A - 2] = ch

    v_n = _ln(b_v, lnvg_ref[...], lnvb_ref[...])
    out_b = b_u * (ws0_ref[...] * v_n + bs0_ref[...])
    nv_ref[...] = v_n

    glu = c_a * jax.nn.sigmoid(c_g)
    y_c = convcw_ref[CONV_C - 1:CONV_C, :] * glu + convcb_ref[...]
    for k in range(CONV_C - 1):
        y_c = y_c + convcw_ref[k:k + 1, :] * sc_ref[k]
    out_c = _silu(_ln(y_c, lncg_ref[...], lncb_ref[...]))
    for k in range(CONV_C - 2):
        nc_ref[k] = sc_ref[k + 1]
    nc_ref[CONV_C - 2] = glu

    hist = POOL_MAX - 1
    run = d_p
    taken = 0
    sums = []
    for w in POOL_WINDOWS:
        while taken < w - 1:
            run = run + sd_ref[hist - 1 - taken]
            taken += 1
        sums.append(run / float(min(start_pos + 1, w)))
    pooled = _lane_group_select(sums, d_p.shape) - d_p
    out_d = _dot(pooled.astype(BF16), poolw_ref[...].astype(BF16)) * poolscale_ref[...]
    for k in range(hist - 1):
        nd_ref[k] = sd_ref[k + 1]
    nd_ref[hist - 1] = d_p

    xo_ref[...] = _merge_and_project(x, gate_ref[...], [out_a, out_b, out_c, out_d], gout_ref,
                                     wout_ref[...].astype(BF16))


def _mix_sample(x, mod_s, sa_t, sc_t, sd_t, l, p, start_pos):
    n, d = x.shape
    assert start_pos + 1 >= POOL_MAX
    w = _mix_weights(l, p)
    ws0 = jnp.repeat(p['w_s'][l][:, 0, 0], HEAD_B).reshape(1, W_GROUP)
    bs0 = jnp.repeat(p['b_s'][l][:, 0], HEAD_B).reshape(1, W_GROUP)
    mod_spec = lambda k: pl.BlockSpec((None, None, n, d), lambda i: (l, k, 0, 0))
    st_spec = lambda r: pl.BlockSpec((None, r, n, W_GROUP), lambda i: (l, 0, 0, 0))
    weights = [w['gmix'], p['w_in'][l], p['w_out'][l], w['conva'], w['lnvg'], w['lnvb'], ws0, bs0,
               w['convc'], w['convcb'], w['lncg'], w['lncb'], w['poolw'], w['poolscale'], w['gout']]
    full = lambda shape: pl.BlockSpec(shape, lambda i: (0,) * len(shape))
    return pl.pallas_call(
        functools.partial(_mix_sample_kernel, start_pos=start_pos),
        grid=(1,),
        in_specs=[full((n, d)), mod_spec(0), mod_spec(1), mod_spec(2),
                  st_spec(CONV_A - 1), st_spec(CONV_C - 1), st_spec(POOL_MAX - 1)]
                 + [full(a.shape) for a in weights],
        out_specs=[full((n, d)), full((CONV_A - 1, n, W_GROUP)), full((CONV_C - 1, n, W_GROUP)),
                   full((POOL_MAX - 1, n, W_GROUP)), full((n, W_GROUP))],
        out_shape=[jax.ShapeDtypeStruct((n, d), F32),
                   jax.ShapeDtypeStruct((CONV_A - 1, n, W_GROUP), F32),
                   jax.ShapeDtypeStruct((CONV_C - 1, n, W_GROUP), F32),
                   jax.ShapeDtypeStruct((POOL_MAX - 1, n, W_GROUP), F32),
                   jax.ShapeDtypeStruct((n, W_GROUP), F32)],
        compiler_params=pltpu.CompilerParams(
            dimension_semantics=("arbitrary",), vmem_limit_bytes=VMEM_LIMIT),
        name=f"mix_sample_{l}",
    )(x, mod_s, mod_s, mod_s, sa_t, sc_t, sd_t, *weights)


def _modulation(shift_ref, scale_ref, gate_ref, rows_per_seq, tm):
    if rows_per_seq == 1:
        return shift_ref[...], scale_ref[...], gate_ref[...]
    b = (pl.program_id(0) * tm) // rows_per_seq
    return shift_ref[pl.ds(b, 1), :], scale_ref[pl.ds(b, 1), :], gate_ref[pl.ds(b, 1), :]


def _ffn_dense_kernel(x_ref, shift_ref, scale_ref, gate_ref, gffn_ref, gfin_ref, w1_ref, w3_ref,
                      w2_ref, o_ref, *, rows_per_seq, tm, final_norm):
    x = x_ref[...]
    shift, scale, gate = _modulation(shift_ref, scale_ref, gate_ref, rows_per_seq, tm)
    h = (_rms(x, gffn_ref[...]) * (1.0 + scale) + shift).astype(BF16)
    act = (_silu(_dot(h, w1_ref[...])) * _dot(h, w3_ref[...])).astype(BF16)
    y = x + gate * _dot(act, w2_ref[...])
    o_ref[...] = _rms(y, gfin_ref[...]) if final_norm else y


def _router_comb(h32, wr_ref, br_ref):
    wr = wr_ref[...]
    h_hi = h32.astype(BF16)
    h_lo = (h32 - h_hi.astype(F32)).astype(BF16)
    w_hi = wr.astype(BF16)
    w_lo = (wr - w_hi.astype(F32)).astype(BF16)
    logits = _dot(h_hi, w_hi) + (_dot(h_lo, w_hi) + _dot(h_hi, w_lo)) + br_ref[...]
    lane = lax.broadcasted_iota(jnp.int32, logits.shape, 1)
    lane_f = lane.astype(F32)
    neg = jnp.float32(-jnp.inf)
    logits = jnp.where(lane < N_EXPERTS, logits, neg)
    m1 = jnp.max(logits, axis=-1, keepdims=True)
    i1 = jnp.min(jnp.where(logits == m1, lane_f, float(LANES)), axis=-1, keepdims=True)
    rest = jnp.where(lane_f == i1, neg, logits)
    m2 = jnp.max(rest, axis=-1, keepdims=True)
    i2 = jnp.min(jnp.where(rest == m2, lane_f, float(LANES)), axis=-1, keepdims=True)
    e = jnp.exp(m2 - m1)
    g1 = 1.0 / (1.0 + e)
    g2 = e / (1.0 + e)
    return jnp.where(lane_f == i1, g1, 0.0) + jnp.where(lane_f == i2, g2, 0.0)


def _ffn_moe_kernel(x_ref, shift_ref, scale_ref, gate_ref, gffn_ref, gfin_ref, wr_ref, br_ref,
                    w1_ref, w3_ref, w2_ref, o_ref, h_bf, comb, acc, *, rows_per_seq, tm,
                    final_norm):
    e = pl.program_id(1)
    shift, scale, gate = _modulation(shift_ref, scale_ref, gate_ref, rows_per_seq, tm)

    @pl.when(e == 0)
    def _route():
        h32 = _rms(x_ref[...], gffn_ref[...]) * (1.0 + scale) + shift
        h_bf[...] = h32.astype(BF16)
        comb[...] = _router_comb(h32, wr_ref, br_ref)
        acc[...] = jnp.zeros(acc.shape, F32)

    h = h_bf[...]
    act = (_silu(_dot(h, w1_ref[...])) * _dot(h, w3_ref[...])).astype(BF16)
    lane = lax.broadcasted_iota(jnp.int32, comb.shape, 1)
    c_e = jnp.sum(jnp.where(lane == e, comb[...], 0.0), axis=-1, keepdims=True)
    acc[...] += c_e * _dot(act, w2_ref[...])

    @pl.when(e == pl.num_programs(1) - 1)
    def _finish():
        y = x_ref[...] + gate * acc[...]
        o_ref[...] = _rms(y, gfin_ref[...]) if final_norm else y


def _ffn(x2d, mod, l, p, rows_per_seq, final_norm):
    m, d = x2d.shape
    n_seq = mod.shape[2]
    tm = min(TM_FFN, m)
    assert m % tm == 0 and (rows_per_seq == 1 or rows_per_seq % tm == 0)
    i = l // 2
    row = lambda a: a.reshape(1, -1)
    if rows_per_seq == 1:
        mod_spec = lambda k: pl.BlockSpec((None, None, tm, d), lambda t, *_: (l, k, t, 0))
    else:
        mod_spec = lambda k: pl.BlockSpec((None, None, n_seq, d), lambda t, *_: (l, k, 0, 0))
    x_spec = pl.BlockSpec((tm, d), lambda t, *_: (t, 0))
    vec_spec = pl.BlockSpec((1, d), lambda t, *_: (0, 0))
    common = dict(rows_per_seq=rows_per_seq, tm=tm, final_norm=final_norm)
    if l % 2 == 0:
        w1, w3, w2 = (p[k][i].astype(BF16) for k in ('w1_dense', 'w3_dense', 'w2_dense'))
        return pl.pallas_call(
            functools.partial(_ffn_dense_kernel, **common),
            grid=(m // tm,),
            in_specs=[x_spec, mod_spec(3), mod_spec(4), mod_spec(5), vec_spec, vec_spec,
                      _const_spec(w1.shape), _const_spec(w3.shape), _const_spec(w2.shape)],
            out_specs=x_spec,
            out_shape=jax.ShapeDtypeStruct((m, d), F32),
            compiler_params=pltpu.CompilerParams(
                dimension_semantics=("arbitrary",), vmem_limit_bytes=VMEM_LIMIT),
            name=f"ffn_dense_{l}",
        )(x2d, mod, mod, mod, row(p['g_ffn'][l]), row(p['g_final']), w1, w3, w2)
    w1, w3, w2 = (p[k][i].astype(BF16) for k in ('w1_moe', 'w3_moe', 'w2_moe'))
    wr = jnp.pad(p['w_router'][i], ((0, 0), (0, LANES - N_EXPERTS)))
    br = jnp.pad(p['b_router'][i], (0, LANES - N_EXPERTS)).reshape(1, LANES)
    ff = w1.shape[-1]
    return pl.pallas_call(
        functools.partial(_ffn_moe_kernel, **common),
        grid=(m // tm, N_EXPERTS),
        in_specs=[x_spec, mod_spec(3), mod_spec(4), mod_spec(5), vec_spec, vec_spec,
                  _const_spec(wr.shape), _const_spec(br.shape),
                  pl.BlockSpec((None, d, ff), lambda t, e: (e, 0, 0)),
                  pl.BlockSpec((None, d, ff), lambda t, e: (e, 0, 0)),
                  pl.BlockSpec((None, ff, d), lambda t, e: (e, 0, 0))],
        out_specs=x_spec,
        out_shape=jax.ShapeDtypeStruct((m, d), F32),
        scratch_shapes=[pltpu.VMEM((tm, d), BF16), pltpu.VMEM((tm, LANES), F32),
                        pltpu.VMEM((tm, d), F32)],
        compiler_params=pltpu.CompilerParams(
            dimension_semantics=("arbitrary", "arbitrary"), vmem_limit_bytes=VMEM_LIMIT),
        name=f"ffn_moe_{l}",
    )(x2d, mod, mod, mod, row(p['g_ffn'][l]), row(p['g_final']), wr, br, w1, w3, w2)


def kernel(x_prompt, x_sample, state_conv_a, state_conv_c, state_pool_d, c_prompt, c_sample,
           w_ada, b_ada, g_mix, w_in, conv_a_w, ln_v_g, ln_v_b, w_s, b_s, conv_c_w, conv_c_b,
           ln_c_g, ln_c_b, pool_w, pool_scale, g_out, w_out, g_ffn, w1_dense, w3_dense, w2_dense,
           w_router, b_router, w1_moe, w3_moe, w2_moe, g_final):
    p = dict(g_mix=g_mix, w_in=w_in, conv_a_w=conv_a_w, ln_v_g=ln_v_g, ln_v_b=ln_v_b, w_s=w_s,
             b_s=b_s, conv_c_w=conv_c_w, conv_c_b=conv_c_b, ln_c_g=ln_c_g, ln_c_b=ln_c_b,
             pool_w=pool_w, pool_scale=pool_scale, g_out=g_out, w_out=w_out, g_ffn=g_ffn,
             w1_dense=w1_dense, w3_dense=w3_dense, w2_dense=w2_dense, w_router=w_router,
             b_router=b_router, w1_moe=w1_moe, w3_moe=w3_moe, w2_moe=w2_moe, g_final=g_final)
    n_p, seq, d = x_prompt.shape
    n_s, dec_seq, _ = x_sample.shape
    assert dec_seq == 1 and d == D_MODEL

    mod_p, mod_s = _ada(c_prompt, c_sample, w_ada, b_ada)
    kmajor = lambda s: jnp.transpose(s, (0, 2, 1, 3))
    sa_t, sc_t, sd_t = kmajor(state_conv_a), kmajor(state_conv_c), kmajor(state_pool_d)

    xp = x_prompt
    xs = x_sample.reshape(n_s, d)
    states_p = [[], [], [], []]
    states_s = [[], [], [], []]
    for l in range(DEPTH):
        last = l == DEPTH - 1
        xp, *st_p = _mix_prompt(xp, mod_p, l, p, 0)
        xs, *st_s = _mix_sample(xs, mod_s, sa_t, sc_t, sd_t, l, p, PAST_LEN)
        for acc, s in zip(states_p, st_p):
            acc.append(s)
        for acc, s in zip(states_s, st_s):
            acc.append(s)
        xp = _ffn(xp.reshape(n_p * seq, d), mod_p, l, p, seq, last).reshape(n_p, seq, d)
        xs = _ffn(xs, mod_s, l, p, 1, last)

    a_p, c_p, d_p, v_p = (jnp.stack(s) for s in states_p)
    a_s, c_s, d_s = (kmajor(jnp.stack(s)) for s in states_s[:3])
    v_s = jnp.stack(states_s[3]).reshape(DEPTH, n_s, 1, W_GROUP)
    return (xp, xs.reshape(n_s, 1, d), a_p, c_p, d_p, v_p, a_s, c_s, d_s, v_s)
```

```python
import functools

import jax
import jax.numpy as jnp
from jax import lax
from jax.experimental import pallas as pl
from jax.experimental.pallas import tpu as pltpu

D_MODEL = 1024
DEPTH = 4
W_GROUP = 256
CONV_A = 3
CHUNK = 128
N_HEADS_B = 4
HEAD_B = 64
CONV_C = 31
POOL_WINDOWS = (2, 4, 8, 16)
POOL_MAX = 16
GROUP_D = 64
IN_COLS = 2048
N_EXPERTS = 8
PAST_LEN = 16384
EPS = 1e-6

LANES = 128
HIST_A = 8
HIST_C = 32
HIST_D = 32
TL_MIX = 512
ROW_CHUNK = 64
TM_FFN = 512
SUBLANES = 8
N_SUPER = 4
TM_EXPERT = 256
IDX_ALIGN = 1024
VMEM_LIMIT = 56 * 1024 * 1024
VMEM_LIMIT_EXPERT = 62 * 1024 * 1024

F32 = jnp.float32
BF16 = jnp.bfloat16


def _dot(a, b):
    return jnp.dot(a, b, preferred_element_type=F32)


def _rms(x, g):
    return x * lax.rsqrt(jnp.mean(x * x, axis=-1, keepdims=True) + EPS) * g


def _ln(x, g, b):
    xc = x - jnp.mean(x, axis=-1, keepdims=True)
    var = jnp.mean(xc * xc, axis=-1, keepdims=True)
    return xc * lax.rsqrt(var + EPS) * g + b


def _silu(x):
    return x * jax.nn.sigmoid(x)


def _lane_group_select(vals, shape):
    lane = lax.broadcasted_iota(jnp.int32, shape, 1)
    out = vals[3]
    for g in (2, 1, 0):
        out = jnp.where(lane < (g + 1) * GROUP_D, vals[g], out)
    return out


def _ada_kernel(cp_ref, cs_ref, w_ref, b_ref, op_ref, os_ref):
    w = w_ref[...].astype(BF16)
    b = b_ref[...]
    op_ref[...] = _dot(_silu(cp_ref[...]).astype(BF16), w) + b
    os_ref[...] = _dot(_silu(cs_ref[...]).astype(BF16), w) + b


def _ada(c_prompt, c_sample, w_ada, b_ada):
    n_p, n_s = c_prompt.shape[0], c_sample.shape[0]
    d = D_MODEL
    return pl.pallas_call(
        _ada_kernel,
        grid=(DEPTH, 6),
        in_specs=[
            pl.BlockSpec((n_p, d), lambda l, k: (0, 0)),
            pl.BlockSpec((n_s, d), lambda l, k: (0, 0)),
            pl.BlockSpec((None, d, d), lambda l, k: (l, 0, k)),
            pl.BlockSpec((None, 1, d), lambda l, k: (l, 0, k)),
        ],
        out_specs=[
            pl.BlockSpec((None, None, n_p, d), lambda l, k: (l, k, 0, 0)),
            pl.BlockSpec((None, None, n_s, d), lambda l, k: (l, k, 0, 0)),
        ],
        out_shape=[
            jax.ShapeDtypeStruct((DEPTH, 6, n_p, d), F32),
            jax.ShapeDtypeStruct((DEPTH, 6, n_s, d), F32),
        ],
        compiler_params=pltpu.CompilerParams(
            dimension_semantics=("arbitrary", "arbitrary"), vmem_limit_bytes=VMEM_LIMIT),
        name="ada",
    )(c_prompt, c_sample, w_ada, b_ada.reshape(DEPTH, 1, 6 * d))


def _masked_ws(ws_ref):
    r = lax.broadcasted_iota(jnp.int32, (CHUNK, CHUNK), 0)
    c = lax.broadcasted_iota(jnp.int32, (CHUNK, CHUNK), 1)
    return [jnp.where(c <= r, ws_ref[h], 0.0).astype(BF16) for h in range(N_HEADS_B)]


def _merge_and_project(x, gate, outs, gout_ref, wout_bf):
    merged = jnp.concatenate(
        [_rms(o, gout_ref[i:i + 1, :]) for i, o in enumerate(outs)], axis=1).astype(BF16)
    return x + gate * _dot(merged, wout_bf)


def _mix_prompt_kernel(x_ref, shift_ref, scale_ref, gate_ref, gmix_ref, win_ref, wout_ref,
                       convaw_ref, lnvg_ref, lnvb_ref, ws_ref, bsfull_ref, convcw_ref,
                       convcb_ref, lncg_ref, lncb_ref, poolw_ref, poolscale_ref, gout_ref,
                       xo_ref, na_ref, nc_ref, nd_ref, nv_ref,
                       win_bf, wout_bf, exta, extc, extd, ext2, ext4, ext8,
                       *, tl, start_pos):
    b = pl.program_id(0)
    t = pl.program_id(1)
    last_t = pl.num_programs(1) - 1

    @pl.when(jnp.logical_and(b == 0, t == 0))
    def _cast_weights():
        rows = 128
        def body(i, carry):
            r0 = pl.multiple_of(i * rows, rows)
            win_bf[pl.ds(r0, rows), :] = win_ref[pl.ds(r0, rows), :].astype(BF16)
            wout_bf[pl.ds(r0, rows), :] = wout_ref[pl.ds(r0, rows), :].astype(BF16)
            return carry
        lax.fori_loop(0, D_MODEL // rows, body, 0)

    @pl.when(t == 0)
    def _zero_history():
        exta[0:HIST_A, :] = jnp.zeros((HIST_A, W_GROUP), F32)
        extc[0:HIST_C, :] = jnp.zeros((HIST_C, W_GROUP), F32)
        extd[0:HIST_D, :] = jnp.zeros((HIST_D, W_GROUP), F32)

    x = x_ref[...]
    shift = shift_ref[pl.ds(b, 1), :]
    scale = scale_ref[pl.ds(b, 1), :]
    gate = gate_ref[pl.ds(b, 1), :]
    h = _rms(x, gmix_ref[...]) * (1.0 + scale) + shift
    proj = _dot(h.astype(BF16), win_bf[...])
    a_b, a_c, a_h, b_u, b_v, c_a, c_g, d_p = [
        proj[:, i * W_GROUP:(i + 1) * W_GROUP] for i in range(8)]

    def dwconv(ext_ref, w_ref, hist, width):
        off = hist - (width - 1)
        chunks = []
        for c0 in range(0, tl, ROW_CHUNK):
            acc = None
            for k in range(width):
                s = off + c0 + k
                term = ext_ref[s:s + ROW_CHUNK, :] * w_ref[k:k + 1, :]
                acc = term if acc is None else acc + term
            chunks.append(acc)
        return jnp.concatenate(chunks, axis=0)

    exta[HIST_A:HIST_A + tl, :] = a_c * a_h
    out_a = a_b * dwconv(exta, convaw_ref, HIST_A, CONV_A)

    v_n = _ln(b_v, lnvg_ref[...], lnvb_ref[...])
    v_bf = v_n.astype(BF16)
    wm = _masked_ws(ws_ref)
    lane = lax.broadcasted_iota(jnp.int32, (CHUNK, W_GROUP), 1)
    mixed_chunks = []
    for j in range(tl // CHUNK):
        vc = v_bf[j * CHUNK:(j + 1) * CHUNK, :]
        mixed = _dot(wm[3], vc)
        for hd in (2, 1, 0):
            mixed = jnp.where(lane < (hd + 1) * HEAD_B, _dot(wm[hd], vc), mixed)
        mixed_chunks.append(mixed + bsfull_ref[...])
    out_b = b_u * jnp.concatenate(mixed_chunks, axis=0)

    extc[HIST_C:HIST_C + tl, :] = c_a * jax.nn.sigmoid(c_g)
    y_c = dwconv(extc, convcw_ref, HIST_C, CONV_C) + convcb_ref[...]
    out_c = _silu(_ln(y_c, lncg_ref[...], lncb_ref[...]))

    n = HIST_D + tl
    extd[HIST_D:n, :] = d_p
    ext2[8:n, :] = extd[8:n, :] + extd[7:n - 1, :]
    ext4[16:n, :] = ext2[16:n, :] + ext2[14:n - 2, :]
    ext8[24:n, :] = ext4[24:n, :] + ext4[20:n - 4, :]
    s16 = ext8[HIST_D:n, :] + ext8[HIST_D - 8:n - 8, :]
    sums = [ext2[HIST_D:n, :], ext4[HIST_D:n, :], ext8[HIST_D:n, :], s16]
    shape = (tl, W_GROUP)
    pos1 = start_pos + 1 + t * tl + lax.broadcasted_iota(jnp.int32, shape, 0)
    cnt = _lane_group_select([jnp.minimum(pos1, w) for w in POOL_WINDOWS], shape).astype(F32)
    pooled = _lane_group_select(sums, shape) / cnt - d_p
    out_d = _dot(pooled.astype(BF16), poolw_ref[...].astype(BF16)) * poolscale_ref[...]

    xo_ref[...] = _merge_and_project(x, gate, [out_a, out_b, out_c, out_d], gout_ref, wout_bf[...])

    @pl.when(t == last_t)
    def _emit_state():
        na_ref[...] = exta[HIST_A + tl - (CONV_A - 1):HIST_A + tl, :]
        nc_ref[...] = extc[HIST_C + tl - (CONV_C - 1):HIST_C + tl, :]
        nd_ref[...] = extd[HIST_D + tl - (POOL_MAX - 1):HIST_D + tl, :]
        nv_ref[...] = v_n[tl - CHUNK:tl, :]

    exta[0:HIST_A, :] = exta[tl:tl + HIST_A, :]
    extc[0:HIST_C, :] = extc[tl:tl + HIST_C, :]
    extd[0:HIST_D, :] = extd[tl:tl + HIST_D, :]


def _mix_params(p):
    r3 = lambda a: a.reshape(DEPTH, 1, -1)
    eye = jnp.eye(4, dtype=F32)
    pool_bd = (eye[None, :, None, :, None] * p['pool_w'][:, :, :, None, :]).reshape(
        DEPTH, W_GROUP, W_GROUP)
    return dict(
        gmix=r3(p['g_mix']), w_in=p['w_in'], w_out=p['w_out'], conva=p['conv_a_w'],
        lnvg=r3(p['ln_v_g']), lnvb=r3(p['ln_v_b']), ws=p['w_s'],
        bsfull=jnp.repeat(jnp.swapaxes(p['b_s'], 1, 2), HEAD_B, axis=2),
        ws0=r3(jnp.repeat(p['w_s'][:, :, 0, 0], HEAD_B, axis=1)),
        bs0=r3(jnp.repeat(p['b_s'][:, :, 0], HEAD_B, axis=1)),
        convc=p['conv_c_w'], convcb=r3(p['conv_c_b']), lncg=r3(p['ln_c_g']), lncb=r3(p['ln_c_b']),
        poolw=pool_bd, poolscale=r3(p['pool_scale']), gout=p['g_out'])


def _layer_spec(a, l):
    nd = a.ndim - 1
    return pl.BlockSpec((None,) + a.shape[1:], lambda *_: (l,) + (0,) * nd,
                        pipeline_mode=pl.Buffered(1))


def _const_spec(shape):
    nd = len(shape)
    return pl.BlockSpec(shape, lambda *_: (0,) * nd, pipeline_mode=pl.Buffered(1))


def _mix_prompt(x, mod_p, l, w, start_pos):
    n_b, seq, d = x.shape
    tl = TL_MIX
    assert seq % tl == 0 and tl % CHUNK == 0 and seq >= CHUNK
    mod_spec = lambda k: pl.BlockSpec((None, None, n_b, d), lambda b, t: (l, k, 0, 0))
    weights = [w[k] for k in ('gmix', 'w_in', 'w_out', 'conva', 'lnvg', 'lnvb', 'ws', 'bsfull',
                              'convc', 'convcb', 'lncg', 'lncb', 'poolw', 'poolscale', 'gout')]
    state_spec = lambda r: pl.BlockSpec((None, r, W_GROUP), lambda b, t: (b, 0, 0))
    ext = lambda hist: pltpu.VMEM((hist + tl, W_GROUP), F32)
    return pl.pallas_call(
        functools.partial(_mix_prompt_kernel, tl=tl, start_pos=start_pos),
        grid=(n_b, seq // tl),
        in_specs=[pl.BlockSpec((None, tl, d), lambda b, t: (b, t, 0)),
                  mod_spec(0), mod_spec(1), mod_spec(2)] + [_layer_spec(a, l) for a in weights],
        out_specs=[pl.BlockSpec((None, tl, d), lambda b, t: (b, t, 0)),
                   state_spec(CONV_A - 1), state_spec(CONV_C - 1), state_spec(POOL_MAX - 1),
                   state_spec(CHUNK)],
        out_shape=[jax.ShapeDtypeStruct(x.shape, F32),
                   jax.ShapeDtypeStruct((n_b, CONV_A - 1, W_GROUP), F32),
                   jax.ShapeDtypeStruct((n_b, CONV_C - 1, W_GROUP), F32),
                   jax.ShapeDtypeStruct((n_b, POOL_MAX - 1, W_GROUP), F32),
                   jax.ShapeDtypeStruct((n_b, CHUNK, W_GROUP), F32)],
        scratch_shapes=[pltpu.VMEM((d, IN_COLS), BF16), pltpu.VMEM((d, d), BF16),
                        ext(HIST_A), ext(HIST_C), ext(HIST_D), ext(HIST_D), ext(HIST_D),
                        ext(HIST_D)],
        compiler_params=pltpu.CompilerParams(
            dimension_semantics=("arbitrary", "arbitrary"), vmem_limit_bytes=VMEM_LIMIT),
        name=f"mix_prompt_{l}",
    )(x, mod_p, mod_p, mod_p, *weights)


def _mix_sample_kernel(x_ref, shift_ref, scale_ref, gate_ref, sa_ref, sc_ref, sd_ref, gmix_ref,
                       win_ref, wout_ref, convaw_ref, lnvg_ref, lnvb_ref, ws0_ref, bs0_ref,
                       convcw_ref, convcb_ref, lncg_ref, lncb_ref, poolw_ref, poolscale_ref,
                       gout_ref, xo_ref, na_ref, nc_ref, nd_ref, nv_ref, *, start_pos):
    x = x_ref[...]
    h = _rms(x, gmix_ref[...]) * (1.0 + scale_ref[...]) + shift_ref[...]
    proj = _dot(h.astype(BF16), win_ref[...].astype(BF16))
    a_b, a_c, a_h, b_u, b_v, c_a, c_g, d_p = [
        proj[:, i * W_GROUP:(i + 1) * W_GROUP] for i in range(8)]

    ch = a_c * a_h
    y_a = convaw_ref[CONV_A - 1:CONV_A, :] * ch
    for k in range(CONV_A - 1):
        y_a = y_a + convaw_ref[k:k + 1, :] * sa_ref[k]
    out_a = a_b * y_a
    for k in range(CONV_A - 2):
        na_ref[k] = sa_ref[k + 1]
    na_ref[CONV_A - 2] = ch

    v_n = _ln(b_v, lnvg_ref[...], lnvb_ref[...])
    out_b = b_u * (ws0_ref[...] * v_n + bs0_ref[...])
    nv_ref[...] = v_n

    glu = c_a * jax.nn.sigmoid(c_g)
    y_c = convcw_ref[CONV_C - 1:CONV_C, :] * glu + convcb_ref[...]
    for k in range(CONV_C - 1):
        y_c = y_c + convcw_ref[k:k + 1, :] * sc_ref[k]
    out_c = _silu(_ln(y_c, lncg_ref[...], lncb_ref[...]))
    for k in range(CONV_C - 2):
        nc_ref[k] = sc_ref[k + 1]
    nc_ref[CONV_C - 2] = glu

    hist = POOL_MAX - 1
    run = d_p
    taken = 0
    sums = []
    for w in POOL_WINDOWS:
        while taken < w - 1:
            run = run + sd_ref[hist - 1 - taken]
            taken += 1
        sums.append(run / float(min(start_pos + 1, w)))
    pooled = _lane_group_select(sums, d_p.shape) - d_p
    out_d = _dot(pooled.astype(BF16), poolw_ref[...].astype(BF16)) * poolscale_ref[...]
    for k in range(hist - 1):
        nd_ref[k] = sd_ref[k + 1]
    nd_ref[hist - 1] = d_p

    xo_ref[...] = _merge_and_project(x, gate_ref[...], [out_a, out_b, out_c, out_d], gout_ref,
                                     wout_ref[...].astype(BF16))


def _mix_sample(x, mod_s, sa_t, sc_t, sd_t, l, w, start_pos):
    n, d = x.shape
    assert start_pos + 1 >= POOL_MAX
    mod_spec = lambda k: pl.BlockSpec((None, None, n, d), lambda i: (l, k, 0, 0))
    st_spec = lambda r: pl.BlockSpec((None, r, n, W_GROUP), lambda i: (l, 0, 0, 0))
    weights = [w[k] for k in ('gmix', 'w_in', 'w_out', 'conva', 'lnvg', 'lnvb', 'ws0', 'bs0',
                              'convc', 'convcb', 'lncg', 'lncb', 'poolw', 'poolscale', 'gout')]
    full = lambda shape: pl.BlockSpec(shape, lambda i: (0,) * len(shape))
    return pl.pallas_call(
        functools.partial(_mix_sample_kernel, start_pos=start_pos),
        grid=(1,),
        in_specs=[full((n, d)), mod_spec(0), mod_spec(1), mod_spec(2),
                  st_spec(CONV_A - 1), st_spec(CONV_C - 1), st_spec(POOL_MAX - 1)]
                 + [_layer_spec(a, l) for a in weights],
        out_specs=[full((n, d)), full((CONV_A - 1, n, W_GROUP)), full((CONV_C - 1, n, W_GROUP)),
                   full((POOL_MAX - 1, n, W_GROUP)), full((n, W_GROUP))],
        out_shape=[jax.ShapeDtypeStruct((n, d), F32),
                   jax.ShapeDtypeStruct((CONV_A - 1, n, W_GROUP), F32),
                   jax.ShapeDtypeStruct((CONV_C - 1, n, W_GROUP), F32),
                   jax.ShapeDtypeStruct((POOL_MAX - 1, n, W_GROUP), F32),
                   jax.ShapeDtypeStruct((n, W_GROUP), F32)],
        compiler_params=pltpu.CompilerParams(
            dimension_semantics=("arbitrary",), vmem_limit_bytes=VMEM_LIMIT),
        name=f"mix_sample_{l}",
    )(x, mod_s, mod_s, mod_s, sa_t, sc_t, sd_t, *weights)


def _modulation(refs, rows_per_seq, tm):
    if rows_per_seq == 1:
        return [r[...] for r in refs]
    b = (pl.program_id(0) * tm) // rows_per_seq
    return [r[pl.ds(b, 1), :] for r in refs]


def _ffn_dense_kernel(x_ref, shift_ref, scale_ref, gate_ref, gffn_ref, gfin_ref, w1_ref, w3_ref,
                      w2_ref, o_ref, *, rows_per_seq, tm, final_norm):
    x = x_ref[...]
    shift, scale, gate = _modulation([shift_ref, scale_ref, gate_ref], rows_per_seq, tm)
    h = (_rms(x, gffn_ref[...]) * (1.0 + scale) + shift).astype(BF16)
    act = (_silu(_dot(h, w1_ref[...])) * _dot(h, w3_ref[...])).astype(BF16)
    y = x + gate * _dot(act, w2_ref[...])
    o_ref[...] = _rms(y, gfin_ref[...]) if final_norm else y


def _router_meta(h32, wr_ref, br_ref):
    wr = wr_ref[...]
    h_hi = h32.astype(BF16)
    h_lo = (h32 - h_hi.astype(F32)).astype(BF16)
    w_hi = wr.astype(BF16)
    w_lo = (wr - w_hi.astype(F32)).astype(BF16)
    logits = _dot(h_hi, w_hi) + (_dot(h_lo, w_hi) + _dot(h_hi, w_lo)) + br_ref[...]
    lane = lax.broadcasted_iota(jnp.int32, logits.shape, 1)
    lane_f = lane.astype(F32)
    neg = jnp.float32(-jnp.inf)
    logits = jnp.where(lane < N_EXPERTS, logits, neg)
    m1 = jnp.max(logits, axis=-1, keepdims=True)
    i1 = jnp.min(jnp.where(logits == m1, lane_f, float(LANES)), axis=-1, keepdims=True)
    rest = jnp.where(lane_f == i1, neg, logits)
    m2 = jnp.max(rest, axis=-1, keepdims=True)
    i2 = jnp.min(jnp.where(rest == m2, lane_f, float(LANES)), axis=-1, keepdims=True)
    e = jnp.exp(m2 - m1)
    g1 = 1.0 / (1.0 + e)
    g2 = e / (1.0 + e)
    comb = jnp.where(lane_f == i1, g1, 0.0) + jnp.where(lane_f == i2, g2, 0.0)
    flags = (jnp.where(lane_f == i1 + N_EXPERTS, 1.0, 0.0)
             + jnp.where(lane_f == i2 + N_EXPERTS, 1.0, 0.0))
    return comb + flags


def _mod_specs(mod, l, ks, rows_per_seq, tm):
    d = mod.shape[-1]
    if rows_per_seq == 1:
        return [pl.BlockSpec((None, None, tm, d), lambda t, k=k: (l, k, t, 0)) for k in ks]
    n_seq = mod.shape[2]
    return [pl.BlockSpec((None, None, n_seq, d), lambda t, k=k: (l, k, 0, 0)) for k in ks]


def _ffn_dense(x2d, mod, l, wts, rows_per_seq, final_norm):
    m, d = x2d.shape
    tm = min(TM_FFN, m)
    assert m % tm == 0 and (rows_per_seq == 1 or rows_per_seq % tm == 0)
    i = l // 2
    x_spec = pl.BlockSpec((tm, d), lambda t: (t, 0))
    w1, w3, w2 = wts['w1_dense'], wts['w3_dense'], wts['w2_dense']
    return pl.pallas_call(
        functools.partial(_ffn_dense_kernel, rows_per_seq=rows_per_seq, tm=tm,
                          final_norm=final_norm),
        grid=(m // tm,),
        in_specs=[x_spec] + _mod_specs(mod, l, (3, 4, 5), rows_per_seq, tm)
                 + [_layer_spec(wts['g_ffn'], l), _const_spec(wts['g_final'].shape),
                    _layer_spec(w1, i), _layer_spec(w3, i), _layer_spec(w2, i)],
        out_specs=x_spec,
        out_shape=jax.ShapeDtypeStruct((m, d), F32),
        compiler_params=pltpu.CompilerParams(
            dimension_semantics=("arbitrary",), vmem_limit_bytes=VMEM_LIMIT),
        name=f"ffn_dense_{l}",
    )(x2d, mod, mod, mod, wts['g_ffn'], wts['g_final'], w1, w3, w2)


def _to_token_tiles(ref, val, tm):
    for k in range(SUBLANES):
        ref[pl.ds(k, tm, stride=SUBLANES), :] = val[:, k * LANES:(k + 1) * LANES]


def _from_token_tiles(ref, tm):
    return jnp.concatenate(
        [ref[pl.ds(k, tm, stride=SUBLANES), :] for k in range(SUBLANES)], axis=1)


def _moe_route_kernel(x_ref, shift_ref, scale_ref, gffn_ref, wr_ref, br_ref, h_ref, meta_ref, *,
                      rows_per_seq, tm):
    shift, scale = _modulation([shift_ref, scale_ref], rows_per_seq, tm)
    h32 = _rms(x_ref[...], gffn_ref[...]) * (1.0 + scale) + shift
    _to_token_tiles(h_ref, h32, tm)
    meta_ref[...] = _router_meta(h32, wr_ref, br_ref)


def _moe_route(x2d, mod, l, wts, rows_per_seq):
    m, d = x2d.shape
    tm = min(TM_FFN, m)
    assert m % tm == 0 and (rows_per_seq == 1 or rows_per_seq % tm == 0)
    i = l // 2
    return pl.pallas_call(
        functools.partial(_moe_route_kernel, rows_per_seq=rows_per_seq, tm=tm),
        grid=(m // tm,),
        in_specs=[pl.BlockSpec((tm, d), lambda t: (t, 0))]
                 + _mod_specs(mod, l, (3, 4), rows_per_seq, tm)
                 + [_layer_spec(wts['g_ffn'], l), _layer_spec(wts['w_router'], i),
                    _layer_spec(wts['b_router'], i)],
        out_specs=[pl.BlockSpec((tm * SUBLANES, LANES), lambda t: (t, 0)),
                   pl.BlockSpec((tm, LANES), lambda t: (t, 0))],
        out_shape=[jax.ShapeDtypeStruct((m * SUBLANES, LANES), F32),
                   jax.ShapeDtypeStruct((m, LANES), F32)],
        compiler_params=pltpu.CompilerParams(
            dimension_semantics=("arbitrary",), vmem_limit_bytes=VMEM_LIMIT),
        name=f"moe_route_{l}_{m}",
    )(x2d, mod, mod, wts['g_ffn'], wts['w_router'], wts['b_router'])


def _super_block_pieces(m_p, m_s, s_tok):
    pieces = []
    for k in range(N_SUPER):
        lo, hi = k * s_tok, (k + 1) * s_tok
        ps = []
        if lo < m_p:
            ps.append((0, lo, 0, min(hi, m_p) - lo))
        if hi > m_p:
            s0 = max(lo, m_p)
            ps.append((1, s0 - m_p, s0 - lo, hi - s0))
        pieces.append(ps)
    return pieces


def _moe_expert_kernel(cnt_ref, idx_hbm, g_hbm, hp_hbm, hs_hbm, w1_ref, w3_ref, w2_ref,
                       yp_hbm, ys_hbm, h_scr, y_scr, xbuf, obuf, idx_s, g_s, sem,
                       *, pieces, s_tok, s_pad, tm):
    sb = pl.program_id(0)
    e = pl.program_id(1)
    seg = sb * N_EXPERTS + e
    rows = s_tok * SUBLANES

    def piece_copies(k, to_vmem):
        copies = []
        for j, (grp, src_tok, dst_tok, n) in enumerate(pieces[k]):
            hbm = ((hp_hbm, hs_hbm) if to_vmem else (yp_hbm, ys_hbm))[grp]
            hbm = hbm.at[pl.ds(src_tok * SUBLANES, n * SUBLANES)]
            if to_vmem:
                copies.append(pltpu.make_async_copy(
                    hbm, h_scr.at[pl.ds(dst_tok * SUBLANES, n * SUBLANES)], sem.at[j]))
            else:
                copies.append(pltpu.make_async_copy(
                    y_scr.at[pl.ds(dst_tok * SUBLANES, n * SUBLANES)], hbm, sem.at[2 + j]))
        return copies

    off = pl.multiple_of(seg * s_pad, IDX_ALIGN)
    idx_copy = pltpu.make_async_copy(idx_hbm.at[pl.ds(off, s_pad)], idx_s, sem.at[4])
    g_copy = pltpu.make_async_copy(g_hbm.at[pl.ds(off, s_pad)], g_s, sem.at[5])
    idx_copy.start()
    g_copy.start()

    for k in range(N_SUPER):
        @pl.when(jnp.logical_and(sb == k, e == 0))
        def _load_super_block(k=k):
            for c in piece_copies(k, True):
                c.start()
            zrows = 256
            assert rows % zrows == 0
            def zero(i, carry):
                r0 = pl.multiple_of(i * zrows, zrows)
                y_scr[pl.ds(r0, zrows), :] = jnp.zeros((zrows, LANES), F32)
                return carry
            lax.fori_loop(0, rows // zrows, zero, 0)
            for c in piece_copies(k, True):
                c.wait()

    idx_copy.wait()
    g_copy.wait()

    def tile(i, carry):
        base = i * tm
        for r in range(tm):
            t8 = pl.multiple_of(idx_s[base + r] * SUBLANES, SUBLANES)
            xbuf[r * SUBLANES:(r + 1) * SUBLANES, :] = h_scr[pl.ds(t8, SUBLANES), :]
        x = _from_token_tiles(xbuf, tm).astype(BF16)
        act = (_silu(_dot(x, w1_ref[...])) * _dot(x, w3_ref[...])).astype(BF16)
        _to_token_tiles(obuf, _dot(act, w2_ref[...]), tm)
        for r0 in range(0, tm, SUBLANES):
            upd = []
            for r in range(r0, r0 + SUBLANES):
                t8 = pl.multiple_of(idx_s[base + r] * SUBLANES, SUBLANES)
                o = obuf[r * SUBLANES:(r + 1) * SUBLANES, :]
                upd.append((t8, y_scr[pl.ds(t8, SUBLANES), :] + g_s[base + r] * o))
            for t8, v in upd:
                y_scr[pl.ds(t8, SUBLANES), :] = v
        return carry

    lax.fori_loop(0, (cnt_ref[seg] + tm - 1) // tm, tile, 0)

    for k in range(N_SUPER):
        @pl.when(jnp.logical_and(sb == k, e == N_EXPERTS - 1))
        def _store_super_block(k=k):
            for c in piece_copies(k, False):
                c.start()
            for c in piece_copies(k, False):
                c.wait()


def _moe_experts(counts, idx, gates, h_p, h_s, l, wts, s_tok, s_pad):
    i = l // 2
    tm = TM_EXPERT
    w1, w3, w2 = wts['w1_moe'], wts['w3_moe'], wts['w2_moe']
    w_spec = lambda a: pl.BlockSpec((None, None) + a.shape[2:], lambda sb, e, cnt: (i, e, 0, 0))
    any_spec = pl.BlockSpec(memory_space=pl.ANY)
    rows = s_tok * SUBLANES
    pieces = _super_block_pieces(h_p.shape[0] // SUBLANES, h_s.shape[0] // SUBLANES, s_tok)
    return pl.pallas_call(
        functools.partial(_moe_expert_kernel, pieces=pieces, s_tok=s_tok, s_pad=s_pad, tm=tm),
        grid_spec=pltpu.PrefetchScalarGridSpec(
            num_scalar_prefetch=1,
            grid=(N_SUPER, N_EXPERTS),
            in_specs=[any_spec] * 4 + [w_spec(w1), w_spec(w3), w_spec(w2)],
            out_specs=[any_spec, any_spec],
            scratch_shapes=[pltpu.VMEM((rows, LANES), F32), pltpu.VMEM((rows, LANES), F32),
                            pltpu.VMEM((tm * SUBLANES, LANES), F32),
                            pltpu.VMEM((tm * SUBLANES, LANES), F32),
                            pltpu.SMEM((s_pad,), jnp.int32), pltpu.SMEM((s_pad,), F32),
                            pltpu.SemaphoreType.DMA((6,))]),
        out_shape=[jax.ShapeDtypeStruct(h_p.shape, F32), jax.ShapeDtypeStruct(h_s.shape, F32)],
        compiler_params=pltpu.CompilerParams(
            dimension_semantics=("arbitrary", "arbitrary"), vmem_limit_bytes=VMEM_LIMIT_EXPERT),
        name=f"moe_experts_{l}",
    )(counts, idx, gates, h_p, h_s, w1, w3, w2)


def _moe_residual_kernel(x_ref, gate_ref, gfin_ref, y_ref, o_ref, *, rows_per_seq, tm, final_norm):
    (gate,) = _modulation([gate_ref], rows_per_seq, tm)
    y = x_ref[...] + gate * _from_token_tiles(y_ref, tm)
    o_ref[...] = _rms(y, gfin_ref[...]) if final_norm else y


def _moe_residual(x2d, mod, y, l, wts, rows_per_seq, final_norm):
    m, d = x2d.shape
    tm = min(TM_FFN, m)
    assert m % tm == 0
    x_spec = pl.BlockSpec((tm, d), lambda t: (t, 0))
    return pl.pallas_call(
        functools.partial(_moe_residual_kernel, rows_per_seq=rows_per_seq, tm=tm,
                          final_norm=final_norm),
        grid=(m // tm,),
        in_specs=[x_spec] + _mod_specs(mod, l, (5,), rows_per_seq, tm)
                 + [_const_spec(wts['g_final'].shape),
                    pl.BlockSpec((tm * SUBLANES, LANES), lambda t: (t, 0))],
        out_specs=x_spec,
        out_shape=jax.ShapeDtypeStruct((m, d), F32),
        compiler_params=pltpu.CompilerParams(
            dimension_semantics=("arbitrary",), vmem_limit_bytes=VMEM_LIMIT),
        name=f"moe_residual_{l}_{m}",
    )(x2d, mod, wts['g_final'], y)


def _ffn_moe(xp2d, xs, mod_p, mod_s, l, wts, seq, final_norm):
    m_p, m_s = xp2d.shape[0], xs.shape[0]
    n_tok = m_p + m_s
    s_tok = n_tok // N_SUPER
    assert s_tok * N_SUPER == n_tok and s_tok % SUBLANES == 0
    s_pad = -(-(-(-s_tok // TM_EXPERT) * TM_EXPERT) // IDX_ALIGN) * IDX_ALIGN

    h_p, meta_p = _moe_route(xp2d, mod_p, l, wts, seq)
    h_s, meta_s = _moe_route(xs, mod_s, l, wts, 1)

    meta = jnp.concatenate([meta_p[:, :2 * N_EXPERTS], meta_s[:, :2 * N_EXPERTS]], axis=0)
    per_seg = lambda a: a.reshape(N_SUPER, s_tok, N_EXPERTS).transpose(0, 2, 1)
    gate = per_seg(meta[:, :N_EXPERTS])
    unsel = 1 - per_seg(meta[:, N_EXPERTS:]).astype(jnp.int32)
    tok = lax.broadcasted_iota(jnp.int32, unsel.shape, 2)
    _, idx, gate = lax.sort((unsel, tok, gate), dimension=2, num_keys=1, is_stable=True)
    counts = (s_tok - jnp.sum(unsel, axis=2)).reshape(-1)
    pad = lambda a: jnp.pad(a, ((0, 0), (0, 0), (0, s_pad - s_tok))).reshape(-1)

    y_p, y_s = _moe_experts(counts, pad(idx), pad(gate), h_p, h_s, l, wts, s_tok, s_pad)
    xp_new = _moe_residual(xp2d, mod_p, y_p, l, wts, seq, final_norm)
    xs_new = _moe_residual(xs, mod_s, y_s, l, wts, 1, final_norm)
    return xp_new, xs_new


def kernel(x_prompt, x_sample, state_conv_a, state_conv_c, state_pool_d, c_prompt, c_sample,
           w_ada, b_ada, g_mix, w_in, conv_a_w, ln_v_g, ln_v_b, w_s, b_s, conv_c_w, conv_c_b,
           ln_c_g, ln_c_b, pool_w, pool_scale, g_out, w_out, g_ffn, w1_dense, w3_dense, w2_dense,
           w_router, b_router, w1_moe, w3_moe, w2_moe, g_final):
    p = dict(g_mix=g_mix, w_in=w_in, conv_a_w=conv_a_w, ln_v_g=ln_v_g, ln_v_b=ln_v_b, w_s=w_s,
             b_s=b_s, conv_c_w=conv_c_w, conv_c_b=conv_c_b, ln_c_g=ln_c_g, ln_c_b=ln_c_b,
             pool_w=pool_w, pool_scale=pool_scale, g_out=g_out, w_out=w_out, g_ffn=g_ffn,
             w1_dense=w1_dense, w3_dense=w3_dense, w2_dense=w2_dense, w_router=w_router,
             b_router=b_router, w1_moe=w1_moe, w3_moe=w3_moe, w2_moe=w2_moe, g_final=g_final)
    n_p, seq, d = x_prompt.shape
    n_s, dec_seq, _ = x_sample.shape
    assert dec_seq == 1 and d == D_MODEL

    mod_p, mod_s = _ada(c_prompt, c_sample, w_ada, b_ada)
    kmajor = lambda s: jnp.transpose(s, (0, 2, 1, 3))
    sa_t, sc_t, sd_t = kmajor(state_conv_a), kmajor(state_conv_c), kmajor(state_pool_d)
    mixw = _mix_params(p)
    pad_e = LANES - N_EXPERTS
    wts = dict(
        g_ffn=g_ffn.reshape(DEPTH, 1, d), g_final=g_final.reshape(1, d),
        w1_dense=w1_dense.astype(BF16), w3_dense=w3_dense.astype(BF16),
        w2_dense=w2_dense.astype(BF16),
        w_router=jnp.pad(w_router, ((0, 0), (0, 0), (0, pad_e))),
        b_router=jnp.pad(b_router, ((0, 0), (0, pad_e))).reshape(-1, 1, LANES),
        w1_moe=w1_moe.astype(BF16), w3_moe=w3_moe.astype(BF16), w2_moe=w2_moe.astype(BF16))

    xp = x_prompt
    xs = x_sample.reshape(n_s, d)
    states_p = [[], [], [], []]
    states_s = [[], [], [], []]
    for l in range(DEPTH):
        last = l == DEPTH - 1
        xp, *st_p = _mix_prompt(xp, mod_p, l, mixw, 0)
        xs, *st_s = _mix_sample(xs, mod_s, sa_t, sc_t, sd_t, l, mixw, PAST_LEN)
        for acc, s in zip(states_p, st_p):
            acc.append(s)
        for acc, s in zip(states_s, st_s):
            acc.append(s)
        xp2d = xp.reshape(n_p * seq, d)
        if l % 2 == 0:
            xp2d = _ffn_dense(xp2d, mod_p, l, wts, seq, last)
            xs = _ffn_dense(xs, mod_s, l, wts, 1, last)
        else:
            xp2d, xs = _ffn_moe(xp2d, xs, mod_p, mod_s, l, wts, seq, last)
        xp = xp2d.reshape(n_p, seq, d)

    a_p, c_p, d_p, v_p = (jnp.stack(s) for s in states_p)
    a_s, c_s, d_s = (kmajor(jnp.stack(s)) for s in states_s[:3])
    v_s = jnp.stack(states_s[3]).reshape(DEPTH, n_s, 1, W_GROUP)
    return (xp, xs.reshape(n_s, 1, d), a_p, c_p, d_p, v_p, a_s, c_s, d_s, v_s)
```

```python
import functools

import jax
import jax.numpy as jnp
from jax import lax
from jax.experimental import pallas as pl
from jax.experimental.pallas import tpu as pltpu

D_MODEL = 1024
DEPTH = 4
W_GROUP = 256
CONV_A = 3
CHUNK = 128
N_HEADS_B = 4
HEAD_B = 64
CONV_C = 31
POOL_WINDOWS = (2, 4, 8, 16)
POOL_MAX = 16
GROUP_D = 64
IN_COLS = 2048
N_EXPERTS = 8
PAST_LEN = 16384
EPS = 1e-6

LANES = 128
LANE_TILES = W_GROUP // LANES
HIST_A = 8
HIST_C = 32
HIST_D = 32
TL_MIX = 512
ROW_CHUNK = 64
TM_FFN = 512
SUBLANES = 8
N_SUPER = 4
TM_EXPERT = 256
IDX_ALIGN = 1024
VMEM_LIMIT = 56 * 1024 * 1024
VMEM_LIMIT_EXPERT = 62 * 1024 * 1024

F32 = jnp.float32
BF16 = jnp.bfloat16


def _dot(a, b):
    return jnp.dot(a, b, preferred_element_type=F32)


def _rms(x, g):
    return x * lax.rsqrt(jnp.mean(x * x, axis=-1, keepdims=True) + EPS) * g


def _ln(x, g, b):
    xc = x - jnp.mean(x, axis=-1, keepdims=True)
    var = jnp.mean(xc * xc, axis=-1, keepdims=True)
    return xc * lax.rsqrt(var + EPS) * g + b


def _silu(x):
    return x * jax.nn.sigmoid(x)


def _lane_group_select(vals, shape):
    lane = lax.broadcasted_iota(jnp.int32, shape, 1)
    out = vals[3]
    for g in (2, 1, 0):
        out = jnp.where(lane < (g + 1) * GROUP_D, vals[g], out)
    return out


def _ada_kernel(cp_ref, cs_ref, w_ref, b_ref, op_ref, os_ref):
    w = w_ref[...].astype(BF16)
    b = b_ref[...]
    op_ref[...] = _dot(_silu(cp_ref[...]).astype(BF16), w) + b
    os_ref[...] = _dot(_silu(cs_ref[...]).astype(BF16), w) + b


def _ada(c_prompt, c_sample, w_ada, b_ada):
    n_p, n_s = c_prompt.shape[0], c_sample.shape[0]
    d = D_MODEL
    return pl.pallas_call(
        _ada_kernel,
        grid=(DEPTH, 6),
        in_specs=[
            pl.BlockSpec((n_p, d), lambda l, k: (0, 0)),
            pl.BlockSpec((n_s, d), lambda l, k: (0, 0)),
            pl.BlockSpec((None, d, d), lambda l, k: (l, 0, k)),
            pl.BlockSpec((None, 1, d), lambda l, k: (l, 0, k)),
        ],
        out_specs=[
            pl.BlockSpec((None, None, n_p, d), lambda l, k: (l, k, 0, 0)),
            pl.BlockSpec((None, None, n_s, d), lambda l, k: (l, k, 0, 0)),
        ],
        out_shape=[
            jax.ShapeDtypeStruct((DEPTH, 6, n_p, d), F32),
            jax.ShapeDtypeStruct((DEPTH, 6, n_s, d), F32),
        ],
        compiler_params=pltpu.CompilerParams(
            dimension_semantics=("arbitrary", "arbitrary"), vmem_limit_bytes=VMEM_LIMIT),
        name="ada",
    )(c_prompt, c_sample, w_ada, b_ada.reshape(DEPTH, 1, 6 * d))


def _masked_ws(ws_ref):
    r = lax.broadcasted_iota(jnp.int32, (CHUNK, CHUNK), 0)
    c = lax.broadcasted_iota(jnp.int32, (CHUNK, CHUNK), 1)
    return [jnp.where(c <= r, ws_ref[h], 0.0).astype(BF16) for h in range(N_HEADS_B)]


def _ext_rows(ref, j, row0, n):
    return ref[pl.ds(row0 * LANE_TILES + j, n, stride=LANE_TILES), :]


def _ext_store(ref, row0, val):
    for j in range(LANE_TILES):
        ref[pl.ds(row0 * LANE_TILES + j, val.shape[0], stride=LANE_TILES), :] = (
            val[:, j * LANES:(j + 1) * LANES])


def _ext_load(ref, row0, n):
    return jnp.concatenate([_ext_rows(ref, j, row0, n) for j in range(LANE_TILES)], axis=1)


def _merge_and_project(x, gate, outs, gout_ref, wout_bf):
    merged = jnp.concatenate(
        [_rms(o, gout_ref[i:i + 1, :]) for i, o in enumerate(outs)], axis=1).astype(BF16)
    return x + gate * _dot(merged, wout_bf)


def _mix_prompt_kernel(x_ref, shift_ref, scale_ref, gate_ref, gmix_ref, win_ref, wout_ref,
                       convaw_ref, lnvg_ref, lnvb_ref, ws_ref, bsfull_ref, convcw_ref,
                       convcb_ref, lncg_ref, lncb_ref, poolw_ref, poolscale_ref, gout_ref,
                       xo_ref, na_ref, nc_ref, nd_ref, nv_ref,
                       win_bf, wout_bf, exta, extc, extd, ext2, ext4, ext8,
                       *, tl, start_pos):
    b = pl.program_id(0)
    t = pl.program_id(1)
    last_t = pl.num_programs(1) - 1

    @pl.when(jnp.logical_and(b == 0, t == 0))
    def _cast_weights():
        rows = 128
        def body(i, carry):
            r0 = pl.multiple_of(i * rows, rows)
            win_bf[pl.ds(r0, rows), :] = win_ref[pl.ds(r0, rows), :].astype(BF16)
            wout_bf[pl.ds(r0, rows), :] = wout_ref[pl.ds(r0, rows), :].astype(BF16)
            return carry
        lax.fori_loop(0, D_MODEL // rows, body, 0)

    @pl.when(t == 0)
    def _zero_history():
        for ref, hist in ((exta, HIST_A), (extc, HIST_C), (extd, HIST_D)):
            ref[0:hist * LANE_TILES, :] = jnp.zeros((hist * LANE_TILES, LANES), F32)

    x = x_ref[...]
    shift = shift_ref[pl.ds(b, 1), :]
    scale = scale_ref[pl.ds(b, 1), :]
    gate = gate_ref[pl.ds(b, 1), :]
    h = _rms(x, gmix_ref[...] * (1.0 + scale)) + shift
    proj = _dot(h.astype(BF16), win_bf[...])
    a_b, a_c, a_h, b_u, b_v, c_a, c_g, d_p = [
        proj[:, i * W_GROUP:(i + 1) * W_GROUP] for i in range(8)]

    def dwconv(ext_ref, w_ref, hist, width):
        off = hist - (width - 1)
        halves = []
        for j in range(LANE_TILES):
            chunks = []
            for c0 in range(0, tl, ROW_CHUNK):
                acc = None
                for k in range(width):
                    term = (_ext_rows(ext_ref, j, off + c0 + k, ROW_CHUNK)
                            * w_ref[k:k + 1, j * LANES:(j + 1) * LANES])
                    acc = term if acc is None else acc + term
                chunks.append(acc)
            halves.append(jnp.concatenate(chunks, axis=0))
        return jnp.concatenate(halves, axis=1)

    _ext_store(exta, HIST_A, a_c * a_h)
    out_a = a_b * dwconv(exta, convaw_ref, HIST_A, CONV_A)

    v_n = _ln(b_v, lnvg_ref[...], lnvb_ref[...])
    v_bf = v_n.astype(BF16)
    wm = _masked_ws(ws_ref)
    lane = lax.broadcasted_iota(jnp.int32, (CHUNK, W_GROUP), 1)
    mixed_chunks = []
    for j in range(tl // CHUNK):
        vc = v_bf[j * CHUNK:(j + 1) * CHUNK, :]
        mixed = _dot(wm[3], vc)
        for hd in (2, 1, 0):
            mixed = jnp.where(lane < (hd + 1) * HEAD_B, _dot(wm[hd], vc), mixed)
        mixed_chunks.append(mixed + bsfull_ref[...])
    out_b = b_u * jnp.concatenate(mixed_chunks, axis=0)

    _ext_store(extc, HIST_C, c_a * jax.nn.sigmoid(c_g))
    y_c = dwconv(extc, convcw_ref, HIST_C, CONV_C) + convcb_ref[...]
    out_c = _silu(_ln(y_c, lncg_ref[...], lncb_ref[...]))

    n = HIST_D + tl
    _ext_store(extd, HIST_D, d_p)
    bufs = (extd, ext2, ext4, ext8)
    pos1 = start_pos + 1 + t * tl + lax.broadcasted_iota(jnp.int32, (tl, LANES), 0)
    low_group = lax.broadcasted_iota(jnp.int32, (tl, LANES), 1) < GROUP_D
    means = []
    for j in range(LANE_TILES):
        levels = 2 * (j + 1)
        for lv in range(levels - 1):
            first = 8 * (lv + 1)
            bufs[lv + 1][pl.ds(first * LANE_TILES + j, n - first, stride=LANE_TILES), :] = (
                _ext_rows(bufs[lv], j, first, n - first)
                + _ext_rows(bufs[lv], j, first - (1 << lv), n - first))
        prev = bufs[levels - 1]
        s_lo = _ext_rows(prev, j, HIST_D, tl)
        s_hi = s_lo + _ext_rows(prev, j, HIST_D - (1 << (levels - 1)), tl)
        w_lo, w_hi = POOL_WINDOWS[2 * j], POOL_WINDOWS[2 * j + 1]
        cnt = jnp.where(low_group, jnp.minimum(pos1, w_lo), jnp.minimum(pos1, w_hi)).astype(F32)
        means.append(jnp.where(low_group, s_lo, s_hi) / cnt)
    pooled = jnp.concatenate(means, axis=1) - d_p
    out_d = _dot(pooled.astype(BF16), poolw_ref[...].astype(BF16)) * poolscale_ref[...]

    xo_ref[...] = _merge_and_project(x, gate, [out_a, out_b, out_c, out_d], gout_ref, wout_bf[...])

    @pl.when(t == last_t)
    def _emit_state():
        na_ref[...] = _ext_load(exta, HIST_A + tl - (CONV_A - 1), CONV_A - 1)
        nc_ref[...] = _ext_load(extc, HIST_C + tl - (CONV_C - 1), CONV_C - 1)
        nd_ref[...] = _ext_load(extd, HIST_D + tl - (POOL_MAX - 1), POOL_MAX - 1)
        nv_ref[...] = v_n[tl - CHUNK:tl, :]

    for ref, hist in ((exta, HIST_A), (extc, HIST_C), (extd, HIST_D)):
        ref[0:hist * LANE_TILES, :] = ref[tl * LANE_TILES:(tl + hist) * LANE_TILES, :]


def _mix_params(p):
    r3 = lambda a: a.reshape(DEPTH, 1, -1)
    eye = jnp.eye(4, dtype=F32)
    pool_bd = (eye[None, :, None, :, None] * p['pool_w'][:, :, :, None, :]).reshape(
        DEPTH, W_GROUP, W_GROUP)
    return dict(
        gmix=r3(p['g_mix']), w_in=p['w_in'], w_out=p['w_out'], conva=p['conv_a_w'],
        lnvg=r3(p['ln_v_g']), lnvb=r3(p['ln_v_b']), ws=p['w_s'],
        bsfull=jnp.repeat(jnp.swapaxes(p['b_s'], 1, 2), HEAD_B, axis=2),
        ws0=r3(jnp.repeat(p['w_s'][:, :, 0, 0], HEAD_B, axis=1)),
        bs0=r3(jnp.repeat(p['b_s'][:, :, 0], HEAD_B, axis=1)),
        convc=p['conv_c_w'], convcb=r3(p['conv_c_b']), lncg=r3(p['ln_c_g']), lncb=r3(p['ln_c_b']),
        poolw=pool_bd, poolscale=r3(p['pool_scale']), gout=p['g_out'])


def _layer_spec(a, l):
    nd = a.ndim - 1
    return pl.BlockSpec((None,) + a.shape[1:], lambda *_: (l,) + (0,) * nd,
                        pipeline_mode=pl.Buffered(1))


def _const_spec(shape):
    nd = len(shape)
    return pl.BlockSpec(shape, lambda *_: (0,) * nd, pipeline_mode=pl.Buffered(1))


def _mix_prompt(x, mod_p, l, w, start_pos):
    n_b, seq, d = x.shape
    tl = TL_MIX
    assert seq % tl == 0 and tl % CHUNK == 0 and seq >= CHUNK
    mod_spec = lambda k: pl.BlockSpec((None, None, n_b, d), lambda b, t: (l, k, 0, 0))
    weights = [w[k] for k in ('gmix', 'w_in', 'w_out', 'conva', 'lnvg', 'lnvb', 'ws', 'bsfull',
                              'convc', 'convcb', 'lncg', 'lncb', 'poolw', 'poolscale', 'gout')]
    state_spec = lambda r: pl.BlockSpec((None, r, W_GROUP), lambda b, t: (b, 0, 0))
    ext = lambda hist: pltpu.VMEM(((hist + tl) * LANE_TILES, LANES), F32)
    return pl.pallas_call(
        functools.partial(_mix_prompt_kernel, tl=tl, start_pos=start_pos),
        grid=(n_b, seq // tl),
        in_specs=[pl.BlockSpec((None, tl, d), lambda b, t: (b, t, 0)),
                  mod_spec(0), mod_spec(1), mod_spec(2)] + [_layer_spec(a, l) for a in weights],
        out_specs=[pl.BlockSpec((None, tl, d), lambda b, t: (b, t, 0)),
                   state_spec(CONV_A - 1), state_spec(CONV_C - 1), state_spec(POOL_MAX - 1),
                   state_spec(CHUNK)],
        out_shape=[jax.ShapeDtypeStruct(x.shape, F32),
                   jax.ShapeDtypeStruct((n_b, CONV_A - 1, W_GROUP), F32),
                   jax.ShapeDtypeStruct((n_b, CONV_C - 1, W_GROUP), F32),
                   jax.ShapeDtypeStruct((n_b, POOL_MAX - 1, W_GROUP), F32),
                   jax.ShapeDtypeStruct((n_b, CHUNK, W_GROUP), F32)],
        scratch_shapes=[pltpu.VMEM((d, IN_COLS), BF16), pltpu.VMEM((d, d), BF16),
                        ext(HIST_A), ext(HIST_C), ext(HIST_D), ext(HIST_D), ext(HIST_D),
                        ext(HIST_D)],
        compiler_params=pltpu.CompilerParams(
            dimension_semantics=("arbitrary", "arbitrary"), vmem_limit_bytes=VMEM_LIMIT),
        name=f"mix_prompt_{l}",
    )(x, mod_p, mod_p, mod_p, *weights)


def _mix_sample_kernel(x_ref, shift_ref, scale_ref, gate_ref, sa_ref, sc_ref, sd_ref, gmix_ref,
                       win_ref, wout_ref, convaw_ref, lnvg_ref, lnvb_ref, ws0_ref, bs0_ref,
                       convcw_ref, convcb_ref, lncg_ref, lncb_ref, poolw_ref, poolscale_ref,
                       gout_ref, xo_ref, na_ref, nc_ref, nd_ref, nv_ref, *, start_pos):
    x = x_ref[...]
    h = _rms(x, gmix_ref[...]) * (1.0 + scale_ref[...]) + shift_ref[...]
    proj = _dot(h.astype(BF16), win_ref[...].astype(BF16))
    a_b, a_c, a_h, b_u, b_v, c_a, c_g, d_p = [
        proj[:, i * W_GROUP:(i + 1) * W_GROUP] for i in range(8)]

    ch = a_c * a_h
    y_a = convaw_ref[CONV_A - 1:CONV_A, :] * ch
    for k in range(CONV_A - 1):
        y_a = y_a + convaw_ref[k:k + 1, :] * sa_ref[k]
    out_a = a_b * y_a
    for k in range(CONV_A - 2):
        na_ref[k] = sa_ref[k + 1]
    na_ref[CONV_A - 2] = ch

    v_n = _ln(b_v, lnvg_ref[...], lnvb_ref[...])
    out_b = b_u * (ws0_ref[...] * v_n + bs0_ref[...])
    nv_ref[...] = v_n

    glu = c_a * jax.nn.sigmoid(c_g)
    y_c = convcw_ref[CONV_C - 1:CONV_C, :] * glu + convcb_ref[...]
    for k in range(CONV_C - 1):
        y_c = y_c + convcw_ref[k:k + 1, :] * sc_ref[k]
    out_c = _silu(_ln(y_c, lncg_ref[...], lncb_ref[...]))
    for k in range(CONV_C - 2):
        nc_ref[k] = sc_ref[k + 1]
    nc_ref[CONV_C - 2] = glu

    hist = POOL_MAX - 1
    run = d_p
    taken = 0
    sums = []
    for w in POOL_WINDOWS:
        while taken < w - 1:
            run = run + sd_ref[hist - 1 - taken]
            taken += 1
        sums.append(run / float(min(start_pos + 1, w)))
    pooled = _lane_group_select(sums, d_p.shape) - d_p
    out_d = _dot(pooled.astype(BF16), poolw_ref[...].astype(BF16)) * poolscale_ref[...]
    for k in range(hist - 1):
        nd_ref[k] = sd_ref[k + 1]
    nd_ref[hist - 1] = d_p

    xo_ref[...] = _merge_and_project(x, gate_ref[...], [out_a, out_b, out_c, out_d], gout_ref,
                                     wout_ref[...].astype(BF16))


def _mix_sample(x, mod_s, sa_t, sc_t, sd_t, l, w, start_pos):
    n, d = x.shape
    assert start_pos + 1 >= POOL_MAX
    mod_spec = lambda k: pl.BlockSpec((None, None, n, d), lambda i: (l, k, 0, 0))
    st_spec = lambda r: pl.BlockSpec((None, r, n, W_GROUP), lambda i: (l, 0, 0, 0))
    weights = [w[k] for k in ('gmix', 'w_in', 'w_out', 'conva', 'lnvg', 'lnvb', 'ws0', 'bs0',
                              'convc', 'convcb', 'lncg', 'lncb', 'poolw', 'poolscale', 'gout')]
    full = lambda shape: pl.BlockSpec(shape, lambda i: (0,) * len(shape))
    return pl.pallas_call(
        functools.partial(_mix_sample_kernel, start_pos=start_pos),
        grid=(1,),
        in_specs=[full((n, d)), mod_spec(0), mod_spec(1), mod_spec(2),
                  st_spec(CONV_A - 1), st_spec(CONV_C - 1), st_spec(POOL_MAX - 1)]
                 + [_layer_spec(a, l) for a in weights],
        out_specs=[full((n, d)), full((CONV_A - 1, n, W_GROUP)), full((CONV_C - 1, n, W_GROUP)),
                   full((POOL_MAX - 1, n, W_GROUP)), full((n, W_GROUP))],
        out_shape=[jax.ShapeDtypeStruct((n, d), F32),
                   jax.ShapeDtypeStruct((CONV_A - 1, n, W_GROUP), F32),
                   jax.ShapeDtypeStruct((CONV_C - 1, n, W_GROUP), F32),
                   jax.ShapeDtypeStruct((POOL_MAX - 1, n, W_GROUP), F32),
                   jax.ShapeDtypeStruct((n, W_GROUP), F32)],
        compiler_params=pltpu.CompilerParams(
            dimension_semantics=("arbitrary",), vmem_limit_bytes=VMEM_LIMIT),
        name=f"mix_sample_{l}",
    )(x, mod_s, mod_s, mod_s, sa_t, sc_t, sd_t, *weights)


def _modulation(refs, rows_per_seq, tm):
    if rows_per_seq == 1:
        return [r[...] for r in refs]
    b = (pl.program_id(0) * tm) // rows_per_seq
    return [r[pl.ds(b, 1), :] for r in refs]


def _ffn_dense_kernel(x_ref, shift_ref, scale_ref, gate_ref, gffn_ref, gfin_ref, w1_ref, w3_ref,
                      w2_ref, o_ref, *, rows_per_seq, tm, final_norm):
    x = x_ref[...]
    shift, scale, gate = _modulation([shift_ref, scale_ref, gate_ref], rows_per_seq, tm)
    h = (_rms(x, gffn_ref[...]) * (1.0 + scale) + shift).astype(BF16)
    act = (_silu(_dot(h, w1_ref[...])) * _dot(h, w3_ref[...])).astype(BF16)
    y = x + gate * _dot(act, w2_ref[...])
    o_ref[...] = _rms(y, gfin_ref[...]) if final_norm else y


def _router_meta(h32, wr_ref, br_ref):
    wr = wr_ref[...]
    h_hi = h32.astype(BF16)
    h_lo = (h32 - h_hi.astype(F32)).astype(BF16)
    w_hi = wr.astype(BF16)
    w_lo = (wr - w_hi.astype(F32)).astype(BF16)
    logits = _dot(h_hi, w_hi) + (_dot(h_lo, w_hi) + _dot(h_hi, w_lo)) + br_ref[...]
    lane = lax.broadcasted_iota(jnp.int32, logits.shape, 1)
    lane_f = lane.astype(F32)
    neg = jnp.float32(-jnp.inf)
    logits = jnp.where(lane < N_EXPERTS, logits, neg)
    m1 = jnp.max(logits, axis=-1, keepdims=True)
    i1 = jnp.min(jnp.where(logits == m1, lane_f, float(LANES)), axis=-1, keepdims=True)
    rest = jnp.where(lane_f == i1, neg, logits)
    m2 = jnp.max(rest, axis=-1, keepdims=True)
    i2 = jnp.min(jnp.where(rest == m2, lane_f, float(LANES)), axis=-1, keepdims=True)
    e = jnp.exp(m2 - m1)
    g1 = 1.0 / (1.0 + e)
    g2 = e / (1.0 + e)
    comb = jnp.where(lane_f == i1, g1, 0.0) + jnp.where(lane_f == i2, g2, 0.0)
    flags = (jnp.where(lane_f == i1 + N_EXPERTS, 1.0, 0.0)
             + jnp.where(lane_f == i2 + N_EXPERTS, 1.0, 0.0))
    return comb + flags


def _mod_specs(mod, l, ks, rows_per_seq, tm):
    d = mod.shape[-1]
    if rows_per_seq == 1:
        return [pl.BlockSpec((None, None, tm, d), lambda t, k=k: (l, k, t, 0)) for k in ks]
    n_seq = mod.shape[2]
    return [pl.BlockSpec((None, None, n_seq, d), lambda t, k=k: (l, k, 0, 0)) for k in ks]


def _ffn_dense(x2d, mod, l, wts, rows_per_seq, final_norm):
    m, d = x2d.shape
    tm = min(TM_FFN, m)
    assert m % tm == 0 and (rows_per_seq == 1 or rows_per_seq % tm == 0)
    i = l // 2
    x_spec = pl.BlockSpec((tm, d), lambda t: (t, 0))
    w1, w3, w2 = wts['w1_dense'], wts['w3_dense'], wts['w2_dense']
    return pl.pallas_call(
        functools.partial(_ffn_dense_kernel, rows_per_seq=rows_per_seq, tm=tm,
                          final_norm=final_norm),
        grid=(m // tm,),
        in_specs=[x_spec] + _mod_specs(mod, l, (3, 4, 5), rows_per_seq, tm)
                 + [_layer_spec(wts['g_ffn'], l), _const_spec(wts['g_final'].shape),
                    _layer_spec(w1, i), _layer_spec(w3, i), _layer_spec(w2, i)],
        out_specs=x_spec,
        out_shape=jax.ShapeDtypeStruct((m, d), F32),
        compiler_params=pltpu.CompilerParams(
            dimension_semantics=("arbitrary",), vmem_limit_bytes=VMEM_LIMIT),
        name=f"ffn_dense_{l}",
    )(x2d, mod, mod, mod, wts['g_ffn'], wts['g_final'], w1, w3, w2)


def _to_token_tiles(ref, val, tm):
    for k in range(SUBLANES):
        ref[pl.ds(k, tm, stride=SUBLANES), :] = val[:, k * LANES:(k + 1) * LANES]


def _from_token_tiles(ref, tm):
    return jnp.concatenate(
        [ref[pl.ds(k, tm, stride=SUBLANES), :] for k in range(SUBLANES)], axis=1)


def _moe_route_kernel(x_ref, shift_ref, scale_ref, gffn_ref, wr_ref, br_ref, h_ref, meta_ref, *,
                      rows_per_seq, tm):
    shift, scale = _modulation([shift_ref, scale_ref], rows_per_seq, tm)
    h32 = _rms(x_ref[...], gffn_ref[...]) * (1.0 + scale) + shift
    _to_token_tiles(h_ref, h32, tm)
    meta_ref[...] = _router_meta(h32, wr_ref, br_ref)


def _moe_route(x2d, mod, l, wts, rows_per_seq):
    m, d = x2d.shape
    tm = min(TM_FFN, m)
    assert m % tm == 0 and (rows_per_seq == 1 or rows_per_seq % tm == 0)
    i = l // 2
    return pl.pallas_call(
        functools.partial(_moe_route_kernel, rows_per_seq=rows_per_seq, tm=tm),
        grid=(m // tm,),
        in_specs=[pl.BlockSpec((tm, d), lambda t: (t, 0))]
                 + _mod_specs(mod, l, (3, 4), rows_per_seq, tm)
                 + [_layer_spec(wts['g_ffn'], l), _layer_spec(wts['w_router'], i),
                    _layer_spec(wts['b_router'], i)],
        out_specs=[pl.BlockSpec((tm * SUBLANES, LANES), lambda t: (t, 0)),
                   pl.BlockSpec((tm, LANES), lambda t: (t, 0))],
        out_shape=[jax.ShapeDtypeStruct((m * SUBLANES, LANES), F32),
                   jax.ShapeDtypeStruct((m, LANES), F32)],
        compiler_params=pltpu.CompilerParams(
            dimension_semantics=("arbitrary",), vmem_limit_bytes=VMEM_LIMIT),
        name=f"moe_route_{l}_{m}",
    )(x2d, mod, mod, wts['g_ffn'], wts['w_router'], wts['b_router'])


def _super_block_pieces(m_p, m_s, s_tok):
    pieces = []
    for k in range(N_SUPER):
        lo, hi = k * s_tok, (k + 1) * s_tok
        ps = []
        if lo < m_p:
            ps.append((0, lo, 0, min(hi, m_p) - lo))
        if hi > m_p:
            s0 = max(lo, m_p)
            ps.append((1, s0 - m_p, s0 - lo, hi - s0))
        pieces.append(ps)
    return pieces


def _moe_expert_kernel(cnt_ref, idx_hbm, g_hbm, hp_hbm, hs_hbm, w1_ref, w3_ref, w2_ref,
                       yp_hbm, ys_hbm, h_scr, y_scr, xbuf, obuf, idx_s, g_s, sem,
                       *, pieces, s_tok, s_pad, tm):
    sb = pl.program_id(0)
    e = pl.program_id(1)
    seg = sb * N_EXPERTS + e
    rows = s_tok * SUBLANES

    def piece_copies(k, to_vmem):
        copies = []
        for j, (grp, src_tok, dst_tok, n) in enumerate(pieces[k]):
            hbm = ((hp_hbm, hs_hbm) if to_vmem else (yp_hbm, ys_hbm))[grp]
            hbm = hbm.at[pl.ds(src_tok * SUBLANES, n * SUBLANES)]
            if to_vmem:
                copies.append(pltpu.make_async_copy(
                    hbm, h_scr.at[pl.ds(dst_tok * SUBLANES, n * SUBLANES)], sem.at[j]))
            else:
                copies.append(pltpu.make_async_copy(
                    y_scr.at[pl.ds(dst_tok * SUBLANES, n * SUBLANES)], hbm, sem.at[2 + j]))
        return copies

    off = pl.multiple_of(seg * s_pad, IDX_ALIGN)
    idx_copy = pltpu.make_async_copy(idx_hbm.at[pl.ds(off, s_pad)], idx_s, sem.at[4])
    g_copy = pltpu.make_async_copy(g_hbm.at[pl.ds(off, s_pad)], g_s, sem.at[5])
    idx_copy.start()
    g_copy.start()

    for k in range(N_SUPER):
        @pl.when(jnp.logical_and(sb == k, e == 0))
        def _load_super_block(k=k):
            for c in piece_copies(k, True):
                c.start()
            zrows = 256
            assert rows % zrows == 0
            def zero(i, carry):
                r0 = pl.multiple_of(i * zrows, zrows)
                y_scr[pl.ds(r0, zrows), :] = jnp.zeros((zrows, LANES), F32)
                return carry
            lax.fori_loop(0, rows // zrows, zero, 0)
            for c in piece_copies(k, True):
                c.wait()

    idx_copy.wait()
    g_copy.wait()

    def tile(i, carry):
        base = i * tm
        for r in range(tm):
            t8 = pl.multiple_of(idx_s[base + r] * SUBLANES, SUBLANES)
            xbuf[r * SUBLANES:(r + 1) * SUBLANES, :] = h_scr[pl.ds(t8, SUBLANES), :]
        x = _from_token_tiles(xbuf, tm).astype(BF16)
        act = (_silu(_dot(x, w1_ref[...])) * _dot(x, w3_ref[...])).astype(BF16)
        _to_token_tiles(obuf, _dot(act, w2_ref[...]), tm)
        for r0 in range(0, tm, SUBLANES):
            upd = []
            for r in range(r0, r0 + SUBLANES):
                t8 = pl.multiple_of(idx_s[base + r] * SUBLANES, SUBLANES)
                o = obuf[r * SUBLANES:(r + 1) * SUBLANES, :]
                upd.append((t8, y_scr[pl.ds(t8, SUBLANES), :] + g_s[base + r] * o))
            for t8, v in upd:
                y_scr[pl.ds(t8, SUBLANES), :] = v
        return carry

    lax.fori_loop(0, (cnt_ref[seg] + tm - 1) // tm, tile, 0)

    for k in range(N_SUPER):
        @pl.when(jnp.logical_and(sb == k, e == N_EXPERTS - 1))
        def _store_super_block(k=k):
            for c in piece_copies(k, False):
                c.start()
            for c in piece_copies(k, False):
                c.wait()


def _moe_experts(counts, idx, gates, h_p, h_s, l, wts, s_tok, s_pad):
    i = l // 2
    tm = TM_EXPERT
    w1, w3, w2 = wts['w1_moe'], wts['w3_moe'], wts['w2_moe']
    w_spec = lambda a: pl.BlockSpec((None, None) + a.shape[2:], lambda sb, e, cnt: (i, e, 0, 0))
    any_spec = pl.BlockSpec(memory_space=pl.ANY)
    rows = s_tok * SUBLANES
    pieces = _super_block_pieces(h_p.shape[0] // SUBLANES, h_s.shape[0] // SUBLANES, s_tok)
    return pl.pallas_call(
        functools.partial(_moe_expert_kernel, pieces=pieces, s_tok=s_tok, s_pad=s_pad, tm=tm),
        grid_spec=pltpu.PrefetchScalarGridSpec(
            num_scalar_prefetch=1,
            grid=(N_SUPER, N_EXPERTS),
            in_specs=[any_spec] * 4 + [w_spec(w1), w_spec(w3), w_spec(w2)],
            out_specs=[any_spec, any_spec],
            scratch_shapes=[pltpu.VMEM((rows, LANES), F32), pltpu.VMEM((rows, LANES), F32),
                            pltpu.VMEM((tm * SUBLANES, LANES), F32),
                            pltpu.VMEM((tm * SUBLANES, LANES), F32),
                            pltpu.SMEM((s_pad,), jnp.int32), pltpu.SMEM((s_pad,), F32),
                            pltpu.SemaphoreType.DMA((6,))]),
        out_shape=[jax.ShapeDtypeStruct(h_p.shape, F32), jax.ShapeDtypeStruct(h_s.shape, F32)],
        compiler_params=pltpu.CompilerParams(
            dimension_semantics=("arbitrary", "arbitrary"), vmem_limit_bytes=VMEM_LIMIT_EXPERT),
        name=f"moe_experts_{l}",
    )(counts, idx, gates, h_p, h_s, w1, w3, w2)


def _moe_residual_kernel(x_ref, gate_ref, gfin_ref, y_ref, o_ref, *, rows_per_seq, tm, final_norm):
    (gate,) = _modulation([gate_ref], rows_per_seq, tm)
    y = x_ref[...] + gate * _from_token_tiles(y_ref, tm)
    o_ref[...] = _rms(y, gfin_ref[...]) if final_norm else y


def _moe_residual(x2d, mod, y, l, wts, rows_per_seq, final_norm):
    m, d = x2d.shape
    tm = min(TM_FFN, m)
    assert m % tm == 0
    x_spec = pl.BlockSpec((tm, d), lambda t: (t, 0))
    return pl.pallas_call(
        functools.partial(_moe_residual_kernel, rows_per_seq=rows_per_seq, tm=tm,
                          final_norm=final_norm),
        grid=(m // tm,),
        in_specs=[x_spec] + _mod_specs(mod, l, (5,), rows_per_seq, tm)
                 + [_const_spec(wts['g_final'].shape),
                    pl.BlockSpec((tm * SUBLANES, LANES), lambda t: (t, 0))],
        out_specs=x_spec,
        out_shape=jax.ShapeDtypeStruct((m, d), F32),
        compiler_params=pltpu.CompilerParams(
            dimension_semantics=("arbitrary",), vmem_limit_bytes=VMEM_LIMIT),
        name=f"moe_residual_{l}_{m}",
    )(x2d, mod, wts['g_final'], y)


def _ffn_moe(xp2d, xs, mod_p, mod_s, l, wts, seq, final_norm):
    m_p, m_s = xp2d.shape[0], xs.shape[0]
    n_tok = m_p + m_s
    s_tok = n_tok // N_SUPER
    assert s_tok * N_SUPER == n_tok and s_tok % SUBLANES == 0
    s_pad = -(-(-(-s_tok // TM_EXPERT) * TM_EXPERT) // IDX_ALIGN) * IDX_ALIGN

    h_p, meta_p = _moe_route(xp2d, mod_p, l, wts, seq)
    h_s, meta_s = _moe_route(xs, mod_s, l, wts, 1)

    meta = jnp.concatenate([meta_p[:, :2 * N_EXPERTS], meta_s[:, :2 * N_EXPERTS]], axis=0)
    per_seg = lambda a: a.reshape(N_SUPER, s_tok, N_EXPERTS).transpose(0, 2, 1)
    gate = per_seg(meta[:, :N_EXPERTS])
    unsel = 1 - per_seg(meta[:, N_EXPERTS:]).astype(jnp.int32)
    tok = lax.broadcasted_iota(jnp.int32, unsel.shape, 2)
    _, idx, gate = lax.sort((unsel, tok, gate), dimension=2, num_keys=1, is_stable=True)
    counts = (s_tok - jnp.sum(unsel, axis=2)).reshape(-1)
    pad = lambda a: jnp.pad(a, ((0, 0), (0, 0), (0, s_pad - s_tok))).reshape(-1)

    y_p, y_s = _moe_experts(counts, pad(idx), pad(gate), h_p, h_s, l, wts, s_tok, s_pad)
    xp_new = _moe_residual(xp2d, mod_p, y_p, l, wts, seq, final_norm)
    xs_new = _moe_residual(xs, mod_s, y_s, l, wts, 1, final_norm)
    return xp_new, xs_new


def kernel(x_prompt, x_sample, state_conv_a, state_conv_c, state_pool_d, c_prompt, c_sample,
           w_ada, b_ada, g_mix, w_in, conv_a_w, ln_v_g, ln_v_b, w_s, b_s, conv_c_w, conv_c_b,
           ln_c_g, ln_c_b, pool_w, pool_scale, g_out, w_out, g_ffn, w1_dense, w3_dense, w2_dense,
           w_router, b_router, w1_moe, w3_moe, w2_moe, g_final):
    p = dict(g_mix=g_mix, w_in=w_in, conv_a_w=conv_a_w, ln_v_g=ln_v_g, ln_v_b=ln_v_b, w_s=w_s,
             b_s=b_s, conv_c_w=conv_c_w, conv_c_b=conv_c_b, ln_c_g=ln_c_g, ln_c_b=ln_c_b,
             pool_w=pool_w, pool_scale=pool_scale, g_out=g_out, w_out=w_out, g_ffn=g_ffn,
             w1_dense=w1_dense, w3_dense=w3_dense, w2_dense=w2_dense, w_router=w_router,
             b_router=b_router, w1_moe=w1_moe, w3_moe=w3_moe, w2_moe=w2_moe, g_final=g_final)
    n_p, seq, d = x_prompt.shape
    n_s, dec_seq, _ = x_sample.shape
    assert dec_seq == 1 and d == D_MODEL

    mod_p, mod_s = _ada(c_prompt, c_sample, w_ada, b_ada)
    kmajor = lambda s: jnp.transpose(s, (0, 2, 1, 3))
    sa_t, sc_t, sd_t = kmajor(state_conv_a), kmajor(state_conv_c), kmajor(state_pool_d)
    mixw = _mix_params(p)
    pad_e = LANES - N_EXPERTS
    wts = dict(
        g_ffn=g_ffn.reshape(DEPTH, 1, d), g_final=g_final.reshape(1, d),
        w1_dense=w1_dense.astype(BF16), w3_dense=w3_dense.astype(BF16),
        w2_dense=w2_dense.astype(BF16),
        w_router=jnp.pad(w_router, ((0, 0), (0, 0), (0, pad_e))),
        b_router=jnp.pad(b_router, ((0, 0), (0, pad_e))).reshape(-1, 1, LANES),
        w1_moe=w1_moe.astype(BF16), w3_moe=w3_moe.astype(BF16), w2_moe=w2_moe.astype(BF16))

    xp = x_prompt
    xs = x_sample.reshape(n_s, d)
    states_p = [[], [], [], []]
    states_s = [[], [], [], []]
    for l in range(DEPTH):
        last = l == DEPTH - 1
        xp, *st_p = _mix_prompt(xp, mod_p, l, mixw, 0)
        xs, *st_s = _mix_sample(xs, mod_s, sa_t, sc_t, sd_t, l, mixw, PAST_LEN)
        for acc, s in zip(states_p, st_p):
            acc.append(s)
        for acc, s in zip(states_s, st_s):
            acc.append(s)
        xp2d = xp.reshape(n_p * seq, d)
        if l % 2 == 0:
            xp2d = _ffn_dense(xp2d, mod_p, l, wts, seq, last)
            xs = _ffn_dense(xs, mod_s, l, wts, 1, last)
        else:
            xp2d, xs = _ffn_moe(xp2d, xs, mod_p, mod_s, l, wts, seq, last)
        xp = xp2d.reshape(n_p, seq, d)

    a_p, c_p, d_p, v_p = (jnp.stack(s) for s in states_p)
    a_s, c_s, d_s = (kmajor(jnp.stack(s)) for s in states_s[:3])
    v_s = jnp.stack(states_s[3]).reshape(DEPTH, n_s, 1, W_GROUP)
    return (xp, xs.reshape(n_s, 1, d), a_p, c_p, d_p, v_p, a_s, c_s, d_s, v_s)
```

```python
import functools

import jax
import jax.numpy as jnp
from jax import lax
from jax.experimental import pallas as pl
from jax.experimental.pallas import tpu as pltpu

D_MODEL = 1024
DEPTH = 4
W_GROUP = 256
CONV_A = 3
CHUNK = 128
N_HEADS_B = 4
HEAD_B = 64
CONV_C = 31
POOL_WINDOWS = (2, 4, 8, 16)
POOL_MAX = 16
GROUP_D = 64
IN_COLS = 2048
N_EXPERTS = 8
PAST_LEN = 16384
EPS = 1e-6

LANES = 128
LANE_TILES = W_GROUP // LANES
HIST_A = 8
HIST_C = 32
HIST_D = 32
TL_MIX = 512
ROW_CHUNK = 64
TM_FFN = 512
SUBLANES = 8
N_SUPER = 4
TM_EXPERT = 256
IDX_ALIGN = 1024
VMEM_LIMIT = 56 * 1024 * 1024
VMEM_LIMIT_EXPERT = 62 * 1024 * 1024

F32 = jnp.float32
BF16 = jnp.bfloat16


def _dot(a, b):
    return jnp.dot(a, b, preferred_element_type=F32)


def _rms(x, g):
    return x * lax.rsqrt(jnp.mean(x * x, axis=-1, keepdims=True) + EPS) * g


def _ln(x, g, b):
    xc = x - jnp.mean(x, axis=-1, keepdims=True)
    var = jnp.mean(xc * xc, axis=-1, keepdims=True)
    return xc * lax.rsqrt(var + EPS) * g + b


def _silu(x):
    return x * jax.nn.sigmoid(x)


def _lane_group_select(vals, shape):
    lane = lax.broadcasted_iota(jnp.int32, shape, 1)
    out = vals[3]
    for g in (2, 1, 0):
        out = jnp.where(lane < (g + 1) * GROUP_D, vals[g], out)
    return out


def _ada_kernel(cp_ref, cs_ref, w_ref, b_ref, op_ref, os_ref):
    w = w_ref[...].astype(BF16)
    b = b_ref[...]
    op_ref[...] = _dot(_silu(cp_ref[...]).astype(BF16), w) + b
    os_ref[...] = _dot(_silu(cs_ref[...]).astype(BF16), w) + b


def _ada(c_prompt, c_sample, w_ada, b_ada):
    n_p, n_s = c_prompt.shape[0], c_sample.shape[0]
    d = D_MODEL
    return pl.pallas_call(
        _ada_kernel,
        grid=(DEPTH, 6),
        in_specs=[
            pl.BlockSpec((n_p, d), lambda l, k: (0, 0)),
            pl.BlockSpec((n_s, d), lambda l, k: (0, 0)),
            pl.BlockSpec((None, d, d), lambda l, k: (l, 0, k)),
            pl.BlockSpec((None, 1, d), lambda l, k: (l, 0, k)),
        ],
        out_specs=[
            pl.BlockSpec((None, None, n_p, d), lambda l, k: (l, k, 0, 0)),
            pl.BlockSpec((None, None, n_s, d), lambda l, k: (l, k, 0, 0)),
        ],
        out_shape=[
            jax.ShapeDtypeStruct((DEPTH, 6, n_p, d), F32),
            jax.ShapeDtypeStruct((DEPTH, 6, n_s, d), F32),
        ],
        compiler_params=pltpu.CompilerParams(
            dimension_semantics=("arbitrary", "arbitrary"), vmem_limit_bytes=VMEM_LIMIT),
        name="ada",
    )(c_prompt, c_sample, w_ada, b_ada.reshape(DEPTH, 1, 6 * d))


def _masked_ws(ws_ref):
    r = lax.broadcasted_iota(jnp.int32, (CHUNK, CHUNK), 0)
    c = lax.broadcasted_iota(jnp.int32, (CHUNK, CHUNK), 1)
    return [jnp.where(c <= r, ws_ref[h], 0.0).astype(BF16) for h in range(N_HEADS_B)]


def _ext_rows(ref, j, row0, n):
    return ref[pl.ds(row0 * LANE_TILES + j, n, stride=LANE_TILES), :]


def _ext_store(ref, row0, val):
    for j in range(LANE_TILES):
        ref[pl.ds(row0 * LANE_TILES + j, val.shape[0], stride=LANE_TILES), :] = (
            val[:, j * LANES:(j + 1) * LANES])


def _ext_load(ref, row0, n):
    return jnp.concatenate([_ext_rows(ref, j, row0, n) for j in range(LANE_TILES)], axis=1)


def _merge_and_project(x, gate, outs, gout_ref, wout_bf):
    merged = jnp.concatenate(
        [_rms(o, gout_ref[i:i + 1, :]) for i, o in enumerate(outs)], axis=1).astype(BF16)
    return x + gate * _dot(merged, wout_bf)


def _mix_prompt_kernel(x_ref, shift_ref, scale_ref, gate_ref, gmix_ref, win_ref, wout_ref,
                       convaw_ref, lnvg_ref, lnvb_ref, ws_ref, bsfull_ref, convcw_ref,
                       convcb_ref, lncg_ref, lncb_ref, poolw_ref, poolscale_ref, gout_ref,
                       xo_ref, na_ref, nc_ref, nd_ref, nv_ref,
                       win_bf, wout_bf, exta, extc, extd, ext2, ext4, ext8,
                       *, tl, start_pos):
    b = pl.program_id(0)
    t = pl.program_id(1)
    last_t = pl.num_programs(1) - 1

    @pl.when(jnp.logical_and(b == 0, t == 0))
    def _cast_weights():
        rows = 128
        def body(i, carry):
            r0 = pl.multiple_of(i * rows, rows)
            win_bf[pl.ds(r0, rows), :] = win_ref[pl.ds(r0, rows), :].astype(BF16)
            wout_bf[pl.ds(r0, rows), :] = wout_ref[pl.ds(r0, rows), :].astype(BF16)
            return carry
        lax.fori_loop(0, D_MODEL // rows, body, 0)

    @pl.when(t == 0)
    def _zero_history():
        for ref, hist in ((exta, HIST_A), (extc, HIST_C), (extd, HIST_D)):
            ref[0:hist * LANE_TILES, :] = jnp.zeros((hist * LANE_TILES, LANES), F32)

    x = x_ref[...]
    shift = shift_ref[pl.ds(b, 1), :]
    scale = scale_ref[pl.ds(b, 1), :]
    gate = gate_ref[pl.ds(b, 1), :]
    h = _rms(x, gmix_ref[...] * (1.0 + scale)) + shift
    proj = _dot(h.astype(BF16), win_bf[...])
    a_b, a_c, a_h, b_u, b_v, c_a, c_g, d_p = [
        proj[:, i * W_GROUP:(i + 1) * W_GROUP] for i in range(8)]

    def dwconv(ext_ref, w_ref, hist, width):
        off = hist - (width - 1)
        halves = []
        for j in range(LANE_TILES):
            chunks = []
            for c0 in range(0, tl, ROW_CHUNK):
                acc = None
                for k in range(width):
                    term = (_ext_rows(ext_ref, j, off + c0 + k, ROW_CHUNK)
                            * w_ref[k:k + 1, j * LANES:(j + 1) * LANES])
                    acc = term if acc is None else acc + term
                chunks.append(acc)
            halves.append(jnp.concatenate(chunks, axis=0))
        return jnp.concatenate(halves, axis=1)

    _ext_store(exta, HIST_A, a_c * a_h)
    out_a = a_b * dwconv(exta, convaw_ref, HIST_A, CONV_A)

    v_n = _ln(b_v, lnvg_ref[...], lnvb_ref[...])
    v_bf = v_n.astype(BF16)
    wm = _masked_ws(ws_ref)
    lane = lax.broadcasted_iota(jnp.int32, (CHUNK, W_GROUP), 1)
    mixed_chunks = []
    for j in range(tl // CHUNK):
        vc = v_bf[j * CHUNK:(j + 1) * CHUNK, :]
        mixed = _dot(wm[3], vc)
        for hd in (2, 1, 0):
            mixed = jnp.where(lane < (hd + 1) * HEAD_B, _dot(wm[hd], vc), mixed)
        mixed_chunks.append(mixed + bsfull_ref[...])
    out_b = b_u * jnp.concatenate(mixed_chunks, axis=0)

    _ext_store(extc, HIST_C, c_a * jax.nn.sigmoid(c_g))
    y_c = dwconv(extc, convcw_ref, HIST_C, CONV_C) + convcb_ref[...]
    out_c = _silu(_ln(y_c, lncg_ref[...], lncb_ref[...]))

    n = HIST_D + tl
    _ext_store(extd, HIST_D, d_p)
    bufs = (extd, ext2, ext4, ext8)
    pos1 = start_pos + 1 + t * tl + lax.broadcasted_iota(jnp.int32, (tl, LANES), 0)
    low_group = lax.broadcasted_iota(jnp.int32, (tl, LANES), 1) < GROUP_D
    means = []
    for j in range(LANE_TILES):
        levels = 2 * (j + 1)
        for lv in range(levels - 1):
            first = 8 * (lv + 1)
            bufs[lv + 1][pl.ds(first * LANE_TILES + j, n - first, stride=LANE_TILES), :] = (
                _ext_rows(bufs[lv], j, first, n - first)
                + _ext_rows(bufs[lv], j, first - (1 << lv), n - first))
        prev = bufs[levels - 1]
        s_lo = _ext_rows(prev, j, HIST_D, tl)
        s_hi = s_lo + _ext_rows(prev, j, HIST_D - (1 << (levels - 1)), tl)
        w_lo, w_hi = POOL_WINDOWS[2 * j], POOL_WINDOWS[2 * j + 1]
        cnt = jnp.where(low_group, jnp.minimum(pos1, w_lo), jnp.minimum(pos1, w_hi)).astype(F32)
        means.append(jnp.where(low_group, s_lo, s_hi) / cnt)
    pooled = jnp.concatenate(means, axis=1) - d_p
    out_d = _dot(pooled.astype(BF16), poolw_ref[...].astype(BF16)) * poolscale_ref[...]

    xo_ref[...] = _merge_and_project(x, gate, [out_a, out_b, out_c, out_d], gout_ref, wout_bf[...])

    @pl.when(t == last_t)
    def _emit_state():
        na_ref[...] = _ext_load(exta, HIST_A + tl - (CONV_A - 1), CONV_A - 1)
        nc_ref[...] = _ext_load(extc, HIST_C + tl - (CONV_C - 1), CONV_C - 1)
        nd_ref[...] = _ext_load(extd, HIST_D + tl - (POOL_MAX - 1), POOL_MAX - 1)
        nv_ref[...] = v_n[tl - CHUNK:tl, :]

    for ref, hist in ((exta, HIST_A), (extc, HIST_C), (extd, HIST_D)):
        ref[0:hist * LANE_TILES, :] = ref[tl * LANE_TILES:(tl + hist) * LANE_TILES, :]


def _mix_params(p):
    r3 = lambda a: a.reshape(DEPTH, 1, -1)
    eye = jnp.eye(4, dtype=F32)
    pool_bd = (eye[None, :, None, :, None] * p['pool_w'][:, :, :, None, :]).reshape(
        DEPTH, W_GROUP, W_GROUP)
    return dict(
        gmix=r3(p['g_mix']), w_in=p['w_in'], w_out=p['w_out'], conva=p['conv_a_w'],
        lnvg=r3(p['ln_v_g']), lnvb=r3(p['ln_v_b']), ws=p['w_s'],
        bsfull=jnp.repeat(jnp.swapaxes(p['b_s'], 1, 2), HEAD_B, axis=2),
        ws0=r3(jnp.repeat(p['w_s'][:, :, 0, 0], HEAD_B, axis=1)),
        bs0=r3(jnp.repeat(p['b_s'][:, :, 0], HEAD_B, axis=1)),
        convc=p['conv_c_w'], convcb=r3(p['conv_c_b']), lncg=r3(p['ln_c_g']), lncb=r3(p['ln_c_b']),
        poolw=pool_bd, poolscale=r3(p['pool_scale']), gout=p['g_out'])


def _layer_spec(a, l):
    nd = a.ndim - 1
    return pl.BlockSpec((None,) + a.shape[1:], lambda *_: (l,) + (0,) * nd,
                        pipeline_mode=pl.Buffered(1))


def _const_spec(shape):
    nd = len(shape)
    return pl.BlockSpec(shape, lambda *_: (0,) * nd, pipeline_mode=pl.Buffered(1))


def _mix_prompt(x, mod_p, l, w, start_pos):
    n_b, seq, d = x.shape
    tl = TL_MIX
    assert seq % tl == 0 and tl % CHUNK == 0 and seq >= CHUNK
    mod_spec = lambda k: pl.BlockSpec((None, None, n_b, d), lambda b, t: (l, k, 0, 0))
    weights = [w[k] for k in ('gmix', 'w_in', 'w_out', 'conva', 'lnvg', 'lnvb', 'ws', 'bsfull',
                              'convc', 'convcb', 'lncg', 'lncb', 'poolw', 'poolscale', 'gout')]
    state_spec = lambda r: pl.BlockSpec((None, r, W_GROUP), lambda b, t: (b, 0, 0))
    ext = lambda hist: pltpu.VMEM(((hist + tl) * LANE_TILES, LANES), F32)
    return pl.pallas_call(
        functools.partial(_mix_prompt_kernel, tl=tl, start_pos=start_pos),
        grid=(n_b, seq // tl),
        in_specs=[pl.BlockSpec((None, tl, d), lambda b, t: (b, t, 0)),
                  mod_spec(0), mod_spec(1), mod_spec(2)] + [_layer_spec(a, l) for a in weights],
        out_specs=[pl.BlockSpec((None, tl, d), lambda b, t: (b, t, 0)),
                   state_spec(CONV_A - 1), state_spec(CONV_C - 1), state_spec(POOL_MAX - 1),
                   state_spec(CHUNK)],
        out_shape=[jax.ShapeDtypeStruct(x.shape, F32),
                   jax.ShapeDtypeStruct((n_b, CONV_A - 1, W_GROUP), F32),
                   jax.ShapeDtypeStruct((n_b, CONV_C - 1, W_GROUP), F32),
                   jax.ShapeDtypeStruct((n_b, POOL_MAX - 1, W_GROUP), F32),
                   jax.ShapeDtypeStruct((n_b, CHUNK, W_GROUP), F32)],
        scratch_shapes=[pltpu.VMEM((d, IN_COLS), BF16), pltpu.VMEM((d, d), BF16),
                        ext(HIST_A), ext(HIST_C), ext(HIST_D), ext(HIST_D), ext(HIST_D),
                        ext(HIST_D)],
        compiler_params=pltpu.CompilerParams(
            dimension_semantics=("arbitrary", "arbitrary"), vmem_limit_bytes=VMEM_LIMIT),
        name=f"mix_prompt_{l}",
    )(x, mod_p, mod_p, mod_p, *weights)


def _mix_sample_kernel(x_ref, shift_ref, scale_ref, gate_ref, sa_ref, sc_ref, sd_ref, gmix_ref,
                       win_ref, wout_ref, convaw_ref, lnvg_ref, lnvb_ref, ws0_ref, bs0_ref,
                       convcw_ref, convcb_ref, lncg_ref, lncb_ref, poolw_ref, poolscale_ref,
                       gout_ref, xo_ref, na_ref, nc_ref, nd_ref, nv_ref, *, start_pos):
    x = x_ref[...]
    h = _rms(x, gmix_ref[...]) * (1.0 + scale_ref[...]) + shift_ref[...]
    proj = _dot(h.astype(BF16), win_ref[...].astype(BF16))
    a_b, a_c, a_h, b_u, b_v, c_a, c_g, d_p = [
        proj[:, i * W_GROUP:(i + 1) * W_GROUP] for i in range(8)]

    ch = a_c * a_h
    y_a = convaw_ref[CONV_A - 1:CONV_A, :] * ch
    for k in range(CONV_A - 1):
        y_a = y_a + convaw_ref[k:k + 1, :] * sa_ref[k]
    out_a = a_b * y_a
    for k in range(CONV_A - 2):
        na_ref[k] = sa_ref[k + 1]
    na_ref[CONV_A - 2] = ch

    v_n = _ln(b_v, lnvg_ref[...], lnvb_ref[...])
    out_b = b_u * (ws0_ref[...] * v_n + bs0_ref[...])
    nv_ref[...] = v_n

    glu = c_a * jax.nn.sigmoid(c_g)
    y_c = convcw_ref[CONV_C - 1:CONV_C, :] * glu + convcb_ref[...]
    for k in range(CONV_C - 1):
        y_c = y_c + convcw_ref[k:k + 1, :] * sc_ref[k]
    out_c = _silu(_ln(y_c, lncg_ref[...], lncb_ref[...]))
    for k in range(CONV_C - 2):
        nc_ref[k] = sc_ref[k + 1]
    nc_ref[CONV_C - 2] = glu

    hist = POOL_MAX - 1
    run = d_p
    taken = 0
    sums = []
    for w in POOL_WINDOWS:
        while taken < w - 1:
            run = run + sd_ref[hist - 1 - taken]
            taken += 1
        sums.append(run / float(min(start_pos + 1, w)))
    pooled = _lane_group_select(sums, d_p.shape) - d_p
    out_d = _dot(pooled.astype(BF16), poolw_ref[...].astype(BF16)) * poolscale_ref[...]
    for k in range(hist - 1):
        nd_ref[k] = sd_ref[k + 1]
    nd_ref[hist - 1] = d_p

    xo_ref[...] = _merge_and_project(x, gate_ref[...], [out_a, out_b, out_c, out_d], gout_ref,
                                     wout_ref[...].astype(BF16))


def _mix_sample(x, mod_s, sa_t, sc_t, sd_t, l, w, start_pos):
    n, d = x.shape
    assert start_pos + 1 >= POOL_MAX
    mod_spec = lambda k: pl.BlockSpec((None, None, n, d), lambda i: (l, k, 0, 0))
    st_spec = lambda r: pl.BlockSpec((None, r, n, W_GROUP), lambda i: (l, 0, 0, 0))
    weights = [w[k] for k in ('gmix', 'w_in', 'w_out', 'conva', 'lnvg', 'lnvb', 'ws0', 'bs0',
                              'convc', 'convcb', 'lncg', 'lncb', 'poolw', 'poolscale', 'gout')]
    full = lambda shape: pl.BlockSpec(shape, lambda i: (0,) * len(shape))
    return pl.pallas_call(
        functools.partial(_mix_sample_kernel, start_pos=start_pos),
        grid=(1,),
        in_specs=[full((n, d)), mod_spec(0), mod_spec(1), mod_spec(2),
                  st_spec(CONV_A - 1), st_spec(CONV_C - 1), st_spec(POOL_MAX - 1)]
                 + [_layer_spec(a, l) for a in weights],
        out_specs=[full((n, d)), full((CONV_A - 1, n, W_GROUP)), full((CONV_C - 1, n, W_GROUP)),
                   full((POOL_MAX - 1, n, W_GROUP)), full((n, W_GROUP))],
        out_shape=[jax.ShapeDtypeStruct((n, d), F32),
                   jax.ShapeDtypeStruct((CONV_A - 1, n, W_GROUP), F32),
                   jax.ShapeDtypeStruct((CONV_C - 1, n, W_GROUP), F32),
                   jax.ShapeDtypeStruct((POOL_MAX - 1, n, W_GROUP), F32),
                   jax.ShapeDtypeStruct((n, W_GROUP), F32)],
        compiler_params=pltpu.CompilerParams(
            dimension_semantics=("arbitrary",), vmem_limit_bytes=VMEM_LIMIT),
        name=f"mix_sample_{l}",
    )(x, mod_s, mod_s, mod_s, sa_t, sc_t, sd_t, *weights)


def _modulation(refs, rows_per_seq, tm):
    if rows_per_seq == 1:
        return [r[...] for r in refs]
    b = (pl.program_id(0) * tm) // rows_per_seq
    return [r[pl.ds(b, 1), :] for r in refs]


def _ffn_dense_kernel(x_ref, shift_ref, scale_ref, gate_ref, gffn_ref, gfin_ref, w1_ref, w3_ref,
                      w2_ref, o_ref, *, rows_per_seq, tm, final_norm):
    x = x_ref[...]
    shift, scale, gate = _modulation([shift_ref, scale_ref, gate_ref], rows_per_seq, tm)
    h = (_rms(x, gffn_ref[...]) * (1.0 + scale) + shift).astype(BF16)
    act = (_silu(_dot(h, w1_ref[...])) * _dot(h, w3_ref[...])).astype(BF16)
    y = x + gate * _dot(act, w2_ref[...])
    o_ref[...] = _rms(y, gfin_ref[...]) if final_norm else y


def _router_meta(h32, wr_ref, br_ref):
    wr = wr_ref[...]
    h_hi = h32.astype(BF16)
    h_lo = (h32 - h_hi.astype(F32)).astype(BF16)
    w_hi = wr.astype(BF16)
    w_lo = (wr - w_hi.astype(F32)).astype(BF16)
    logits = _dot(h_hi, w_hi) + (_dot(h_lo, w_hi) + _dot(h_hi, w_lo)) + br_ref[...]
    lane = lax.broadcasted_iota(jnp.int32, logits.shape, 1)
    lane_f = lane.astype(F32)
    neg = jnp.float32(-jnp.inf)
    logits = jnp.where(lane < N_EXPERTS, logits, neg)
    m1 = jnp.max(logits, axis=-1, keepdims=True)
    i1 = jnp.min(jnp.where(logits == m1, lane_f, float(LANES)), axis=-1, keepdims=True)
    rest = jnp.where(lane_f == i1, neg, logits)
    m2 = jnp.max(rest, axis=-1, keepdims=True)
    i2 = jnp.min(jnp.where(rest == m2, lane_f, float(LANES)), axis=-1, keepdims=True)
    e = jnp.exp(m2 - m1)
    g1 = 1.0 / (1.0 + e)
    g2 = e / (1.0 + e)
    comb = jnp.where(lane_f == i1, g1, 0.0) + jnp.where(lane_f == i2, g2, 0.0)
    flags = (jnp.where(lane_f == i1 + N_EXPERTS, 1.0, 0.0)
             + jnp.where(lane_f == i2 + N_EXPERTS, 1.0, 0.0))
    return comb + flags


def _mod_specs(mod, l, ks, rows_per_seq, tm):
    d = mod.shape[-1]
    if rows_per_seq == 1:
        return [pl.BlockSpec((None, None, tm, d), lambda t, k=k: (l, k, t, 0)) for k in ks]
    n_seq = mod.shape[2]
    return [pl.BlockSpec((None, None, n_seq, d), lambda t, k=k: (l, k, 0, 0)) for k in ks]


def _ffn_dense(x2d, mod, l, wts, rows_per_seq, final_norm):
    m, d = x2d.shape
    tm = min(TM_FFN, m)
    assert m % tm == 0 and (rows_per_seq == 1 or rows_per_seq % tm == 0)
    i = l // 2
    x_spec = pl.BlockSpec((tm, d), lambda t: (t, 0))
    w1, w3, w2 = wts['w1_dense'], wts['w3_dense'], wts['w2_dense']
    return pl.pallas_call(
        functools.partial(_ffn_dense_kernel, rows_per_seq=rows_per_seq, tm=tm,
                          final_norm=final_norm),
        grid=(m // tm,),
        in_specs=[x_spec] + _mod_specs(mod, l, (3, 4, 5), rows_per_seq, tm)
                 + [_layer_spec(wts['g_ffn'], l), _const_spec(wts['g_final'].shape),
                    _layer_spec(w1, i), _layer_spec(w3, i), _layer_spec(w2, i)],
        out_specs=x_spec,
        out_shape=jax.ShapeDtypeStruct((m, d), F32),
        compiler_params=pltpu.CompilerParams(
            dimension_semantics=("arbitrary",), vmem_limit_bytes=VMEM_LIMIT),
        name=f"ffn_dense_{l}",
    )(x2d, mod, mod, mod, wts['g_ffn'], wts['g_final'], w1, w3, w2)


def _to_token_tiles(ref, val, tm):
    for k in range(SUBLANES):
        ref[pl.ds(k, tm, stride=SUBLANES), :] = val[:, k * LANES:(k + 1) * LANES]


def _from_token_tiles(ref, tm):
    return jnp.concatenate(
        [ref[pl.ds(k, tm, stride=SUBLANES), :] for k in range(SUBLANES)], axis=1)


def _moe_route_kernel(x_ref, shift_ref, scale_ref, gffn_ref, wr_ref, br_ref, h_ref, meta_ref, *,
                      rows_per_seq, tm):
    shift, scale = _modulation([shift_ref, scale_ref], rows_per_seq, tm)
    h32 = _rms(x_ref[...], gffn_ref[...]) * (1.0 + scale) + shift
    _to_token_tiles(h_ref, h32, tm)
    meta_ref[...] = _router_meta(h32, wr_ref, br_ref)


def _moe_route(x2d, mod, l, wts, rows_per_seq):
    m, d = x2d.shape
    tm = min(TM_FFN, m)
    assert m % tm == 0 and (rows_per_seq == 1 or rows_per_seq % tm == 0)
    i = l // 2
    return pl.pallas_call(
        functools.partial(_moe_route_kernel, rows_per_seq=rows_per_seq, tm=tm),
        grid=(m // tm,),
        in_specs=[pl.BlockSpec((tm, d), lambda t: (t, 0))]
                 + _mod_specs(mod, l, (3, 4), rows_per_seq, tm)
                 + [_layer_spec(wts['g_ffn'], l), _layer_spec(wts['w_router'], i),
                    _layer_spec(wts['b_router'], i)],
        out_specs=[pl.BlockSpec((tm * SUBLANES, LANES), lambda t: (t, 0)),
                   pl.BlockSpec((tm, LANES), lambda t: (t, 0))],
        out_shape=[jax.ShapeDtypeStruct((m * SUBLANES, LANES), F32),
                   jax.ShapeDtypeStruct((m, LANES), F32)],
        compiler_params=pltpu.CompilerParams(
            dimension_semantics=("arbitrary",), vmem_limit_bytes=VMEM_LIMIT),
        name=f"moe_route_{l}_{m}",
    )(x2d, mod, mod, wts['g_ffn'], wts['w_router'], wts['b_router'])


def _super_block_pieces(m_p, m_s, s_tok):
    pieces = []
    for k in range(N_SUPER):
        lo, hi = k * s_tok, (k + 1) * s_tok
        ps = []
        if lo < m_p:
            ps.append((0, lo, 0, min(hi, m_p) - lo))
        if hi > m_p:
            s0 = max(lo, m_p)
            ps.append((1, s0 - m_p, s0 - lo, hi - s0))
        pieces.append(ps)
    return pieces


def _moe_expert_kernel(cnt_ref, idx_hbm, g_hbm, hp_hbm, hs_hbm, w1_ref, w3_ref, w2_ref,
                       yp_hbm, ys_hbm, h_scr, y_scr, xbuf, obuf, idx_s, g_s, sem,
                       *, pieces, s_tok, s_pad, tm):
    sb = pl.program_id(0)
    e = pl.program_id(1)
    seg = sb * N_EXPERTS + e
    rows = s_tok * SUBLANES

    def piece_copies(k, to_vmem):
        copies = []
        for j, (grp, src_tok, dst_tok, n) in enumerate(pieces[k]):
            hbm = ((hp_hbm, hs_hbm) if to_vmem else (yp_hbm, ys_hbm))[grp]
            hbm = hbm.at[pl.ds(src_tok * SUBLANES, n * SUBLANES)]
            if to_vmem:
                copies.append(pltpu.make_async_copy(
                    hbm, h_scr.at[pl.ds(dst_tok * SUBLANES, n * SUBLANES)], sem.at[j]))
            else:
                copies.append(pltpu.make_async_copy(
                    y_scr.at[pl.ds(dst_tok * SUBLANES, n * SUBLANES)], hbm, sem.at[2 + j]))
        return copies

    def list_copies(s):
        slot = lax.rem(s, 2)
        src = pl.ds(pl.multiple_of(s * s_pad, IDX_ALIGN), s_pad)
        dst = pl.ds(pl.multiple_of(slot * s_pad, IDX_ALIGN), s_pad)
        return (pltpu.make_async_copy(idx_hbm.at[src], idx_s.at[dst], sem.at[4 + slot]),
                pltpu.make_async_copy(g_hbm.at[src], g_s.at[dst], sem.at[6 + slot]))

    @pl.when(seg == 0)
    def _first_lists():
        for c in list_copies(seg):
            c.start()
        obuf[...] = jnp.zeros(obuf.shape, F32)

    @pl.when(seg + 1 < N_SUPER * N_EXPERTS)
    def _next_lists():
        for c in list_copies(seg + 1):
            c.start()

    for k in range(N_SUPER):
        @pl.when(jnp.logical_and(sb == k, e == 0))
        def _load_super_block(k=k):
            for c in piece_copies(k, True):
                c.start()
            zrows = 256
            assert rows % zrows == 0
            def zero(i, carry):
                r0 = pl.multiple_of(i * zrows, zrows)
                y_scr[pl.ds(r0, zrows), :] = jnp.zeros((zrows, LANES), F32)
                return carry
            lax.fori_loop(0, rows // zrows, zero, 0)
            for c in piece_copies(k, True):
                c.wait()

    for c in list_copies(seg):
        c.wait()
    lst = lax.rem(seg, 2) * s_pad

    def gather(base):
        for r in range(tm):
            t8 = pl.multiple_of(idx_s[lst + base + r] * SUBLANES, SUBLANES)
            xbuf[r * SUBLANES:(r + 1) * SUBLANES, :] = h_scr[pl.ds(t8, SUBLANES), :]

    def scatter_add(base, limit):
        for r0 in range(0, tm, SUBLANES):
            upd = []
            for r in range(r0, r0 + SUBLANES):
                tok = idx_s[lst + base + r]
                t8 = pl.multiple_of(tok * SUBLANES, SUBLANES)
                g = jnp.where(base + r < limit, g_s[lst + tok], 0.0)
                o = obuf[r * SUBLANES:(r + 1) * SUBLANES, :]
                upd.append((t8, y_scr[pl.ds(t8, SUBLANES), :] + g * o))
            for t8, v in upd:
                y_scr[pl.ds(t8, SUBLANES), :] = v

    n_sel = cnt_ref[seg]
    n_tiles = (n_sel + tm - 1) // tm

    def tile(i, carry):
        x = _from_token_tiles(xbuf, tm).astype(BF16)
        gather((i + 1) * tm)
        scatter_add(jnp.maximum(i - 1, 0) * tm, jnp.where(i > 0, n_sel, 0))
        act = (_silu(_dot(x, w1_ref[...])) * _dot(x, w3_ref[...])).astype(BF16)
        _to_token_tiles(obuf, _dot(act, w2_ref[...]), tm)
        return carry

    gather(0)
    lax.fori_loop(0, n_tiles, tile, 0)

    @pl.when(n_tiles > 0)
    def _last_scatter():
        scatter_add((n_tiles - 1) * tm, n_sel)

    for k in range(N_SUPER):
        @pl.when(jnp.logical_and(sb == k, e == N_EXPERTS - 1))
        def _store_super_block(k=k):
            for c in piece_copies(k, False):
                c.start()
            for c in piece_copies(k, False):
                c.wait()


def _moe_experts(counts, idx, gates, h_p, h_s, l, wts, s_tok, s_pad):
    i = l // 2
    tm = TM_EXPERT
    w1, w3, w2 = wts['w1_moe'], wts['w3_moe'], wts['w2_moe']
    w_spec = lambda a: pl.BlockSpec((None, None) + a.shape[2:], lambda sb, e, cnt: (i, e, 0, 0))
    any_spec = pl.BlockSpec(memory_space=pl.ANY)
    rows = s_tok * SUBLANES
    pieces = _super_block_pieces(h_p.shape[0] // SUBLANES, h_s.shape[0] // SUBLANES, s_tok)
    return pl.pallas_call(
        functools.partial(_moe_expert_kernel, pieces=pieces, s_tok=s_tok, s_pad=s_pad, tm=tm),
        grid_spec=pltpu.PrefetchScalarGridSpec(
            num_scalar_prefetch=1,
            grid=(N_SUPER, N_EXPERTS),
            in_specs=[any_spec] * 4 + [w_spec(w1), w_spec(w3), w_spec(w2)],
            out_specs=[any_spec, any_spec],
            scratch_shapes=[pltpu.VMEM((rows, LANES), F32), pltpu.VMEM((rows, LANES), F32),
                            pltpu.VMEM((tm * SUBLANES, LANES), F32),
                            pltpu.VMEM((tm * SUBLANES, LANES), F32),
                            pltpu.SMEM((2 * s_pad,), jnp.int32), pltpu.SMEM((2 * s_pad,), F32),
                            pltpu.SemaphoreType.DMA((8,))]),
        out_shape=[jax.ShapeDtypeStruct(h_p.shape, F32), jax.ShapeDtypeStruct(h_s.shape, F32)],
        compiler_params=pltpu.CompilerParams(
            dimension_semantics=("arbitrary", "arbitrary"), vmem_limit_bytes=VMEM_LIMIT_EXPERT),
        name=f"moe_experts_{l}",
    )(counts, idx, gates, h_p, h_s, w1, w3, w2)


def _moe_residual_kernel(x_ref, gate_ref, gfin_ref, y_ref, o_ref, *, rows_per_seq, tm, final_norm):
    (gate,) = _modulation([gate_ref], rows_per_seq, tm)
    y = x_ref[...] + gate * _from_token_tiles(y_ref, tm)
    o_ref[...] = _rms(y, gfin_ref[...]) if final_norm else y


def _moe_residual(x2d, mod, y, l, wts, rows_per_seq, final_norm):
    m, d = x2d.shape
    tm = min(TM_FFN, m)
    assert m % tm == 0
    x_spec = pl.BlockSpec((tm, d), lambda t: (t, 0))
    return pl.pallas_call(
        functools.partial(_moe_residual_kernel, rows_per_seq=rows_per_seq, tm=tm,
                          final_norm=final_norm),
        grid=(m // tm,),
        in_specs=[x_spec] + _mod_specs(mod, l, (5,), rows_per_seq, tm)
                 + [_const_spec(wts['g_final'].shape),
                    pl.BlockSpec((tm * SUBLANES, LANES), lambda t: (t, 0))],
        out_specs=x_spec,
        out_shape=jax.ShapeDtypeStruct((m, d), F32),
        compiler_params=pltpu.CompilerParams(
            dimension_semantics=("arbitrary",), vmem_limit_bytes=VMEM_LIMIT),
        name=f"moe_residual_{l}_{m}",
    )(x2d, mod, wts['g_final'], y)


def _ffn_moe(xp2d, xs, mod_p, mod_s, l, wts, seq, final_norm):
    m_p, m_s = xp2d.shape[0], xs.shape[0]
    n_tok = m_p + m_s
    s_tok = n_tok // N_SUPER
    assert s_tok * N_SUPER == n_tok and s_tok % SUBLANES == 0
    s_pad = -(-((-(-s_tok // TM_EXPERT) + 1) * TM_EXPERT) // IDX_ALIGN) * IDX_ALIGN

    h_p, meta_p = _moe_route(xp2d, mod_p, l, wts, seq)
    h_s, meta_s = _moe_route(xs, mod_s, l, wts, 1)

    meta = jnp.concatenate([meta_p[:, :2 * N_EXPERTS], meta_s[:, :2 * N_EXPERTS]], axis=0)
    per_seg = lambda a: a.reshape(N_SUPER, s_tok, N_EXPERTS).transpose(0, 2, 1)
    gate = per_seg(meta[:, :N_EXPERTS])
    unsel = 1 - per_seg(meta[:, N_EXPERTS:]).astype(jnp.int32)
    tok = lax.broadcasted_iota(jnp.int32, unsel.shape, 2)
    tok_bits = (s_tok - 1).bit_length()
    idx = lax.sort(unsel * (1 << tok_bits) + tok, dimension=2) & ((1 << tok_bits) - 1)
    counts = (s_tok - jnp.sum(unsel, axis=2)).reshape(-1)
    pad = lambda a: jnp.pad(a, ((0, 0), (0, 0), (0, s_pad - s_tok))).reshape(-1)

    y_p, y_s = _moe_experts(counts, pad(idx), pad(gate), h_p, h_s, l, wts, s_tok, s_pad)
    xp_new = _moe_residual(xp2d, mod_p, y_p, l, wts, seq, final_norm)
    xs_new = _moe_residual(xs, mod_s, y_s, l, wts, 1, final_norm)
    return xp_new, xs_new


def kernel(x_prompt, x_sample, state_conv_a, state_conv_c, state_pool_d, c_prompt, c_sample,
           w_ada, b_ada, g_mix, w_in, conv_a_w, ln_v_g, ln_v_b, w_s, b_s, conv_c_w, conv_c_b,
           ln_c_g, ln_c_b, pool_w, pool_scale, g_out, w_out, g_ffn, w1_dense, w3_dense, w2_dense,
           w_router, b_router, w1_moe, w3_moe, w2_moe, g_final):
    p = dict(g_mix=g_mix, w_in=w_in, conv_a_w=conv_a_w, ln_v_g=ln_v_g, ln_v_b=ln_v_b, w_s=w_s,
             b_s=b_s, conv_c_w=conv_c_w, conv_c_b=conv_c_b, ln_c_g=ln_c_g, ln_c_b=ln_c_b,
             pool_w=pool_w, pool_scale=pool_scale, g_out=g_out, w_out=w_out, g_ffn=g_ffn,
             w1_dense=w1_dense, w3_dense=w3_dense, w2_dense=w2_dense, w_router=w_router,
             b_router=b_router, w1_moe=w1_moe, w3_moe=w3_moe, w2_moe=w2_moe, g_final=g_final)
    n_p, seq, d = x_prompt.shape
    n_s, dec_seq, _ = x_sample.shape
    assert dec_seq == 1 and d == D_MODEL

    mod_p, mod_s = _ada(c_prompt, c_sample, w_ada, b_ada)
    kmajor = lambda s: jnp.transpose(s, (0, 2, 1, 3))
    sa_t, sc_t, sd_t = kmajor(state_conv_a), kmajor(state_conv_c), kmajor(state_pool_d)
    mixw = _mix_params(p)
    pad_e = LANES - N_EXPERTS
    wts = dict(
        g_ffn=g_ffn.reshape(DEPTH, 1, d), g_final=g_final.reshape(1, d),
        w1_dense=w1_dense.astype(BF16), w3_dense=w3_dense.astype(BF16),
        w2_dense=w2_dense.astype(BF16),
        w_router=jnp.pad(w_router, ((0, 0), (0, 0), (0, pad_e))),
        b_router=jnp.pad(b_router, ((0, 0), (0, pad_e))).reshape(-1, 1, LANES),
        w1_moe=w1_moe.astype(BF16), w3_moe=w3_moe.astype(BF16), w2_moe=w2_moe.astype(BF16))

    xp = x_prompt
    xs = x_sample.reshape(n_s, d)
    states_p = [[], [], [], []]
    states_s = [[], [], [], []]
    for l in range(DEPTH):
        last = l == DEPTH - 1
        xp, *st_p = _mix_prompt(xp, mod_p, l, mixw, 0)
        xs, *st_s = _mix_sample(xs, mod_s, sa_t, sc_t, sd_t, l, mixw, PAST_LEN)
        for acc, s in zip(states_p, st_p):
            acc.append(s)
        for acc, s in zip(states_s, st_s):
            acc.append(s)
        xp2d = xp.reshape(n_p * seq, d)
        if l % 2 == 0:
            xp2d = _ffn_dense(xp2d, mod_p, l, wts, seq, last)
            xs = _ffn_dense(xs, mod_s, l, wts, 1, last)
        else:
            xp2d, xs = _ffn_moe(xp2d, xs, mod_p, mod_s, l, wts, seq, last)
        xp = xp2d.reshape(n_p, seq, d)

    a_p, c_p, d_p, v_p = (jnp.stack(s) for s in states_p)
    a_s, c_s, d_s = (kmajor(jnp.stack(s)) for s in states_s[:3])
    v_s = jnp.stack(states_s[3]).reshape(DEPTH, n_s, 1, W_GROUP)
    return (xp, xs.reshape(n_s, 1, d), a_p, c_p, d_p, v_p, a_s, c_s, d_s, v_s)
```

```python
import functools

import jax
import jax.numpy as jnp
from jax import lax
from jax.experimental import pallas as pl
from jax.experimental.pallas import tpu as pltpu

D_MODEL = 1024
DEPTH = 4
W_GROUP = 256
CONV_A = 3
CHUNK = 128
N_HEADS_B = 4
HEAD_B = 64
CONV_C = 31
POOL_WINDOWS = (2, 4, 8, 16)
POOL_MAX = 16
GROUP_D = 64
IN_COLS = 2048
N_EXPERTS = 8
PAST_LEN = 16384
EPS = 1e-6

LANES = 128
LANE_TILES = W_GROUP // LANES
HIST_A = 8
HIST_C = 32
HIST_D = 32
TL_MIX = 512
ROW_CHUNK = 64
TM_FFN = 512
SUBLANES = 8
N_SUPER = 4
TM_EXPERT = 256
IDX_ALIGN = 1024
VMEM_LIMIT = 56 * 1024 * 1024
VMEM_LIMIT_EXPERT = 62 * 1024 * 1024

F32 = jnp.float32
BF16 = jnp.bfloat16


def _dot(a, b):
    return jnp.dot(a, b, preferred_element_type=F32)


def _rms(x, g):
    return x * lax.rsqrt(jnp.mean(x * x, axis=-1, keepdims=True) + EPS) * g


def _ln(x, g, b):
    xc = x - jnp.mean(x, axis=-1, keepdims=True)
    var = jnp.mean(xc * xc, axis=-1, keepdims=True)
    return xc * lax.rsqrt(var + EPS) * g + b


def _silu(x):
    return x * jax.nn.sigmoid(x)


def _lane_group_select(vals, shape):
    lane = lax.broadcasted_iota(jnp.int32, shape, 1)
    out = vals[3]
    for g in (2, 1, 0):
        out = jnp.where(lane < (g + 1) * GROUP_D, vals[g], out)
    return out


def _ada_kernel(cp_ref, cs_ref, w_ref, b_ref, op_ref, os_ref):
    w = w_ref[...].astype(BF16)
    b = b_ref[...]
    op_ref[...] = _dot(_silu(cp_ref[...]).astype(BF16), w) + b
    os_ref[...] = _dot(_silu(cs_ref[...]).astype(BF16), w) + b


def _ada(c_prompt, c_sample, w_ada, b_ada):
    n_p, n_s = c_prompt.shape[0], c_sample.shape[0]
    d = D_MODEL
    return pl.pallas_call(
        _ada_kernel,
        grid=(DEPTH, 6),
        in_specs=[
            pl.BlockSpec((n_p, d), lambda l, k: (0, 0)),
            pl.BlockSpec((n_s, d), lambda l, k: (0, 0)),
            pl.BlockSpec((None, d, d), lambda l, k: (l, 0, k)),
            pl.BlockSpec((None, 1, d), lambda l, k: (l, 0, k)),
        ],
        out_specs=[
            pl.BlockSpec((None, None, n_p, d), lambda l, k: (l, k, 0, 0)),
            pl.BlockSpec((None, None, n_s, d), lambda l, k: (l, k, 0, 0)),
        ],
        out_shape=[
            jax.ShapeDtypeStruct((DEPTH, 6, n_p, d), F32),
            jax.ShapeDtypeStruct((DEPTH, 6, n_s, d), F32),
        ],
        compiler_params=pltpu.CompilerParams(
            dimension_semantics=("arbitrary", "arbitrary"), vmem_limit_bytes=VMEM_LIMIT),
        name="ada",
    )(c_prompt, c_sample, w_ada, b_ada.reshape(DEPTH, 1, 6 * d))


def _masked_ws(ws_ref):
    r = lax.broadcasted_iota(jnp.int32, (CHUNK, CHUNK), 0)
    c = lax.broadcasted_iota(jnp.int32, (CHUNK, CHUNK), 1)
    return [jnp.where(c <= r, ws_ref[h], 0.0).astype(BF16) for h in range(N_HEADS_B)]


def _ext_rows(ref, j, row0, n):
    return ref[pl.ds(row0 * LANE_TILES + j, n, stride=LANE_TILES), :]


def _ext_store(ref, row0, val):
    for j in range(LANE_TILES):
        ref[pl.ds(row0 * LANE_TILES + j, val.shape[0], stride=LANE_TILES), :] = (
            val[:, j * LANES:(j + 1) * LANES])


def _ext_load(ref, row0, n):
    return jnp.concatenate([_ext_rows(ref, j, row0, n) for j in range(LANE_TILES)], axis=1)


def _merge_and_project(x, gate, outs, gout_ref, wout_bf):
    merged = jnp.concatenate(
        [_rms(o, gout_ref[i:i + 1, :]) for i, o in enumerate(outs)], axis=1).astype(BF16)
    return x + gate * _dot(merged, wout_bf)


def _to_token_tiles(ref, val, tm):
    for k in range(SUBLANES):
        ref[pl.ds(k, tm, stride=SUBLANES), :] = val[:, k * LANES:(k + 1) * LANES]


def _from_token_tiles(ref, tm):
    return jnp.concatenate(
        [ref[pl.ds(k, tm, stride=SUBLANES), :] for k in range(SUBLANES)], axis=1)


def _router_meta(h32, wr_ref, br_ref):
    wr = wr_ref[...]
    h_hi = h32.astype(BF16)
    h_lo = (h32 - h_hi.astype(F32)).astype(BF16)
    w_hi = wr.astype(BF16)
    w_lo = (wr - w_hi.astype(F32)).astype(BF16)
    logits = _dot(h_hi, w_hi) + (_dot(h_lo, w_hi) + _dot(h_hi, w_lo)) + br_ref[...]
    lane = lax.broadcasted_iota(jnp.int32, logits.shape, 1)
    lane_f = lane.astype(F32)
    neg = jnp.float32(-jnp.inf)
    logits = jnp.where(lane < N_EXPERTS, logits, neg)
    m1 = jnp.max(logits, axis=-1, keepdims=True)
    i1 = jnp.min(jnp.where(logits == m1, lane_f, float(LANES)), axis=-1, keepdims=True)
    rest = jnp.where(lane_f == i1, neg, logits)
    m2 = jnp.max(rest, axis=-1, keepdims=True)
    i2 = jnp.min(jnp.where(rest == m2, lane_f, float(LANES)), axis=-1, keepdims=True)
    e = jnp.exp(m2 - m1)
    g1 = 1.0 / (1.0 + e)
    g2 = e / (1.0 + e)
    comb = jnp.where(lane_f == i1, g1, 0.0) + jnp.where(lane_f == i2, g2, 0.0)
    flags = (jnp.where(lane_f == i1 + N_EXPERTS, 1.0, 0.0)
             + jnp.where(lane_f == i2 + N_EXPERTS, 1.0, 0.0))
    return comb + flags


def _mix_prompt_kernel(*refs, tl, start_pos, pre, post):
    refs = list(refs)
    take = lambda n: [refs.pop(0) for _ in range(n)]
    (x_ref,) = take(1)
    y_ref, gprev_ref = take(2) if pre else (None, None)
    shift_ref, scale_ref, gate_ref = take(3)
    shift2_ref, scale2_ref = take(2) if post else (None, None)
    (gmix_ref, win_ref, wout_ref, convaw_ref, lnvg_ref, lnvb_ref, ws_ref, bsfull_ref, convcw_ref,
     convcb_ref, lncg_ref, lncb_ref, poolw_ref, poolscale_ref, gout_ref) = take(15)
    gffn_ref, wr_ref, br_ref = take(3) if post else (None, None, None)
    xo_ref, na_ref, nc_ref, nd_ref, nv_ref = take(5)
    h_ref, meta_ref = take(2) if post else (None, None)
    win_bf, wout_bf, exta, extc, extd, ext2, ext4, ext8 = take(8)
    assert not refs
    b = pl.program_id(0)
    t = pl.program_id(1)
    last_t = pl.num_programs(1) - 1

    @pl.when(jnp.logical_and(b == 0, t == 0))
    def _cast_weights():
        rows = 128
        def body(i, carry):
            r0 = pl.multiple_of(i * rows, rows)
            win_bf[pl.ds(r0, rows), :] = win_ref[pl.ds(r0, rows), :].astype(BF16)
            wout_bf[pl.ds(r0, rows), :] = wout_ref[pl.ds(r0, rows), :].astype(BF16)
            return carry
        lax.fori_loop(0, D_MODEL // rows, body, 0)

    @pl.when(t == 0)
    def _zero_history():
        for ref, hist in ((exta, HIST_A), (extc, HIST_C), (extd, HIST_D)):
            ref[0:hist * LANE_TILES, :] = jnp.zeros((hist * LANE_TILES, LANES), F32)

    x = x_ref[...]
    if pre:
        x = x + gprev_ref[pl.ds(b, 1), :] * _from_token_tiles(y_ref, tl)
    shift = shift_ref[pl.ds(b, 1), :]
    scale = scale_ref[pl.ds(b, 1), :]
    gate = gate_ref[pl.ds(b, 1), :]
    h = _rms(x, gmix_ref[...] * (1.0 + scale)) + shift
    proj = _dot(h.astype(BF16), win_bf[...])
    a_b, a_c, a_h, b_u, b_v, c_a, c_g, d_p = [
        proj[:, i * W_GROUP:(i + 1) * W_GROUP] for i in range(8)]

    def dwconv(ext_ref, w_ref, hist, width):
        off = hist - (width - 1)
        halves = []
        for j in range(LANE_TILES):
            chunks = []
            for c0 in range(0, tl, ROW_CHUNK):
                acc = None
                for k in range(width):
                    term = (_ext_rows(ext_ref, j, off + c0 + k, ROW_CHUNK)
                            * w_ref[k:k + 1, j * LANES:(j + 1) * LANES])
                    acc = term if acc is None else acc + term
                chunks.append(acc)
            halves.append(jnp.concatenate(chunks, axis=0))
        return jnp.concatenate(halves, axis=1)

    _ext_store(exta, HIST_A, a_c * a_h)
    out_a = a_b * dwconv(exta, convaw_ref, HIST_A, CONV_A)

    v_n = _ln(b_v, lnvg_ref[...], lnvb_ref[...])
    v_bf = v_n.astype(BF16)
    wm = _masked_ws(ws_ref)
    lane = lax.broadcasted_iota(jnp.int32, (CHUNK, W_GROUP), 1)
    mixed_chunks = []
    for j in range(tl // CHUNK):
        vc = v_bf[j * CHUNK:(j + 1) * CHUNK, :]
        mixed = _dot(wm[3], vc)
        for hd in (2, 1, 0):
            mixed = jnp.where(lane < (hd + 1) * HEAD_B, _dot(wm[hd], vc), mixed)
        mixed_chunks.append(mixed + bsfull_ref[...])
    out_b = b_u * jnp.concatenate(mixed_chunks, axis=0)

    _ext_store(extc, HIST_C, c_a * jax.nn.sigmoid(c_g))
    y_c = dwconv(extc, convcw_ref, HIST_C, CONV_C) + convcb_ref[...]
    out_c = _silu(_ln(y_c, lncg_ref[...], lncb_ref[...]))

    n = HIST_D + tl
    _ext_store(extd, HIST_D, d_p)
    bufs = (extd, ext2, ext4, ext8)
    pos1 = start_pos + 1 + t * tl + lax.broadcasted_iota(jnp.int32, (tl, LANES), 0)
    low_group = lax.broadcasted_iota(jnp.int32, (tl, LANES), 1) < GROUP_D
    means = []
    for j in range(LANE_TILES):
        levels = 2 * (j + 1)
        for lv in range(levels - 1):
            first = 8 * (lv + 1)
            bufs[lv + 1][pl.ds(first * LANE_TILES + j, n - first, stride=LANE_TILES), :] = (
                _ext_rows(bufs[lv], j, first, n - first)
                + _ext_rows(bufs[lv], j, first - (1 << lv), n - first))
        prev = bufs[levels - 1]
        s_lo = _ext_rows(prev, j, HIST_D, tl)
        s_hi = s_lo + _ext_rows(prev, j, HIST_D - (1 << (levels - 1)), tl)
        w_lo, w_hi = POOL_WINDOWS[2 * j], POOL_WINDOWS[2 * j + 1]
        cnt = jnp.where(low_group, jnp.minimum(pos1, w_lo), jnp.minimum(pos1, w_hi)).astype(F32)
        means.append(jnp.where(low_group, s_lo, s_hi) / cnt)
    pooled = jnp.concatenate(means, axis=1) - d_p
    out_d = _dot(pooled.astype(BF16), poolw_ref[...].astype(BF16)) * poolscale_ref[...]

    x_new = _merge_and_project(x, gate, [out_a, out_b, out_c, out_d], gout_ref, wout_bf[...])
    xo_ref[...] = x_new
    if post:
        h2 = (_rms(x_new, gffn_ref[...]) * (1.0 + scale2_ref[pl.ds(b, 1), :])
              + shift2_ref[pl.ds(b, 1), :])
        _to_token_tiles(h_ref, h2, tl)
        meta_ref[...] = _router_meta(h2, wr_ref, br_ref)

    @pl.when(t == last_t)
    def _emit_state():
        na_ref[...] = _ext_load(exta, HIST_A + tl - (CONV_A - 1), CONV_A - 1)
        nc_ref[...] = _ext_load(extc, HIST_C + tl - (CONV_C - 1), CONV_C - 1)
        nd_ref[...] = _ext_load(extd, HIST_D + tl - (POOL_MAX - 1), POOL_MAX - 1)
        nv_ref[...] = v_n[tl - CHUNK:tl, :]

    for ref, hist in ((exta, HIST_A), (extc, HIST_C), (extd, HIST_D)):
        ref[0:hist * LANE_TILES, :] = ref[tl * LANE_TILES:(tl + hist) * LANE_TILES, :]


def _mix_params(p):
    r3 = lambda a: a.reshape(DEPTH, 1, -1)
    eye = jnp.eye(4, dtype=F32)
    pool_bd = (eye[None, :, None, :, None] * p['pool_w'][:, :, :, None, :]).reshape(
        DEPTH, W_GROUP, W_GROUP)
    return dict(
        gmix=r3(p['g_mix']), w_in=p['w_in'], w_out=p['w_out'], conva=p['conv_a_w'],
        lnvg=r3(p['ln_v_g']), lnvb=r3(p['ln_v_b']), ws=p['w_s'],
        bsfull=jnp.repeat(jnp.swapaxes(p['b_s'], 1, 2), HEAD_B, axis=2),
        ws0=r3(jnp.repeat(p['w_s'][:, :, 0, 0], HEAD_B, axis=1)),
        bs0=r3(jnp.repeat(p['b_s'][:, :, 0], HEAD_B, axis=1)),
        convc=p['conv_c_w'], convcb=r3(p['conv_c_b']), lncg=r3(p['ln_c_g']), lncb=r3(p['ln_c_b']),
        poolw=pool_bd, poolscale=r3(p['pool_scale']), gout=p['g_out'])


def _layer_spec(a, l):
    nd = a.ndim - 1
    return pl.BlockSpec((None,) + a.shape[1:], lambda *_: (l,) + (0,) * nd,
                        pipeline_mode=pl.Buffered(1))


def _const_spec(shape):
    nd = len(shape)
    return pl.BlockSpec(shape, lambda *_: (0,) * nd, pipeline_mode=pl.Buffered(1))


def _mix_prompt(x, mod_p, l, w, start_pos, prev_moe=None, wts=None):
    n_b, seq, d = x.shape
    tl = TL_MIX
    nt = seq // tl
    assert seq % tl == 0 and tl % CHUNK == 0 and seq >= CHUNK
    pre, post = prev_moe is not None, wts is not None
    mod_spec = lambda lay, k: pl.BlockSpec((None, None, n_b, d), lambda b, t: (lay, k, 0, 0))
    tiles_spec = lambda rows: pl.BlockSpec((tl * rows, LANES), lambda b, t: (b * nt + t, 0))
    weights = [w[k] for k in ('gmix', 'w_in', 'w_out', 'conva', 'lnvg', 'lnvb', 'ws', 'bsfull',
                              'convc', 'convcb', 'lncg', 'lncb', 'poolw', 'poolscale', 'gout')]
    state_spec = lambda r: pl.BlockSpec((None, r, W_GROUP), lambda b, t: (b, 0, 0))
    ext = lambda hist: pltpu.VMEM(((hist + tl) * LANE_TILES, LANES), F32)

    args, in_specs = [x], [pl.BlockSpec((None, tl, d), lambda b, t: (b, t, 0))]
    if pre:
        y_prev, l_prev = prev_moe
        args += [y_prev, mod_p]
        in_specs += [tiles_spec(SUBLANES), mod_spec(l_prev, 5)]
    args += [mod_p] * 3
    in_specs += [mod_spec(l, 0), mod_spec(l, 1), mod_spec(l, 2)]
    if post:
        args += [mod_p] * 2
        in_specs += [mod_spec(l, 3), mod_spec(l, 4)]
    args += weights
    in_specs += [_layer_spec(a, l) for a in weights]
    out_specs = [pl.BlockSpec((None, tl, d), lambda b, t: (b, t, 0)),
                 state_spec(CONV_A - 1), state_spec(CONV_C - 1), state_spec(POOL_MAX - 1),
                 state_spec(CHUNK)]
    out_shape = [jax.ShapeDtypeStruct(x.shape, F32),
                 jax.ShapeDtypeStruct((n_b, CONV_A - 1, W_GROUP), F32),
                 jax.ShapeDtypeStruct((n_b, CONV_C - 1, W_GROUP), F32),
                 jax.ShapeDtypeStruct((n_b, POOL_MAX - 1, W_GROUP), F32),
                 jax.ShapeDtypeStruct((n_b, CHUNK, W_GROUP), F32)]
    if post:
        i = l // 2
        args += [wts['g_ffn'], wts['w_router'], wts['b_router']]
        in_specs += [_layer_spec(wts['g_ffn'], l), _layer_spec(wts['w_router'], i),
                     _layer_spec(wts['b_router'], i)]
        out_specs += [tiles_spec(SUBLANES), tiles_spec(1)]
        out_shape += [jax.ShapeDtypeStruct((n_b * seq * SUBLANES, LANES), F32),
                      jax.ShapeDtypeStruct((n_b * seq, LANES), F32)]
    return pl.pallas_call(
        functools.partial(_mix_prompt_kernel, tl=tl, start_pos=start_pos, pre=pre, post=post),
        grid=(n_b, nt),
        in_specs=in_specs,
        out_specs=out_specs,
        out_shape=out_shape,
        scratch_shapes=[pltpu.VMEM((d, IN_COLS), BF16), pltpu.VMEM((d, d), BF16),
                        ext(HIST_A), ext(HIST_C), ext(HIST_D), ext(HIST_D), ext(HIST_D),
                        ext(HIST_D)],
        compiler_params=pltpu.CompilerParams(
            dimension_semantics=("arbitrary", "arbitrary"), vmem_limit_bytes=VMEM_LIMIT),
        name=f"mix_prompt_{l}",
    )(*args)


def _mix_sample_kernel(x_ref, shift_ref, scale_ref, gate_ref, sa_ref, sc_ref, sd_ref, gmix_ref,
                       win_ref, wout_ref, convaw_ref, lnvg_ref, lnvb_ref, ws0_ref, bs0_ref,
                       convcw_ref, convcb_ref, lncg_ref, lncb_ref, poolw_ref, poolscale_ref,
                       gout_ref, xo_ref, na_ref, nc_ref, nd_ref, nv_ref, *, start_pos):
    x = x_ref[...]
    h = _rms(x, gmix_ref[...]) * (1.0 + scale_ref[...]) + shift_ref[...]
    proj = _dot(h.astype(BF16), win_ref[...].astype(BF16))
    a_b, a_c, a_h, b_u, b_v, c_a, c_g, d_p = [
        proj[:, i * W_GROUP:(i + 1) * W_GROUP] for i in range(8)]

    ch = a_c * a_h
    y_a = convaw_ref[CONV_A - 1:CONV_A, :] * ch
    for k in range(CONV_A - 1):
        y_a = y_a + convaw_ref[k:k + 1, :] * sa_ref[k]
    out_a = a_b * y_a
    for k in range(CONV_A - 2):
        na_ref[k] = sa_ref[k + 1]
    na_ref[CONV_A - 2] = ch

    v_n = _ln(b_v, lnvg_ref[...], lnvb_ref[...])
    out_b = b_u * (ws0_ref[...] * v_n + bs0_ref[...])
    nv_ref[...] = v_n

    glu = c_a * jax.nn.sigmoid(c_g)
    y_c = convcw_ref[CONV_C - 1:CONV_C, :] * glu + convcb_ref[...]
    for k in range(CONV_C - 1):
        y_c = y_c + convcw_ref[k:k + 1, :] * sc_ref[k]
    out_c = _silu(_ln(y_c, lncg_ref[...], lncb_ref[...]))
    for k in range(CONV_C - 2):
        nc_ref[k] = sc_ref[k + 1]
    nc_ref[CONV_C - 2] = glu

    hist = POOL_MAX - 1
    run = d_p
    taken = 0
    sums = []
    for w in POOL_WINDOWS:
        while taken < w - 1:
            run = run + sd_ref[hist - 1 - taken]
            taken += 1
        sums.append(run / float(min(start_pos + 1, w)))
    pooled = _lane_group_select(sums, d_p.shape) - d_p
    out_d = _dot(pooled.astype(BF16), poolw_ref[...].astype(BF16)) * poolscale_ref[...]
    for k in range(hist - 1):
        nd_ref[k] = sd_ref[k + 1]
    nd_ref[hist - 1] = d_p

    xo_ref[...] = _merge_and_project(x, gate_ref[...], [out_a, out_b, out_c, out_d], gout_ref,
                                     wout_ref[...].astype(BF16))


def _mix_sample(x, mod_s, sa_t, sc_t, sd_t, l, w, start_pos):
    n, d = x.shape
    assert start_pos + 1 >= POOL_MAX
    mod_spec = lambda k: pl.BlockSpec((None, None, n, d), lambda i: (l, k, 0, 0))
    st_spec = lambda r: pl.BlockSpec((None, r, n, W_GROUP), lambda i: (l, 0, 0, 0))
    weights = [w[k] for k in ('gmix', 'w_in', 'w_out', 'conva', 'lnvg', 'lnvb', 'ws0', 'bs0',
                              'convc', 'convcb', 'lncg', 'lncb', 'poolw', 'poolscale', 'gout')]
    full = lambda shape: pl.BlockSpec(shape, lambda i: (0,) * len(shape))
    return pl.pallas_call(
        functools.partial(_mix_sample_kernel, start_pos=start_pos),
        grid=(1,),
        in_specs=[full((n, d)), mod_spec(0), mod_spec(1), mod_spec(2),
                  st_spec(CONV_A - 1), st_spec(CONV_C - 1), st_spec(POOL_MAX - 1)]
                 + [_layer_spec(a, l) for a in weights],
        out_specs=[full((n, d)), full((CONV_A - 1, n, W_GROUP)), full((CONV_C - 1, n, W_GROUP)),
                   full((POOL_MAX - 1, n, W_GROUP)), full((n, W_GROUP))],
        out_shape=[jax.ShapeDtypeStruct((n, d), F32),
                   jax.ShapeDtypeStruct((CONV_A - 1, n, W_GROUP), F32),
                   jax.ShapeDtypeStruct((CONV_C - 1, n, W_GROUP), F32),
                   jax.ShapeDtypeStruct((POOL_MAX - 1, n, W_GROUP), F32),
                   jax.ShapeDtypeStruct((n, W_GROUP), F32)],
        compiler_params=pltpu.CompilerParams(
            dimension_semantics=("arbitrary",), vmem_limit_bytes=VMEM_LIMIT),
        name=f"mix_sample_{l}",
    )(x, mod_s, mod_s, mod_s, sa_t, sc_t, sd_t, *weights)


def _modulation(refs, rows_per_seq, tm):
    if rows_per_seq == 1:
        return [r[...] for r in refs]
    b = (pl.program_id(0) * tm) // rows_per_seq
    return [r[pl.ds(b, 1), :] for r in refs]


def _ffn_dense_kernel(x_ref, shift_ref, scale_ref, gate_ref, gffn_ref, gfin_ref, w1_ref, w3_ref,
                      w2_ref, o_ref, *, rows_per_seq, tm, final_norm):
    x = x_ref[...]
    shift, scale, gate = _modulation([shift_ref, scale_ref, gate_ref], rows_per_seq, tm)
    h = (_rms(x, gffn_ref[...]) * (1.0 + scale) + shift).astype(BF16)
    act = (_silu(_dot(h, w1_ref[...])) * _dot(h, w3_ref[...])).astype(BF16)
    y = x + gate * _dot(act, w2_ref[...])
    o_ref[...] = _rms(y, gfin_ref[...]) if final_norm else y


def _mod_specs(mod, l, ks, rows_per_seq, tm):
    d = mod.shape[-1]
    if rows_per_seq == 1:
        return [pl.BlockSpec((None, None, tm, d), lambda t, k=k: (l, k, t, 0)) for k in ks]
    n_seq = mod.shape[2]
    return [pl.BlockSpec((None, None, n_seq, d), lambda t, k=k: (l, k, 0, 0)) for k in ks]


def _ffn_dense(x2d, mod, l, wts, rows_per_seq, final_norm):
    m, d = x2d.shape
    tm = min(TM_FFN, m)
    assert m % tm == 0 and (rows_per_seq == 1 or rows_per_seq % tm == 0)
    i = l // 2
    x_spec = pl.BlockSpec((tm, d), lambda t: (t, 0))
    w1, w3, w2 = wts['w1_dense'], wts['w3_dense'], wts['w2_dense']
    return pl.pallas_call(
        functools.partial(_ffn_dense_kernel, rows_per_seq=rows_per_seq, tm=tm,
                          final_norm=final_norm),
        grid=(m // tm,),
        in_specs=[x_spec] + _mod_specs(mod, l, (3, 4, 5), rows_per_seq, tm)
                 + [_layer_spec(wts['g_ffn'], l), _const_spec(wts['g_final'].shape),
                    _layer_spec(w1, i), _layer_spec(w3, i), _layer_spec(w2, i)],
        out_specs=x_spec,
        out_shape=jax.ShapeDtypeStruct((m, d), F32),
        compiler_params=pltpu.CompilerParams(
            dimension_semantics=("arbitrary",), vmem_limit_bytes=VMEM_LIMIT),
        name=f"ffn_dense_{l}",
    )(x2d, mod, mod, mod, wts['g_ffn'], wts['g_final'], w1, w3, w2)


def _moe_route_kernel(x_ref, shift_ref, scale_ref, gffn_ref, wr_ref, br_ref, h_ref, meta_ref, *,
                      rows_per_seq, tm):
    shift, scale = _modulation([shift_ref, scale_ref], rows_per_seq, tm)
    h32 = _rms(x_ref[...], gffn_ref[...]) * (1.0 + scale) + shift
    _to_token_tiles(h_ref, h32, tm)
    meta_ref[...] = _router_meta(h32, wr_ref, br_ref)


def _moe_route(x2d, mod, l, wts, rows_per_seq):
    m, d = x2d.shape
    tm = min(TM_FFN, m)
    assert m % tm == 0 and (rows_per_seq == 1 or rows_per_seq % tm == 0)
    i = l // 2
    return pl.pallas_call(
        functools.partial(_moe_route_kernel, rows_per_seq=rows_per_seq, tm=tm),
        grid=(m // tm,),
        in_specs=[pl.BlockSpec((tm, d), lambda t: (t, 0))]
                 + _mod_specs(mod, l, (3, 4), rows_per_seq, tm)
                 + [_layer_spec(wts['g_ffn'], l), _layer_spec(wts['w_router'], i),
                    _layer_spec(wts['b_router'], i)],
        out_specs=[pl.BlockSpec((tm * SUBLANES, LANES), lambda t: (t, 0)),
                   pl.BlockSpec((tm, LANES), lambda t: (t, 0))],
        out_shape=[jax.ShapeDtypeStruct((m * SUBLANES, LANES), F32),
                   jax.ShapeDtypeStruct((m, LANES), F32)],
        compiler_params=pltpu.CompilerParams(
            dimension_semantics=("arbitrary",), vmem_limit_bytes=VMEM_LIMIT),
        name=f"moe_route_{l}_{m}",
    )(x2d, mod, mod, wts['g_ffn'], wts['w_router'], wts['b_router'])


def _super_block_pieces(m_p, m_s, s_tok):
    pieces = []
    for k in range(N_SUPER):
        lo, hi = k * s_tok, (k + 1) * s_tok
        ps = []
        if lo < m_p:
            ps.append((0, lo, 0, min(hi, m_p) - lo))
        if hi > m_p:
            s0 = max(lo, m_p)
            ps.append((1, s0 - m_p, s0 - lo, hi - s0))
        pieces.append(ps)
    return pieces


def _moe_expert_kernel(cnt_ref, idx_hbm, g_hbm, hp_hbm, hs_hbm, w1_ref, w3_ref, w2_ref,
                       yp_hbm, ys_hbm, h_scr, y_scr, xbuf, obuf, idx_s, g_s, sem,
                       *, pieces, s_tok, s_pad, tm):
    sb = pl.program_id(0)
    e = pl.program_id(1)
    seg = sb * N_EXPERTS + e
    rows = s_tok * SUBLANES

    def piece_copies(k, to_vmem):
        copies = []
        for j, (grp, src_tok, dst_tok, n) in enumerate(pieces[k]):
            hbm = ((hp_hbm, hs_hbm) if to_vmem else (yp_hbm, ys_hbm))[grp]
            hbm = hbm.at[pl.ds(src_tok * SUBLANES, n * SUBLANES)]
            if to_vmem:
                copies.append(pltpu.make_async_copy(
                    hbm, h_scr.at[pl.ds(dst_tok * SUBLANES, n * SUBLANES)], sem.at[j]))
            else:
                copies.append(pltpu.make_async_copy(
                    y_scr.at[pl.ds(dst_tok * SUBLANES, n * SUBLANES)], hbm, sem.at[2 + j]))
        return copies

    def list_copies(s):
        slot = lax.rem(s, 2)
        src = pl.ds(pl.multiple_of(s * s_pad, IDX_ALIGN), s_pad)
        dst = pl.ds(pl.multiple_of(slot * s_pad, IDX_ALIGN), s_pad)
        return (pltpu.make_async_copy(idx_hbm.at[src], idx_s.at[dst], sem.at[4 + slot]),
                pltpu.make_async_copy(g_hbm.at[src], g_s.at[dst], sem.at[6 + slot]))

    @pl.when(seg == 0)
    def _first_lists():
        for c in list_copies(seg):
            c.start()
        obuf[...] = jnp.zeros(obuf.shape, F32)

    @pl.when(seg + 1 < N_SUPER * N_EXPERTS)
    def _next_lists():
        for c in list_copies(seg + 1):
            c.start()

    for k in range(N_SUPER):
        @pl.when(jnp.logical_and(sb == k, e == 0))
        def _load_super_block(k=k):
            for c in piece_copies(k, True):
                c.start()
            zrows = 256
            assert rows % zrows == 0
            def zero(i, carry):
                r0 = pl.multiple_of(i * zrows, zrows)
                y_scr[pl.ds(r0, zrows), :] = jnp.zeros((zrows, LANES), F32)
                return carry
            lax.fori_loop(0, rows // zrows, zero, 0)
            for c in piece_copies(k, True):
                c.wait()

    for c in list_copies(seg):
        c.wait()
    lst = lax.rem(seg, 2) * s_pad

    def gather(base):
        for r in range(tm):
            t8 = pl.multiple_of(idx_s[lst + base + r] * SUBLANES, SUBLANES)
            xbuf[r * SUBLANES:(r + 1) * SUBLANES, :] = h_scr[pl.ds(t8, SUBLANES), :]

    def scatter_add(base, limit):
        for r0 in range(0, tm, SUBLANES):
            upd = []
            for r in range(r0, r0 + SUBLANES):
                tok = idx_s[lst + base + r]
                t8 = pl.multiple_of(tok * SUBLANES, SUBLANES)
                g = jnp.where(base + r < limit, g_s[lst + tok], 0.0)
                o = obuf[r * SUBLANES:(r + 1) * SUBLANES, :]
                upd.append((t8, y_scr[pl.ds(t8, SUBLANES), :] + g * o))
            for t8, v in upd:
                y_scr[pl.ds(t8, SUBLANES), :] = v

    n_sel = cnt_ref[seg]
    n_tiles = (n_sel + tm - 1) // tm

    def tile(i, carry):
        x = _from_token_tiles(xbuf, tm).astype(BF16)
        gather((i + 1) * tm)
        scatter_add(jnp.maximum(i - 1, 0) * tm, jnp.where(i > 0, n_sel, 0))
        act = (_silu(_dot(x, w1_ref[...])) * _dot(x, w3_ref[...])).astype(BF16)
        _to_token_tiles(obuf, _dot(act, w2_ref[...]), tm)
        return carry

    gather(0)
    lax.fori_loop(0, n_tiles, tile, 0)

    @pl.when(n_tiles > 0)
    def _last_scatter():
        scatter_add((n_tiles - 1) * tm, n_sel)

    for k in range(N_SUPER):
        @pl.when(jnp.logical_and(sb == k, e == N_EXPERTS - 1))
        def _store_super_block(k=k):
            for c in piece_copies(k, False):
                c.start()
            for c in piece_copies(k, False):
                c.wait()


def _moe_experts(counts, idx, gates, h_p, h_s, l, wts, s_tok, s_pad):
    i = l // 2
    tm = TM_EXPERT
    w1, w3, w2 = wts['w1_moe'], wts['w3_moe'], wts['w2_moe']
    w_spec = lambda a: pl.BlockSpec((None, None) + a.shape[2:], lambda sb, e, cnt: (i, e, 0, 0))
    any_spec = pl.BlockSpec(memory_space=pl.ANY)
    rows = s_tok * SUBLANES
    pieces = _super_block_pieces(h_p.shape[0] // SUBLANES, h_s.shape[0] // SUBLANES, s_tok)
    return pl.pallas_call(
        functools.partial(_moe_expert_kernel, pieces=pieces, s_tok=s_tok, s_pad=s_pad, tm=tm),
        grid_spec=pltpu.PrefetchScalarGridSpec(
            num_scalar_prefetch=1,
            grid=(N_SUPER, N_EXPERTS),
            in_specs=[any_spec] * 4 + [w_spec(w1), w_spec(w3), w_spec(w2)],
            out_specs=[any_spec, any_spec],
            scratch_shapes=[pltpu.VMEM((rows, LANES), F32), pltpu.VMEM((rows, LANES), F32),
                            pltpu.VMEM((tm * SUBLANES, LANES), F32),
                            pltpu.VMEM((tm * SUBLANES, LANES), F32),
                            pltpu.SMEM((2 * s_pad,), jnp.int32), pltpu.SMEM((2 * s_pad,), F32),
                            pltpu.SemaphoreType.DMA((8,))]),
        out_shape=[jax.ShapeDtypeStruct(h_p.shape, F32), jax.ShapeDtypeStruct(h_s.shape, F32)],
        compiler_params=pltpu.CompilerParams(
            dimension_semantics=("arbitrary", "arbitrary"), vmem_limit_bytes=VMEM_LIMIT_EXPERT),
        name=f"moe_experts_{l}",
    )(counts, idx, gates, h_p, h_s, w1, w3, w2)


def _moe_residual_kernel(x_ref, gate_ref, gfin_ref, y_ref, o_ref, *, rows_per_seq, tm, final_norm):
    (gate,) = _modulation([gate_ref], rows_per_seq, tm)
    y = x_ref[...] + gate * _from_token_tiles(y_ref, tm)
    o_ref[...] = _rms(y, gfin_ref[...]) if final_norm else y


def _moe_residual(x2d, mod, y, l, wts, rows_per_seq, final_norm):
    m, d = x2d.shape
    tm = min(TM_FFN, m)
    assert m % tm == 0
    x_spec = pl.BlockSpec((tm, d), lambda t: (t, 0))
    return pl.pallas_call(
        functools.partial(_moe_residual_kernel, rows_per_seq=rows_per_seq, tm=tm,
                          final_norm=final_norm),
        grid=(m // tm,),
        in_specs=[x_spec] + _mod_specs(mod, l, (5,), rows_per_seq, tm)
                 + [_const_spec(wts['g_final'].shape),
                    pl.BlockSpec((tm * SUBLANES, LANES), lambda t: (t, 0))],
        out_specs=x_spec,
        out_shape=jax.ShapeDtypeStruct((m, d), F32),
        compiler_params=pltpu.CompilerParams(
            dimension_semantics=("arbitrary",), vmem_limit_bytes=VMEM_LIMIT),
        name=f"moe_residual_{l}_{m}",
    )(x2d, mod, wts['g_final'], y)


def _ffn_moe(routed_p, xs, mod_s, l, wts):
    h_p, meta_p = routed_p
    m_p, m_s = meta_p.shape[0], xs.shape[0]
    n_tok = m_p + m_s
    s_tok = n_tok // N_SUPER
    assert s_tok * N_SUPER == n_tok and s_tok % SUBLANES == 0
    s_pad = -(-((-(-s_tok // TM_EXPERT) + 1) * TM_EXPERT) // IDX_ALIGN) * IDX_ALIGN

    h_s, meta_s = _moe_route(xs, mod_s, l, wts, 1)

    meta = jnp.concatenate([meta_p[:, :2 * N_EXPERTS], meta_s[:, :2 * N_EXPERTS]], axis=0)
    per_seg = lambda a: a.reshape(N_SUPER, s_tok, N_EXPERTS).transpose(0, 2, 1)
    gate = per_seg(meta[:, :N_EXPERTS])
    unsel = 1 - per_seg(meta[:, N_EXPERTS:]).astype(jnp.int32)
    tok = lax.broadcasted_iota(jnp.int32, unsel.shape, 2)
    tok_bits = (s_tok - 1).bit_length()
    idx = lax.sort(unsel * (1 << tok_bits) + tok, dimension=2) & ((1 << tok_bits) - 1)
    counts = (s_tok - jnp.sum(unsel, axis=2)).reshape(-1)
    pad = lambda a: jnp.pad(a, ((0, 0), (0, 0), (0, s_pad - s_tok))).reshape(-1)

    return _moe_experts(counts, pad(idx), pad(gate), h_p, h_s, l, wts, s_tok, s_pad)


def kernel(x_prompt, x_sample, state_conv_a, state_conv_c, state_pool_d, c_prompt, c_sample,
           w_ada, b_ada, g_mix, w_in, conv_a_w, ln_v_g, ln_v_b, w_s, b_s, conv_c_w, conv_c_b,
           ln_c_g, ln_c_b, pool_w, pool_scale, g_out, w_out, g_ffn, w1_dense, w3_dense, w2_dense,
           w_router, b_router, w1_moe, w3_moe, w2_moe, g_final):
    p = dict(g_mix=g_mix, w_in=w_in, conv_a_w=conv_a_w, ln_v_g=ln_v_g, ln_v_b=ln_v_b, w_s=w_s,
             b_s=b_s, conv_c_w=conv_c_w, conv_c_b=conv_c_b, ln_c_g=ln_c_g, ln_c_b=ln_c_b,
             pool_w=pool_w, pool_scale=pool_scale, g_out=g_out, w_out=w_out, g_ffn=g_ffn,
             w1_dense=w1_dense, w3_dense=w3_dense, w2_dense=w2_dense, w_router=w_router,
             b_router=b_router, w1_moe=w1_moe, w3_moe=w3_moe, w2_moe=w2_moe, g_final=g_final)
    n_p, seq, d = x_prompt.shape
    n_s, dec_seq, _ = x_sample.shape
    assert dec_seq == 1 and d == D_MODEL

    mod_p, mod_s = _ada(c_prompt, c_sample, w_ada, b_ada)
    kmajor = lambda s: jnp.transpose(s, (0, 2, 1, 3))
    sa_t, sc_t, sd_t = kmajor(state_conv_a), kmajor(state_conv_c), kmajor(state_pool_d)
    mixw = _mix_params(p)
    pad_e = LANES - N_EXPERTS
    wts = dict(
        g_ffn=g_ffn.reshape(DEPTH, 1, d), g_final=g_final.reshape(1, d),
        w1_dense=w1_dense.astype(BF16), w3_dense=w3_dense.astype(BF16),
        w2_dense=w2_dense.astype(BF16),
        w_router=jnp.pad(w_router, ((0, 0), (0, 0), (0, pad_e))),
        b_router=jnp.pad(b_router, ((0, 0), (0, pad_e))).reshape(-1, 1, LANES),
        w1_moe=w1_moe.astype(BF16), w3_moe=w3_moe.astype(BF16), w2_moe=w2_moe.astype(BF16))

    xp = x_prompt
    xs = x_sample.reshape(n_s, d)
    states_p = [[], [], [], []]
    states_s = [[], [], [], []]
    pending = None
    for l in range(DEPTH):
        last = l == DEPTH - 1
        moe = l % 2 == 1
        xp, *st_p = _mix_prompt(xp, mod_p, l, mixw, 0, pending, wts if moe else None)
        pending = None
        xs, *st_s = _mix_sample(xs, mod_s, sa_t, sc_t, sd_t, l, mixw, PAST_LEN)
        for acc, s in zip(states_p, st_p[:4]):
            acc.append(s)
        for acc, s in zip(states_s, st_s):
            acc.append(s)
        if moe:
            y_p, y_s = _ffn_moe(st_p[4:], xs, mod_s, l, wts)
            xs = _moe_residual(xs, mod_s, y_s, l, wts, 1, last)
            if last:
                xp = _moe_residual(xp.reshape(n_p * seq, d), mod_p, y_p, l, wts, seq,
                                   True).reshape(n_p, seq, d)
            else:
                pending = (y_p, l)
        else:
            xp = _ffn_dense(xp.reshape(n_p * seq, d), mod_p, l, wts, seq, last).reshape(n_p, seq, d)
            xs = _ffn_dense(xs, mod_s, l, wts, 1, last)

    a_p, c_p, d_p, v_p = (jnp.stack(s) for s in states_p)
    a_s, c_s, d_s = (kmajor(jnp.stack(s)) for s in states_s[:3])
    v_s = jnp.stack(states_s[3]).reshape(DEPTH, n_s, 1, W_GROUP)
    return (xp, xs.reshape(n_s, 1, d), a_p, c_p, d_p, v_p, a_s, c_s, d_s, v_s)
```

```python
import functools

import jax
import jax.numpy as jnp
from jax import lax
from jax.experimental import pallas as pl
from jax.experimental.pallas import tpu as pltpu

D_MODEL = 1024
DEPTH = 4
W_GROUP = 256
CONV_A = 3
CHUNK = 128
N_HEADS_B = 4
HEAD_B = 64
CONV_C = 31
POOL_WINDOWS = (2, 4, 8, 16)
POOL_MAX = 16
GROUP_D = 64
IN_COLS = 2048
N_EXPERTS = 8
PAST_LEN = 16384
EPS = 1e-6

LANES = 128
LANE_TILES = W_GROUP // LANES
HIST_A = 8
HIST_C = 32
HIST_D = 32
TL_MIX = 512
ROW_CHUNK = 64
TM_FFN = 512
SUBLANES = 8
N_SUPER = 4
TM_EXPERT = 256
IDX_ALIGN = 1024
VMEM_LIMIT = 56 * 1024 * 1024
VMEM_LIMIT_DENSE = 60 * 1024 * 1024
VMEM_LIMIT_EXPERT = 62 * 1024 * 1024

F32 = jnp.float32
BF16 = jnp.bfloat16


def _dot(a, b):
    return jnp.dot(a, b, preferred_element_type=F32)


def _rms(x, g):
    return x * lax.rsqrt(jnp.mean(x * x, axis=-1, keepdims=True) + EPS) * g


def _ln(x, g, b):
    xc = x - jnp.mean(x, axis=-1, keepdims=True)
    var = jnp.mean(xc * xc, axis=-1, keepdims=True)
    return xc * lax.rsqrt(var + EPS) * g + b


def _silu(x):
    return x * jax.nn.sigmoid(x)


def _lane_group_select(vals, shape):
    lane = lax.broadcasted_iota(jnp.int32, shape, 1)
    out = vals[3]
    for g in (2, 1, 0):
        out = jnp.where(lane < (g + 1) * GROUP_D, vals[g], out)
    return out


def _ada_kernel(cp_ref, cs_ref, w_ref, b_ref, op_ref, os_ref):
    w = w_ref[...].astype(BF16)
    b = b_ref[...]
    op_ref[...] = _dot(_silu(cp_ref[...]).astype(BF16), w) + b
    os_ref[...] = _dot(_silu(cs_ref[...]).astype(BF16), w) + b


def _ada(c_prompt, c_sample, w_ada, b_ada):
    n_p, n_s = c_prompt.shape[0], c_sample.shape[0]
    d = D_MODEL
    return pl.pallas_call(
        _ada_kernel,
        grid=(DEPTH, 6),
        in_specs=[
            pl.BlockSpec((n_p, d), lambda l, k: (0, 0)),
            pl.BlockSpec((n_s, d), lambda l, k: (0, 0)),
            pl.BlockSpec((None, d, d), lambda l, k: (l, 0, k)),
            pl.BlockSpec((None, 1, d), lambda l, k: (l, 0, k)),
        ],
        out_specs=[
            pl.BlockSpec((None, None, n_p, d), lambda l, k: (l, k, 0, 0)),
            pl.BlockSpec((None, None, n_s, d), lambda l, k: (l, k, 0, 0)),
        ],
        out_shape=[
            jax.ShapeDtypeStruct((DEPTH, 6, n_p, d), F32),
            jax.ShapeDtypeStruct((DEPTH, 6, n_s, d), F32),
        ],
        compiler_params=pltpu.CompilerParams(
            dimension_semantics=("arbitrary", "arbitrary"), vmem_limit_bytes=VMEM_LIMIT),
        name="ada",
    )(c_prompt, c_sample, w_ada, b_ada.reshape(DEPTH, 1, 6 * d))


def _masked_ws(ws_ref):
    r = lax.broadcasted_iota(jnp.int32, (CHUNK, CHUNK), 0)
    c = lax.broadcasted_iota(jnp.int32, (CHUNK, CHUNK), 1)
    return [jnp.where(c <= r, ws_ref[h], 0.0).astype(BF16) for h in range(N_HEADS_B)]


def _ext_rows(ref, j, row0, n):
    return ref[pl.ds(row0 * LANE_TILES + j, n, stride=LANE_TILES), :]


def _ext_store(ref, row0, val):
    for j in range(LANE_TILES):
        ref[pl.ds(row0 * LANE_TILES + j, val.shape[0], stride=LANE_TILES), :] = (
            val[:, j * LANES:(j + 1) * LANES])


def _ext_load(ref, row0, n):
    return jnp.concatenate([_ext_rows(ref, j, row0, n) for j in range(LANE_TILES)], axis=1)


def _merge_and_project(x, gate, outs, gout_ref, wout_bf):
    merged = jnp.concatenate(
        [_rms(o, gout_ref[i:i + 1, :]) for i, o in enumerate(outs)], axis=1).astype(BF16)
    return x + gate * _dot(merged, wout_bf)


def _to_token_tiles(ref, val, tm):
    for k in range(SUBLANES):
        ref[pl.ds(k, tm, stride=SUBLANES), :] = val[:, k * LANES:(k + 1) * LANES]


def _from_token_tiles(ref, tm):
    return jnp.concatenate(
        [ref[pl.ds(k, tm, stride=SUBLANES), :] for k in range(SUBLANES)], axis=1)


def _router_meta(h32, wr_ref, br_ref):
    wr = wr_ref[...]
    h_hi = h32.astype(BF16)
    h_lo = (h32 - h_hi.astype(F32)).astype(BF16)
    w_hi = wr.astype(BF16)
    w_lo = (wr - w_hi.astype(F32)).astype(BF16)
    logits = _dot(h_hi, w_hi) + (_dot(h_lo, w_hi) + _dot(h_hi, w_lo)) + br_ref[...]
    lane = lax.broadcasted_iota(jnp.int32, logits.shape, 1)
    lane_f = lane.astype(F32)
    neg = jnp.float32(-jnp.inf)
    logits = jnp.where(lane < N_EXPERTS, logits, neg)
    m1 = jnp.max(logits, axis=-1, keepdims=True)
    i1 = jnp.min(jnp.where(logits == m1, lane_f, float(LANES)), axis=-1, keepdims=True)
    rest = jnp.where(lane_f == i1, neg, logits)
    m2 = jnp.max(rest, axis=-1, keepdims=True)
    i2 = jnp.min(jnp.where(rest == m2, lane_f, float(LANES)), axis=-1, keepdims=True)
    e = jnp.exp(m2 - m1)
    g1 = 1.0 / (1.0 + e)
    g2 = e / (1.0 + e)
    comb = jnp.where(lane_f == i1, g1, 0.0) + jnp.where(lane_f == i2, g2, 0.0)
    flags = (jnp.where(lane_f == i1 + N_EXPERTS, 1.0, 0.0)
             + jnp.where(lane_f == i2 + N_EXPERTS, 1.0, 0.0))
    return comb + flags


def _mix_prompt_kernel(*refs, tl, start_pos, pre, post):
    refs = list(refs)
    take = lambda n: [refs.pop(0) for _ in range(n)]
    (x_ref,) = take(1)
    y_ref, gprev_ref = take(2) if pre else (None, None)
    shift_ref, scale_ref, gate_ref = take(3)
    shift2_ref, scale2_ref = take(2) if post else (None, None)
    (gmix_ref, win_ref, wout_ref, convaw_ref, lnvg_ref, lnvb_ref, ws_ref, bsfull_ref, convcw_ref,
     convcb_ref, lncg_ref, lncb_ref, poolw_ref, poolscale_ref, gout_ref) = take(15)
    gffn_ref, wr_ref, br_ref = take(3) if post else (None, None, None)
    xo_ref, na_ref, nc_ref, nd_ref, nv_ref = take(5)
    h_ref, meta_ref = take(2) if post else (None, None)
    win_bf, wout_bf, exta, extc, extd, ext2, ext4, ext8 = take(8)
    assert not refs
    b = pl.program_id(0)
    t = pl.program_id(1)
    last_t = pl.num_programs(1) - 1

    @pl.when(jnp.logical_and(b == 0, t == 0))
    def _cast_weights():
        rows = 128
        def body(i, carry):
            r0 = pl.multiple_of(i * rows, rows)
            win_bf[pl.ds(r0, rows), :] = win_ref[pl.ds(r0, rows), :].astype(BF16)
            wout_bf[pl.ds(r0, rows), :] = wout_ref[pl.ds(r0, rows), :].astype(BF16)
            return carry
        lax.fori_loop(0, D_MODEL // rows, body, 0)

    @pl.when(t == 0)
    def _zero_history():
        for ref, hist in ((exta, HIST_A), (extc, HIST_C), (extd, HIST_D)):
            ref[0:hist * LANE_TILES, :] = jnp.zeros((hist * LANE_TILES, LANES), F32)

    x = x_ref[...]
    if pre:
        x = x + gprev_ref[pl.ds(b, 1), :] * _from_token_tiles(y_ref, tl)
    shift = shift_ref[pl.ds(b, 1), :]
    scale = scale_ref[pl.ds(b, 1), :]
    gate = gate_ref[pl.ds(b, 1), :]
    h = _rms(x, gmix_ref[...] * (1.0 + scale)) + shift
    proj = _dot(h.astype(BF16), win_bf[...])
    a_b, a_c, a_h, b_u, b_v, c_a, c_g, d_p = [
        proj[:, i * W_GROUP:(i + 1) * W_GROUP] for i in range(8)]

    def dwconv(ext_ref, w_ref, hist, width):
        off = hist - (width - 1)
        halves = []
        for j in range(LANE_TILES):
            chunks = []
            for c0 in range(0, tl, ROW_CHUNK):
                acc = None
                for k in range(width):
                    term = (_ext_rows(ext_ref, j, off + c0 + k, ROW_CHUNK)
                            * w_ref[k:k + 1, j * LANES:(j + 1) * LANES])
                    acc = term if acc is None else acc + term
                chunks.append(acc)
            halves.append(jnp.concatenate(chunks, axis=0))
        return jnp.concatenate(halves, axis=1)

    _ext_store(exta, HIST_A, a_c * a_h)
    out_a = a_b * dwconv(exta, convaw_ref, HIST_A, CONV_A)

    v_n = _ln(b_v, lnvg_ref[...], lnvb_ref[...])
    v_bf = v_n.astype(BF16)
    wm = _masked_ws(ws_ref)
    lane = lax.broadcasted_iota(jnp.int32, (CHUNK, W_GROUP), 1)
    mixed_chunks = []
    for j in range(tl // CHUNK):
        vc = v_bf[j * CHUNK:(j + 1) * CHUNK, :]
        mixed = _dot(wm[3], vc)
        for hd in (2, 1, 0):
            mixed = jnp.where(lane < (hd + 1) * HEAD_B, _dot(wm[hd], vc), mixed)
        mixed_chunks.append(mixed + bsfull_ref[...])
    out_b = b_u * jnp.concatenate(mixed_chunks, axis=0)

    _ext_store(extc, HIST_C, c_a * jax.nn.sigmoid(c_g))
    y_c = dwconv(extc, convcw_ref, HIST_C, CONV_C) + convcb_ref[...]
    out_c = _silu(_ln(y_c, lncg_ref[...], lncb_ref[...]))

    n = HIST_D + tl
    _ext_store(extd, HIST_D, d_p)
    bufs = (extd, ext2, ext4, ext8)
    pos1 = start_pos + 1 + t * tl + lax.broadcasted_iota(jnp.int32, (tl, LANES), 0)
    low_group = lax.broadcasted_iota(jnp.int32, (tl, LANES), 1) < GROUP_D
    means = []
    for j in range(LANE_TILES):
        levels = 2 * (j + 1)
        for lv in range(levels - 1):
            first = 8 * (lv + 1)
            bufs[lv + 1][pl.ds(first * LANE_TILES + j, n - first, stride=LANE_TILES), :] = (
                _ext_rows(bufs[lv], j, first, n - first)
                + _ext_rows(bufs[lv], j, first - (1 << lv), n - first))
        prev = bufs[levels - 1]
        s_lo = _ext_rows(prev, j, HIST_D, tl)
        s_hi = s_lo + _ext_rows(prev, j, HIST_D - (1 << (levels - 1)), tl)
        w_lo, w_hi = POOL_WINDOWS[2 * j], POOL_WINDOWS[2 * j + 1]
        cnt = jnp.where(low_group, jnp.minimum(pos1, w_lo), jnp.minimum(pos1, w_hi)).astype(F32)
        means.append(jnp.where(low_group, s_lo, s_hi) / cnt)
    pooled = jnp.concatenate(means, axis=1) - d_p
    out_d = _dot(pooled.astype(BF16), poolw_ref[...].astype(BF16)) * poolscale_ref[...]

    x_new = _merge_and_project(x, gate, [out_a, out_b, out_c, out_d], gout_ref, wout_bf[...])
    xo_ref[...] = x_new
    if post:
        h2 = (_rms(x_new, gffn_ref[...]) * (1.0 + scale2_ref[pl.ds(b, 1), :])
              + shift2_ref[pl.ds(b, 1), :])
        _to_token_tiles(h_ref, h2, tl)
        meta_ref[...] = _router_meta(h2, wr_ref, br_ref)

    @pl.when(t == last_t)
    def _emit_state():
        na_ref[...] = _ext_load(exta, HIST_A + tl - (CONV_A - 1), CONV_A - 1)
        nc_ref[...] = _ext_load(extc, HIST_C + tl - (CONV_C - 1), CONV_C - 1)
        nd_ref[...] = _ext_load(extd, HIST_D + tl - (POOL_MAX - 1), POOL_MAX - 1)
        nv_ref[...] = v_n[tl - CHUNK:tl, :]

    for ref, hist in ((exta, HIST_A), (extc, HIST_C), (extd, HIST_D)):
        ref[0:hist * LANE_TILES, :] = ref[tl * LANE_TILES:(tl + hist) * LANE_TILES, :]


def _mix_params(p):
    r3 = lambda a: a.reshape(DEPTH, 1, -1)
    eye = jnp.eye(4, dtype=F32)
    pool_bd = (eye[None, :, None, :, None] * p['pool_w'][:, :, :, None, :]).reshape(
        DEPTH, W_GROUP, W_GROUP)
    return dict(
        gmix=r3(p['g_mix']), w_in=p['w_in'], w_out=p['w_out'], conva=p['conv_a_w'],
        lnvg=r3(p['ln_v_g']), lnvb=r3(p['ln_v_b']), ws=p['w_s'],
        bsfull=jnp.repeat(jnp.swapaxes(p['b_s'], 1, 2), HEAD_B, axis=2),
        ws0=r3(jnp.repeat(p['w_s'][:, :, 0, 0], HEAD_B, axis=1)),
        bs0=r3(jnp.repeat(p['b_s'][:, :, 0], HEAD_B, axis=1)),
        convc=p['conv_c_w'], convcb=r3(p['conv_c_b']), lncg=r3(p['ln_c_g']), lncb=r3(p['ln_c_b']),
        poolw=pool_bd, poolscale=r3(p['pool_scale']), gout=p['g_out'])


def _layer_spec(a, l):
    nd = a.ndim - 1
    return pl.BlockSpec((None,) + a.shape[1:], lambda *_: (l,) + (0,) * nd,
                        pipeline_mode=pl.Buffered(1))


def _const_spec(shape):
    nd = len(shape)
    return pl.BlockSpec(shape, lambda *_: (0,) * nd, pipeline_mode=pl.Buffered(1))


def _mix_prompt(x, mod_p, l, w, start_pos, prev_moe=None, wts=None):
    n_b, seq, d = x.shape
    tl = TL_MIX
    nt = seq // tl
    assert seq % tl == 0 and tl % CHUNK == 0 and seq >= CHUNK
    pre, post = prev_moe is not None, wts is not None
    mod_spec = lambda lay, k: pl.BlockSpec((None, None, n_b, d), lambda b, t: (lay, k, 0, 0))
    tiles_spec = lambda rows: pl.BlockSpec((tl * rows, LANES), lambda b, t: (b * nt + t, 0))
    weights = [w[k] for k in ('gmix', 'w_in', 'w_out', 'conva', 'lnvg', 'lnvb', 'ws', 'bsfull',
                              'convc', 'convcb', 'lncg', 'lncb', 'poolw', 'poolscale', 'gout')]
    state_spec = lambda r: pl.BlockSpec((None, r, W_GROUP), lambda b, t: (b, 0, 0))
    ext = lambda hist: pltpu.VMEM(((hist + tl) * LANE_TILES, LANES), F32)

    args, in_specs = [x], [pl.BlockSpec((None, tl, d), lambda b, t: (b, t, 0))]
    if pre:
        y_prev, l_prev = prev_moe
        args += [y_prev, mod_p]
        in_specs += [tiles_spec(SUBLANES), mod_spec(l_prev, 5)]
    args += [mod_p] * 3
    in_specs += [mod_spec(l, 0), mod_spec(l, 1), mod_spec(l, 2)]
    if post:
        args += [mod_p] * 2
        in_specs += [mod_spec(l, 3), mod_spec(l, 4)]
    args += weights
    in_specs += [_layer_spec(a, l) for a in weights]
    out_specs = [pl.BlockSpec((None, tl, d), lambda b, t: (b, t, 0)),
                 state_spec(CONV_A - 1), state_spec(CONV_C - 1), state_spec(POOL_MAX - 1),
                 state_spec(CHUNK)]
    out_shape = [jax.ShapeDtypeStruct(x.shape, F32),
                 jax.ShapeDtypeStruct((n_b, CONV_A - 1, W_GROUP), F32),
                 jax.ShapeDtypeStruct((n_b, CONV_C - 1, W_GROUP), F32),
                 jax.ShapeDtypeStruct((n_b, POOL_MAX - 1, W_GROUP), F32),
                 jax.ShapeDtypeStruct((n_b, CHUNK, W_GROUP), F32)]
    if post:
        i = l // 2
        args += [wts['g_ffn'], wts['w_router'], wts['b_router']]
        in_specs += [_layer_spec(wts['g_ffn'], l), _layer_spec(wts['w_router'], i),
                     _layer_spec(wts['b_router'], i)]
        out_specs += [tiles_spec(SUBLANES), tiles_spec(1)]
        out_shape += [jax.ShapeDtypeStruct((n_b * seq * SUBLANES, LANES), F32),
                      jax.ShapeDtypeStruct((n_b * seq, LANES), F32)]
    return pl.pallas_call(
        functools.partial(_mix_prompt_kernel, tl=tl, start_pos=start_pos, pre=pre, post=post),
        grid=(n_b, nt),
        in_specs=in_specs,
        out_specs=out_specs,
        out_shape=out_shape,
        scratch_shapes=[pltpu.VMEM((d, IN_COLS), BF16), pltpu.VMEM((d, d), BF16),
                        ext(HIST_A), ext(HIST_C), ext(HIST_D), ext(HIST_D), ext(HIST_D),
                        ext(HIST_D)],
        compiler_params=pltpu.CompilerParams(
            dimension_semantics=("arbitrary", "arbitrary"), vmem_limit_bytes=VMEM_LIMIT),
        name=f"mix_prompt_{l}",
    )(*args)


def _mix_sample_kernel(x_ref, shift_ref, scale_ref, gate_ref, sa_ref, sc_ref, sd_ref, gmix_ref,
                       win_ref, wout_ref, convaw_ref, lnvg_ref, lnvb_ref, ws0_ref, bs0_ref,
                       convcw_ref, convcb_ref, lncg_ref, lncb_ref, poolw_ref, poolscale_ref,
                       gout_ref, xo_ref, na_ref, nc_ref, nd_ref, nv_ref, *, start_pos):
    x = x_ref[...]
    h = _rms(x, gmix_ref[...]) * (1.0 + scale_ref[...]) + shift_ref[...]
    proj = _dot(h.astype(BF16), win_ref[...].astype(BF16))
    a_b, a_c, a_h, b_u, b_v, c_a, c_g, d_p = [
        proj[:, i * W_GROUP:(i + 1) * W_GROUP] for i in range(8)]

    ch = a_c * a_h
    y_a = convaw_ref[CONV_A - 1:CONV_A, :] * ch
    for k in range(CONV_A - 1):
        y_a = y_a + convaw_ref[k:k + 1, :] * sa_ref[k]
    out_a = a_b * y_a
    for k in range(CONV_A - 2):
        na_ref[k] = sa_ref[k + 1]
    na_ref[CONV_A - 2] = ch

    v_n = _ln(b_v, lnvg_ref[...], lnvb_ref[...])
    out_b = b_u * (ws0_ref[...] * v_n + bs0_ref[...])
    nv_ref[...] = v_n

    glu = c_a * jax.nn.sigmoid(c_g)
    y_c = convcw_ref[CONV_C - 1:CONV_C, :] * glu + convcb_ref[...]
    for k in range(CONV_C - 1):
        y_c = y_c + convcw_ref[k:k + 1, :] * sc_ref[k]
    out_c = _silu(_ln(y_c, lncg_ref[...], lncb_ref[...]))
    for k in range(CONV_C - 2):
        nc_ref[k] = sc_ref[k + 1]
    nc_ref[CONV_C - 2] = glu

    hist = POOL_MAX - 1
    run = d_p
    taken = 0
    sums = []
    for w in POOL_WINDOWS:
        while taken < w - 1:
            run = run + sd_ref[hist - 1 - taken]
            taken += 1
        sums.append(run / float(min(start_pos + 1, w)))
    pooled = _lane_group_select(sums, d_p.shape) - d_p
    out_d = _dot(pooled.astype(BF16), poolw_ref[...].astype(BF16)) * poolscale_ref[...]
    for k in range(hist - 1):
        nd_ref[k] = sd_ref[k + 1]
    nd_ref[hist - 1] = d_p

    xo_ref[...] = _merge_and_project(x, gate_ref[...], [out_a, out_b, out_c, out_d], gout_ref,
                                     wout_ref[...].astype(BF16))


def _mix_sample(x, mod_s, sa_t, sc_t, sd_t, l, w, start_pos):
    n, d = x.shape
    assert start_pos + 1 >= POOL_MAX
    mod_spec = lambda k: pl.BlockSpec((None, None, n, d), lambda i: (l, k, 0, 0))
    st_spec = lambda r: pl.BlockSpec((None, r, n, W_GROUP), lambda i: (l, 0, 0, 0))
    weights = [w[k] for k in ('gmix', 'w_in', 'w_out', 'conva', 'lnvg', 'lnvb', 'ws0', 'bs0',
                              'convc', 'convcb', 'lncg', 'lncb', 'poolw', 'poolscale', 'gout')]
    full = lambda shape: pl.BlockSpec(shape, lambda i: (0,) * len(shape))
    return pl.pallas_call(
        functools.partial(_mix_sample_kernel, start_pos=start_pos),
        grid=(1,),
        in_specs=[full((n, d)), mod_spec(0), mod_spec(1), mod_spec(2),
                  st_spec(CONV_A - 1), st_spec(CONV_C - 1), st_spec(POOL_MAX - 1)]
                 + [_layer_spec(a, l) for a in weights],
        out_specs=[full((n, d)), full((CONV_A - 1, n, W_GROUP)), full((CONV_C - 1, n, W_GROUP)),
                   full((POOL_MAX - 1, n, W_GROUP)), full((n, W_GROUP))],
        out_shape=[jax.ShapeDtypeStruct((n, d), F32),
                   jax.ShapeDtypeStruct((CONV_A - 1, n, W_GROUP), F32),
                   jax.ShapeDtypeStruct((CONV_C - 1, n, W_GROUP), F32),
                   jax.ShapeDtypeStruct((POOL_MAX - 1, n, W_GROUP), F32),
                   jax.ShapeDtypeStruct((n, W_GROUP), F32)],
        compiler_params=pltpu.CompilerParams(
            dimension_semantics=("arbitrary",), vmem_limit_bytes=VMEM_LIMIT),
        name=f"mix_sample_{l}",
    )(x, mod_s, mod_s, mod_s, sa_t, sc_t, sd_t, *weights)


def _modulation(refs, rows_per_seq, tm):
    if rows_per_seq == 1:
        return [r[...] for r in refs]
    b = (pl.program_id(0) * tm) // rows_per_seq
    return [r[pl.ds(b, 1), :] for r in refs]


def _ffn_dense_kernel(x_ref, shift_ref, scale_ref, gate_ref, gffn_ref, gfin_ref, w1_ref, w3_ref,
                      w2_ref, *rest, rows_per_seq, tm, final_norm):
    n_cast = (len(rest) - 1) // 2
    o_ref = rest[n_cast]
    x = x_ref[...]
    shift, scale, gate = _modulation([shift_ref, scale_ref, gate_ref], rows_per_seq, tm)
    h = (_rms(x, gffn_ref[...]) * (1.0 + scale) + shift).astype(BF16)
    act = (_silu(_dot(h, w1_ref[...])) * _dot(h, w3_ref[...])).astype(BF16)
    y = x + gate * _dot(act, w2_ref[...])
    o_ref[...] = _rms(y, gfin_ref[...]) if final_norm else y
    for src, dst in zip(rest[:n_cast], rest[n_cast + 1:]):
        dst[...] = src[...].astype(BF16)


def _mod_specs(mod, l, ks, rows_per_seq, tm):
    d = mod.shape[-1]
    if rows_per_seq == 1:
        return [pl.BlockSpec((None, None, tm, d), lambda t, k=k: (l, k, t, 0)) for k in ks]
    n_seq = mod.shape[2]
    return [pl.BlockSpec((None, None, n_seq, d), lambda t, k=k: (l, k, 0, 0)) for k in ks]


def _ffn_dense(x2d, mod, l, wts, rows_per_seq, final_norm, cast=()):
    m, d = x2d.shape
    tm = min(TM_FFN, m)
    assert m % tm == 0 and (rows_per_seq == 1 or rows_per_seq % tm == 0)
    i = l // 2
    steps = m // tm
    x_spec = pl.BlockSpec((tm, d), lambda t: (t, 0))
    w1, w3, w2 = wts['w1_dense'], wts['w3_dense'], wts['w2_dense']
    cast_in, cast_specs, cast_out_specs, cast_shapes = [], [], [], []
    for a, j in cast:
        rows, cols = a.shape[1] * a.shape[2], a.shape[3]
        assert rows % steps == 0
        cast_in.append(a.reshape(a.shape[0], rows, cols))
        cast_specs.append(pl.BlockSpec((None, rows // steps, cols), lambda t, j=j: (j, t, 0)))
        cast_out_specs.append(pl.BlockSpec((rows // steps, cols), lambda t: (t, 0)))
        cast_shapes.append(jax.ShapeDtypeStruct((rows, cols), BF16))
    out = pl.pallas_call(
        functools.partial(_ffn_dense_kernel, rows_per_seq=rows_per_seq, tm=tm,
                          final_norm=final_norm),
        grid=(steps,),
        in_specs=[x_spec] + _mod_specs(mod, l, (3, 4, 5), rows_per_seq, tm)
                 + [_layer_spec(wts['g_ffn'], l), _const_spec(wts['g_final'].shape),
                    _layer_spec(w1, i), _layer_spec(w3, i), _layer_spec(w2, i)] + cast_specs,
        out_specs=[x_spec] + cast_out_specs,
        out_shape=[jax.ShapeDtypeStruct((m, d), F32)] + cast_shapes,
        compiler_params=pltpu.CompilerParams(
            dimension_semantics=("arbitrary",), vmem_limit_bytes=VMEM_LIMIT_DENSE),
        name=f"ffn_dense_{l}",
    )(x2d, mod, mod, mod, wts['g_ffn'], wts['g_final'], w1, w3, w2, *cast_in)
    return out[0], [o.reshape(a.shape[1:]) for o, (a, _) in zip(out[1:], cast)]


def _moe_route_kernel(x_ref, shift_ref, scale_ref, gffn_ref, wr_ref, br_ref, h_ref, meta_ref, *,
                      rows_per_seq, tm):
    shift, scale = _modulation([shift_ref, scale_ref], rows_per_seq, tm)
    h32 = _rms(x_ref[...], gffn_ref[...]) * (1.0 + scale) + shift
    _to_token_tiles(h_ref, h32, tm)
    meta_ref[...] = _router_meta(h32, wr_ref, br_ref)


def _moe_route(x2d, mod, l, wts, rows_per_seq):
    m, d = x2d.shape
    tm = min(TM_FFN, m)
    assert m % tm == 0 and (rows_per_seq == 1 or rows_per_seq % tm == 0)
    i = l // 2
    return pl.pallas_call(
        functools.partial(_moe_route_kernel, rows_per_seq=rows_per_seq, tm=tm),
        grid=(m // tm,),
        in_specs=[pl.BlockSpec((tm, d), lambda t: (t, 0))]
                 + _mod_specs(mod, l, (3, 4), rows_per_seq, tm)
                 + [_layer_spec(wts['g_ffn'], l), _layer_spec(wts['w_router'], i),
                    _layer_spec(wts['b_router'], i)],
        out_specs=[pl.BlockSpec((tm * SUBLANES, LANES), lambda t: (t, 0)),
                   pl.BlockSpec((tm, LANES), lambda t: (t, 0))],
        out_shape=[jax.ShapeDtypeStruct((m * SUBLANES, LANES), F32),
                   jax.ShapeDtypeStruct((m, LANES), F32)],
        compiler_params=pltpu.CompilerParams(
            dimension_semantics=("arbitrary",), vmem_limit_bytes=VMEM_LIMIT),
        name=f"moe_route_{l}_{m}",
    )(x2d, mod, mod, wts['g_ffn'], wts['w_router'], wts['b_router'])


def _super_block_pieces(m_p, m_s, s_tok):
    pieces = []
    for k in range(N_SUPER):
        lo, hi = k * s_tok, (k + 1) * s_tok
        ps = []
        if lo < m_p:
            ps.append((0, lo, 0, min(hi, m_p) - lo))
        if hi > m_p:
            s0 = max(lo, m_p)
            ps.append((1, s0 - m_p, s0 - lo, hi - s0))
        pieces.append(ps)
    return pieces


def _moe_expert_kernel(cnt_ref, idx_hbm, g_hbm, hp_hbm, hs_hbm, w1_ref, w3_ref, w2_ref,
                       yp_hbm, ys_hbm, h_scr, y_scr, xbuf, obuf, idx_s, g_s, sem,
                       *, pieces, s_tok, s_pad, tm):
    sb = pl.program_id(0)
    e = pl.program_id(1)
    seg = sb * N_EXPERTS + e
    rows = s_tok * SUBLANES

    def piece_copies(k, to_vmem):
        copies = []
        for j, (grp, src_tok, dst_tok, n) in enumerate(pieces[k]):
            hbm = ((hp_hbm, hs_hbm) if to_vmem else (yp_hbm, ys_hbm))[grp]
            hbm = hbm.at[pl.ds(src_tok * SUBLANES, n * SUBLANES)]
            if to_vmem:
                copies.append(pltpu.make_async_copy(
                    hbm, h_scr.at[pl.ds(dst_tok * SUBLANES, n * SUBLANES)], sem.at[j]))
            else:
                copies.append(pltpu.make_async_copy(
                    y_scr.at[pl.ds(dst_tok * SUBLANES, n * SUBLANES)], hbm, sem.at[2 + j]))
        return copies

    def list_copies(s):
        slot = lax.rem(s, 2)
        src = pl.ds(pl.multiple_of(s * s_pad, IDX_ALIGN), s_pad)
        dst = pl.ds(pl.multiple_of(slot * s_pad, IDX_ALIGN), s_pad)
        return (pltpu.make_async_copy(idx_hbm.at[src], idx_s.at[dst], sem.at[4 + slot]),
                pltpu.make_async_copy(g_hbm.at[src], g_s.at[dst], sem.at[6 + slot]))

    @pl.when(seg == 0)
    def _first_lists():
        for c in list_copies(seg):
            c.start()
        obuf[...] = jnp.zeros(obuf.shape, F32)

    @pl.when(seg + 1 < N_SUPER * N_EXPERTS)
    def _next_lists():
        for c in list_copies(seg + 1):
            c.start()

    for k in range(N_SUPER):
        @pl.when(jnp.logical_and(sb == k, e == 0))
        def _load_super_block(k=k):
            for c in piece_copies(k, True):
                c.start()
            zrows = 256
            assert rows % zrows == 0
            def zero(i, carry):
                r0 = pl.multiple_of(i * zrows, zrows)
                y_scr[pl.ds(r0, zrows), :] = jnp.zeros((zrows, LANES), F32)
                return carry
            lax.fori_loop(0, rows // zrows, zero, 0)
            for c in piece_copies(k, True):
                c.wait()

    for c in list_copies(seg):
        c.wait()
    lst = lax.rem(seg, 2) * s_pad

    def gather(base):
        for r in range(tm):
            t8 = pl.multiple_of(idx_s[lst + base + r] * SUBLANES, SUBLANES)
            xbuf[r * SUBLANES:(r + 1) * SUBLANES, :] = h_scr[pl.ds(t8, SUBLANES), :]

    def scatter_add(base, limit):
        for r0 in range(0, tm, SUBLANES):
            upd = []
            for r in range(r0, r0 + SUBLANES):
                tok = idx_s[lst + base + r]
                t8 = pl.multiple_of(tok * SUBLANES, SUBLANES)
                g = jnp.where(base + r < limit, g_s[lst + tok], 0.0)
                o = obuf[r * SUBLANES:(r + 1) * SUBLANES, :]
                upd.append((t8, y_scr[pl.ds(t8, SUBLANES), :] + g * o))
            for t8, v in upd:
                y_scr[pl.ds(t8, SUBLANES), :] = v

    n_sel = cnt_ref[seg]
    n_tiles = (n_sel + tm - 1) // tm

    def tile(i, carry):
        x = _from_token_tiles(xbuf, tm).astype(BF16)
        gather((i + 1) * tm)
        scatter_add(jnp.maximum(i - 1, 0) * tm, jnp.where(i > 0, n_sel, 0))
        act = (_silu(_dot(x, w1_ref[...])) * _dot(x, w3_ref[...])).astype(BF16)
        _to_token_tiles(obuf, _dot(act, w2_ref[...]), tm)
        return carry

    gather(0)
    lax.fori_loop(0, n_tiles, tile, 0)

    @pl.when(n_tiles > 0)
    def _last_scatter():
        scatter_add((n_tiles - 1) * tm, n_sel)

    for k in range(N_SUPER):
        @pl.when(jnp.logical_and(sb == k, e == N_EXPERTS - 1))
        def _store_super_block(k=k):
            for c in piece_copies(k, False):
                c.start()
            for c in piece_copies(k, False):
                c.wait()


def _moe_experts(counts, idx, gates, h_p, h_s, l, w_bf, s_tok, s_pad):
    tm = TM_EXPERT
    w1, w3, w2 = w_bf
    w_spec = lambda a: pl.BlockSpec((None,) + a.shape[1:], lambda sb, e, cnt: (e, 0, 0))
    any_spec = pl.BlockSpec(memory_space=pl.ANY)
    rows = s_tok * SUBLANES
    pieces = _super_block_pieces(h_p.shape[0] // SUBLANES, h_s.shape[0] // SUBLANES, s_tok)
    return pl.pallas_call(
        functools.partial(_moe_expert_kernel, pieces=pieces, s_tok=s_tok, s_pad=s_pad, tm=tm),
        grid_spec=pltpu.PrefetchScalarGridSpec(
            num_scalar_prefetch=1,
            grid=(N_SUPER, N_EXPERTS),
            in_specs=[any_spec] * 4 + [w_spec(w1), w_spec(w3), w_spec(w2)],
            out_specs=[any_spec, any_spec],
            scratch_shapes=[pltpu.VMEM((rows, LANES), F32), pltpu.VMEM((rows, LANES), F32),
                            pltpu.VMEM((tm * SUBLANES, LANES), F32),
                            pltpu.VMEM((tm * SUBLANES, LANES), F32),
                            pltpu.SMEM((2 * s_pad,), jnp.int32), pltpu.SMEM((2 * s_pad,), F32),
                            pltpu.SemaphoreType.DMA((8,))]),
        out_shape=[jax.ShapeDtypeStruct(h_p.shape, F32), jax.ShapeDtypeStruct(h_s.shape, F32)],
        compiler_params=pltpu.CompilerParams(
            dimension_semantics=("arbitrary", "arbitrary"), vmem_limit_bytes=VMEM_LIMIT_EXPERT),
        name=f"moe_experts_{l}",
    )(counts, idx, gates, h_p, h_s, w1, w3, w2)


def _moe_residual_kernel(x_ref, gate_ref, gfin_ref, y_ref, o_ref, *, rows_per_seq, tm, final_norm):
    (gate,) = _modulation([gate_ref], rows_per_seq, tm)
    y = x_ref[...] + gate * _from_token_tiles(y_ref, tm)
    o_ref[...] = _rms(y, gfin_ref[...]) if final_norm else y


def _moe_residual(x2d, mod, y, l, wts, rows_per_seq, final_norm):
    m, d = x2d.shape
    tm = min(TM_FFN, m)
    assert m % tm == 0
    x_spec = pl.BlockSpec((tm, d), lambda t: (t, 0))
    return pl.pallas_call(
        functools.partial(_moe_residual_kernel, rows_per_seq=rows_per_seq, tm=tm,
                          final_norm=final_norm),
        grid=(m // tm,),
        in_specs=[x_spec] + _mod_specs(mod, l, (5,), rows_per_seq, tm)
                 + [_const_spec(wts['g_final'].shape),
                    pl.BlockSpec((tm * SUBLANES, LANES), lambda t: (t, 0))],
        out_specs=x_spec,
        out_shape=jax.ShapeDtypeStruct((m, d), F32),
        compiler_params=pltpu.CompilerParams(
            dimension_semantics=("arbitrary",), vmem_limit_bytes=VMEM_LIMIT),
        name=f"moe_residual_{l}_{m}",
    )(x2d, mod, wts['g_final'], y)


def _ffn_moe(routed_p, xs, mod_s, l, wts, w_bf):
    h_p, meta_p = routed_p
    m_p, m_s = meta_p.shape[0], xs.shape[0]
    n_tok = m_p + m_s
    s_tok = n_tok // N_SUPER
    assert s_tok * N_SUPER == n_tok and s_tok % SUBLANES == 0
    s_pad = -(-((-(-s_tok // TM_EXPERT) + 1) * TM_EXPERT) // IDX_ALIGN) * IDX_ALIGN

    h_s, meta_s = _moe_route(xs, mod_s, l, wts, 1)

    meta = jnp.concatenate([meta_p[:, :2 * N_EXPERTS], meta_s[:, :2 * N_EXPERTS]], axis=0)
    per_seg = lambda a: a.reshape(N_SUPER, s_tok, N_EXPERTS).transpose(0, 2, 1)
    gate = per_seg(meta[:, :N_EXPERTS])
    unsel = 1 - per_seg(meta[:, N_EXPERTS:]).astype(jnp.int32)
    tok = lax.broadcasted_iota(jnp.int32, unsel.shape, 2)
    tok_bits = (s_tok - 1).bit_length()
    idx = lax.sort(unsel * (1 << tok_bits) + tok, dimension=2) & ((1 << tok_bits) - 1)
    counts = (s_tok - jnp.sum(unsel, axis=2)).reshape(-1)
    pad = lambda a: jnp.pad(a, ((0, 0), (0, 0), (0, s_pad - s_tok))).reshape(-1)

    return _moe_experts(counts, pad(idx), pad(gate), h_p, h_s, l, w_bf, s_tok, s_pad)


def kernel(x_prompt, x_sample, state_conv_a, state_conv_c, state_pool_d, c_prompt, c_sample,
           w_ada, b_ada, g_mix, w_in, conv_a_w, ln_v_g, ln_v_b, w_s, b_s, conv_c_w, conv_c_b,
           ln_c_g, ln_c_b, pool_w, pool_scale, g_out, w_out, g_ffn, w1_dense, w3_dense, w2_dense,
           w_router, b_router, w1_moe, w3_moe, w2_moe, g_final):
    p = dict(g_mix=g_mix, w_in=w_in, conv_a_w=conv_a_w, ln_v_g=ln_v_g, ln_v_b=ln_v_b, w_s=w_s,
             b_s=b_s, conv_c_w=conv_c_w, conv_c_b=conv_c_b, ln_c_g=ln_c_g, ln_c_b=ln_c_b,
             pool_w=pool_w, pool_scale=pool_scale, g_out=g_out, w_out=w_out, g_ffn=g_ffn,
             w1_dense=w1_dense, w3_dense=w3_dense, w2_dense=w2_dense, w_router=w_router,
             b_router=b_router, w1_moe=w1_moe, w3_moe=w3_moe, w2_moe=w2_moe, g_final=g_final)
    n_p, seq, d = x_prompt.shape
    n_s, dec_seq, _ = x_sample.shape
    assert dec_seq == 1 and d == D_MODEL

    mod_p, mod_s = _ada(c_prompt, c_sample, w_ada, b_ada)
    kmajor = lambda s: jnp.transpose(s, (0, 2, 1, 3))
    sa_t, sc_t, sd_t = kmajor(state_conv_a), kmajor(state_conv_c), kmajor(state_pool_d)
    mixw = _mix_params(p)
    pad_e = LANES - N_EXPERTS
    wts = dict(
        g_ffn=g_ffn.reshape(DEPTH, 1, d), g_final=g_final.reshape(1, d),
        w1_dense=w1_dense.astype(BF16), w3_dense=w3_dense.astype(BF16),
        w2_dense=w2_dense.astype(BF16),
        w_router=jnp.pad(w_router, ((0, 0), (0, 0), (0, pad_e))),
        b_router=jnp.pad(b_router, ((0, 0), (0, pad_e))).reshape(-1, 1, LANES),
        w1_moe=w1_moe, w3_moe=w3_moe, w2_moe=w2_moe)

    xp = x_prompt
    xs = x_sample.reshape(n_s, d)
    states_p = [[], [], [], []]
    states_s = [[], [], [], []]
    pending = None
    moe_bf = None
    for l in range(DEPTH):
        last = l == DEPTH - 1
        moe = l % 2 == 1
        xp, *st_p = _mix_prompt(xp, mod_p, l, mixw, 0, pending, wts if moe else None)
        pending = None
        xs, *st_s = _mix_sample(xs, mod_s, sa_t, sc_t, sd_t, l, mixw, PAST_LEN)
        for acc, s in zip(states_p, st_p[:4]):
            acc.append(s)
        for acc, s in zip(states_s, st_s):
            acc.append(s)
        if moe:
            if moe_bf is None:
                moe_bf = [wts[k][l // 2].astype(BF16) for k in ('w1_moe', 'w3_moe', 'w2_moe')]
            y_p, y_s = _ffn_moe(st_p[4:], xs, mod_s, l, wts, moe_bf)
            moe_bf = None
            xs = _moe_residual(xs, mod_s, y_s, l, wts, 1, last)
            if last:
                xp = _moe_residual(xp.reshape(n_p * seq, d), mod_p, y_p, l, wts, seq,
                                   True).reshape(n_p, seq, d)
            else:
                pending = (y_p, l)
        else:
            cast = ([(wts[k], (l + 1) // 2) for k in ('w1_moe', 'w3_moe', 'w2_moe')]
                    if l + 1 < DEPTH else [])
            xp2d, moe_bf = _ffn_dense(xp.reshape(n_p * seq, d), mod_p, l, wts, seq, last, cast)
            xp = xp2d.reshape(n_p, seq, d)
            xs, _ = _ffn_dense(xs, mod_s, l, wts, 1, last)
            moe_bf = moe_bf or None

    a_p, c_p, d_p, v_p = (jnp.stack(s) for s in states_p)
    a_s, c_s, d_s = (kmajor(jnp.stack(s)) for s in states_s[:3])
    v_s = jnp.stack(states_s[3]).reshape(DEPTH, n_s, 1, W_GROUP)
    return (xp, xs.reshape(n_s, 1, d), a_p, c_p, d_p, v_p, a_s, c_s, d_s, v_s)
```

```python
import functools

import jax
import jax.numpy as jnp
from jax import lax
from jax.experimental import pallas as pl
from jax.experimental.pallas import tpu as pltpu

D_MODEL = 1024
DEPTH = 4
W_GROUP = 256
CONV_A = 3
CHUNK = 128
N_HEADS_B = 4
HEAD_B = 64
CONV_C = 31
POOL_WINDOWS = (2, 4, 8, 16)
POOL_MAX = 16
GROUP_D = 64
IN_COLS = 2048
N_EXPERTS = 8
PAST_LEN = 16384
EPS = 1e-6

LANES = 128
LANE_TILES = W_GROUP // LANES
HIST_A = 8
HIST_C = 32
HIST_D = 32
TL_MIX = 512
ROW_CHUNK = 64
TM_FFN = 512
SUBLANES = 8
N_SUPER = 4
TM_EXPERT = 256
IDX_ALIGN = 1024
VMEM_LIMIT = 56 * 1024 * 1024
VMEM_LIMIT_DENSE = 60 * 1024 * 1024
VMEM_LIMIT_EXPERT = 62 * 1024 * 1024

F32 = jnp.float32
BF16 = jnp.bfloat16


def _dot(a, b):
    return jnp.dot(a, b, preferred_element_type=F32)


def _rms(x, g):
    return x * lax.rsqrt(jnp.mean(x * x, axis=-1, keepdims=True) + EPS) * g


def _ln(x, g, b):
    xc = x - jnp.mean(x, axis=-1, keepdims=True)
    var = jnp.mean(xc * xc, axis=-1, keepdims=True)
    return xc * lax.rsqrt(var + EPS) * g + b


def _silu(x):
    return x * jax.nn.sigmoid(x)


def _lane_group_select(vals, shape):
    lane = lax.broadcasted_iota(jnp.int32, shape, 1)
    out = vals[3]
    for g in (2, 1, 0):
        out = jnp.where(lane < (g + 1) * GROUP_D, vals[g], out)
    return out


def _ada_kernel(cp_ref, cs_ref, w_ref, b_ref, op_ref, os_ref):
    w = w_ref[...].astype(BF16)
    b = b_ref[...]
    op_ref[...] = _dot(_silu(cp_ref[...]).astype(BF16), w) + b
    os_ref[...] = _dot(_silu(cs_ref[...]).astype(BF16), w) + b


def _ada(c_prompt, c_sample, w_ada, b_ada):
    n_p, n_s = c_prompt.shape[0], c_sample.shape[0]
    d = D_MODEL
    return pl.pallas_call(
        _ada_kernel,
        grid=(DEPTH, 6),
        in_specs=[
            pl.BlockSpec((n_p, d), lambda l, k: (0, 0)),
            pl.BlockSpec((n_s, d), lambda l, k: (0, 0)),
            pl.BlockSpec((None, d, d), lambda l, k: (l, 0, k)),
            pl.BlockSpec((None, 1, d), lambda l, k: (l, 0, k)),
        ],
        out_specs=[
            pl.BlockSpec((None, None, n_p, d), lambda l, k: (l, k, 0, 0)),
            pl.BlockSpec((None, None, n_s, d), lambda l, k: (l, k, 0, 0)),
        ],
        out_shape=[
            jax.ShapeDtypeStruct((DEPTH, 6, n_p, d), F32),
            jax.ShapeDtypeStruct((DEPTH, 6, n_s, d), F32),
        ],
        compiler_params=pltpu.CompilerParams(
            dimension_semantics=("arbitrary", "arbitrary"), vmem_limit_bytes=VMEM_LIMIT),
        name="ada",
    )(c_prompt, c_sample, w_ada, b_ada.reshape(DEPTH, 1, 6 * d))


def _masked_ws(ws_ref):
    r = lax.broadcasted_iota(jnp.int32, (CHUNK, CHUNK), 0)
    c = lax.broadcasted_iota(jnp.int32, (CHUNK, CHUNK), 1)
    return [jnp.where(c <= r, ws_ref[h], 0.0).astype(BF16) for h in range(N_HEADS_B)]


def _ext_rows(ref, j, row0, n):
    return ref[pl.ds(row0 * LANE_TILES + j, n, stride=LANE_TILES), :]


def _ext_store(ref, row0, val):
    for j in range(LANE_TILES):
        ref[pl.ds(row0 * LANE_TILES + j, val.shape[0], stride=LANE_TILES), :] = (
            val[:, j * LANES:(j + 1) * LANES])


def _ext_load(ref, row0, n):
    return jnp.concatenate([_ext_rows(ref, j, row0, n) for j in range(LANE_TILES)], axis=1)


def _merge_and_project(x, gate, outs, gout_ref, wout_bf):
    merged = jnp.concatenate(
        [_rms(o, gout_ref[i:i + 1, :]) for i, o in enumerate(outs)], axis=1).astype(BF16)
    return x + gate * _dot(merged, wout_bf)


def _to_token_tiles(ref, val, tm):
    for k in range(SUBLANES):
        ref[pl.ds(k, tm, stride=SUBLANES), :] = val[:, k * LANES:(k + 1) * LANES]


def _from_token_tiles(ref, tm):
    return jnp.concatenate(
        [ref[pl.ds(k, tm, stride=SUBLANES), :] for k in range(SUBLANES)], axis=1)


def _router_meta(h32, wr_ref, br_ref):
    wr = wr_ref[...]
    h_hi = h32.astype(BF16)
    h_lo = (h32 - h_hi.astype(F32)).astype(BF16)
    w_hi = wr.astype(BF16)
    w_lo = (wr - w_hi.astype(F32)).astype(BF16)
    logits = _dot(h_hi, w_hi) + (_dot(h_lo, w_hi) + _dot(h_hi, w_lo)) + br_ref[...]
    lane = lax.broadcasted_iota(jnp.int32, logits.shape, 1)
    lane_f = lane.astype(F32)
    neg = jnp.float32(-jnp.inf)
    logits = jnp.where(lane < N_EXPERTS, logits, neg)
    m1 = jnp.max(logits, axis=-1, keepdims=True)
    i1 = jnp.min(jnp.where(logits == m1, lane_f, float(LANES)), axis=-1, keepdims=True)
    rest = jnp.where(lane_f == i1, neg, logits)
    m2 = jnp.max(rest, axis=-1, keepdims=True)
    i2 = jnp.min(jnp.where(rest == m2, lane_f, float(LANES)), axis=-1, keepdims=True)
    e = jnp.exp(m2 - m1)
    g1 = 1.0 / (1.0 + e)
    g2 = e / (1.0 + e)
    comb = jnp.where(lane_f == i1, g1, 0.0) + jnp.where(lane_f == i2, g2, 0.0)
    flags = (jnp.where(lane_f == i1 + N_EXPERTS, 1.0, 0.0)
             + jnp.where(lane_f == i2 + N_EXPERTS, 1.0, 0.0))
    chosen = (jnp.where(lane == 2 * N_EXPERTS, i1, 0.0)
              + jnp.where(lane == 2 * N_EXPERTS + 1, i2, 0.0))
    return comb + flags + chosen


def _mix_prompt_kernel(*refs, tl, start_pos, pre, post):
    refs = list(refs)
    take = lambda n: [refs.pop(0) for _ in range(n)]
    (x_ref,) = take(1)
    y_ref, gprev_ref = take(2) if pre else (None, None)
    shift_ref, scale_ref, gate_ref = take(3)
    shift2_ref, scale2_ref = take(2) if post else (None, None)
    (gmix_ref, win_ref, wout_ref, convaw_ref, lnvg_ref, lnvb_ref, ws_ref, bsfull_ref, convcw_ref,
     convcb_ref, lncg_ref, lncb_ref, poolw_ref, poolscale_ref, gout_ref) = take(15)
    gffn_ref, wr_ref, br_ref = take(3) if post else (None, None, None)
    xo_ref, na_ref, nc_ref, nd_ref, nv_ref = take(5)
    h_ref, meta_ref = take(2) if post else (None, None)
    win_bf, wout_bf, exta, extc, extd, ext2, ext4, ext8 = take(8)
    assert not refs
    b = pl.program_id(0)
    t = pl.program_id(1)
    last_t = pl.num_programs(1) - 1

    @pl.when(jnp.logical_and(b == 0, t == 0))
    def _cast_weights():
        rows = 128
        def body(i, carry):
            r0 = pl.multiple_of(i * rows, rows)
            win_bf[pl.ds(r0, rows), :] = win_ref[pl.ds(r0, rows), :].astype(BF16)
            wout_bf[pl.ds(r0, rows), :] = wout_ref[pl.ds(r0, rows), :].astype(BF16)
            return carry
        lax.fori_loop(0, D_MODEL // rows, body, 0)

    @pl.when(t == 0)
    def _zero_history():
        for ref, hist in ((exta, HIST_A), (extc, HIST_C), (extd, HIST_D)):
            ref[0:hist * LANE_TILES, :] = jnp.zeros((hist * LANE_TILES, LANES), F32)

    x = x_ref[...]
    if pre:
        x = x + gprev_ref[pl.ds(b, 1), :] * _from_token_tiles(y_ref, tl)
    shift = shift_ref[pl.ds(b, 1), :]
    scale = scale_ref[pl.ds(b, 1), :]
    gate = gate_ref[pl.ds(b, 1), :]
    h = _rms(x, gmix_ref[...] * (1.0 + scale)) + shift
    proj = _dot(h.astype(BF16), win_bf[...])
    a_b, a_c, a_h, b_u, b_v, c_a, c_g, d_p = [
        proj[:, i * W_GROUP:(i + 1) * W_GROUP] for i in range(8)]

    def dwconv(ext_ref, w_ref, hist, width):
        off = hist - (width - 1)
        halves = []
        for j in range(LANE_TILES):
            chunks = []
            for c0 in range(0, tl, ROW_CHUNK):
                acc = None
                for k in range(width):
                    term = (_ext_rows(ext_ref, j, off + c0 + k, ROW_CHUNK)
                            * w_ref[k:k + 1, j * LANES:(j + 1) * LANES])
                    acc = term if acc is None else acc + term
                chunks.append(acc)
            halves.append(jnp.concatenate(chunks, axis=0))
        return jnp.concatenate(halves, axis=1)

    _ext_store(exta, HIST_A, a_c * a_h)
    out_a = a_b * dwconv(exta, convaw_ref, HIST_A, CONV_A)

    v_n = _ln(b_v, lnvg_ref[...], lnvb_ref[...])
    v_bf = v_n.astype(BF16)
    wm = _masked_ws(ws_ref)
    lane = lax.broadcasted_iota(jnp.int32, (CHUNK, W_GROUP), 1)
    mixed_chunks = []
    for j in range(tl // CHUNK):
        vc = v_bf[j * CHUNK:(j + 1) * CHUNK, :]
        mixed = _dot(wm[3], vc)
        for hd in (2, 1, 0):
            mixed = jnp.where(lane < (hd + 1) * HEAD_B, _dot(wm[hd], vc), mixed)
        mixed_chunks.append(mixed + bsfull_ref[...])
    out_b = b_u * jnp.concatenate(mixed_chunks, axis=0)

    _ext_store(extc, HIST_C, c_a * jax.nn.sigmoid(c_g))
    y_c = dwconv(extc, convcw_ref, HIST_C, CONV_C) + convcb_ref[...]
    out_c = _silu(_ln(y_c, lncg_ref[...], lncb_ref[...]))

    n = HIST_D + tl
    _ext_store(extd, HIST_D, d_p)
    bufs = (extd, ext2, ext4, ext8)
    pos1 = start_pos + 1 + t * tl + lax.broadcasted_iota(jnp.int32, (tl, LANES), 0)
    low_group = lax.broadcasted_iota(jnp.int32, (tl, LANES), 1) < GROUP_D
    means = []
    for j in range(LANE_TILES):
        levels = 2 * (j + 1)
        for lv in range(levels - 1):
            first = 8 * (lv + 1)
            bufs[lv + 1][pl.ds(first * LANE_TILES + j, n - first, stride=LANE_TILES), :] = (
                _ext_rows(bufs[lv], j, first, n - first)
                + _ext_rows(bufs[lv], j, first - (1 << lv), n - first))
        prev = bufs[levels - 1]
        s_lo = _ext_rows(prev, j, HIST_D, tl)
        s_hi = s_lo + _ext_rows(prev, j, HIST_D - (1 << (levels - 1)), tl)
        w_lo, w_hi = POOL_WINDOWS[2 * j], POOL_WINDOWS[2 * j + 1]
        cnt = jnp.where(low_group, jnp.minimum(pos1, w_lo), jnp.minimum(pos1, w_hi)).astype(F32)
        means.append(jnp.where(low_group, s_lo, s_hi) / cnt)
    pooled = jnp.concatenate(means, axis=1) - d_p
    out_d = _dot(pooled.astype(BF16), poolw_ref[...].astype(BF16)) * poolscale_ref[...]

    x_new = _merge_and_project(x, gate, [out_a, out_b, out_c, out_d], gout_ref, wout_bf[...])
    xo_ref[...] = x_new
    if post:
        h2 = (_rms(x_new, gffn_ref[...]) * (1.0 + scale2_ref[pl.ds(b, 1), :])
              + shift2_ref[pl.ds(b, 1), :])
        _to_token_tiles(h_ref, h2, tl)
        meta_ref[...] = _router_meta(h2, wr_ref, br_ref)

    @pl.when(t == last_t)
    def _emit_state():
        na_ref[...] = _ext_load(exta, HIST_A + tl - (CONV_A - 1), CONV_A - 1)
        nc_ref[...] = _ext_load(extc, HIST_C + tl - (CONV_C - 1), CONV_C - 1)
        nd_ref[...] = _ext_load(extd, HIST_D + tl - (POOL_MAX - 1), POOL_MAX - 1)
        nv_ref[...] = v_n[tl - CHUNK:tl, :]

    for ref, hist in ((exta, HIST_A), (extc, HIST_C), (extd, HIST_D)):
        ref[0:hist * LANE_TILES, :] = ref[tl * LANE_TILES:(tl + hist) * LANE_TILES, :]


def _mix_params(p):
    r3 = lambda a: a.reshape(DEPTH, 1, -1)
    eye = jnp.eye(4, dtype=F32)
    pool_bd = (eye[None, :, None, :, None] * p['pool_w'][:, :, :, None, :]).reshape(
        DEPTH, W_GROUP, W_GROUP)
    return dict(
        gmix=r3(p['g_mix']), w_in=p['w_in'], w_out=p['w_out'], conva=p['conv_a_w'],
        lnvg=r3(p['ln_v_g']), lnvb=r3(p['ln_v_b']), ws=p['w_s'],
        bsfull=jnp.repeat(jnp.swapaxes(p['b_s'], 1, 2), HEAD_B, axis=2),
        ws0=r3(jnp.repeat(p['w_s'][:, :, 0, 0], HEAD_B, axis=1)),
        bs0=r3(jnp.repeat(p['b_s'][:, :, 0], HEAD_B, axis=1)),
        convc=p['conv_c_w'], convcb=r3(p['conv_c_b']), lncg=r3(p['ln_c_g']), lncb=r3(p['ln_c_b']),
        poolw=pool_bd, poolscale=r3(p['pool_scale']), gout=p['g_out'])


def _layer_spec(a, l):
    nd = a.ndim - 1
    return pl.BlockSpec((None,) + a.shape[1:], lambda *_: (l,) + (0,) * nd,
                        pipeline_mode=pl.Buffered(1))


def _const_spec(shape):
    nd = len(shape)
    return pl.BlockSpec(shape, lambda *_: (0,) * nd, pipeline_mode=pl.Buffered(1))


def _mix_prompt(x, mod_p, l, w, start_pos, prev_moe=None, wts=None):
    n_b, seq, d = x.shape
    tl = TL_MIX
    nt = seq // tl
    assert seq % tl == 0 and tl % CHUNK == 0 and seq >= CHUNK
    pre, post = prev_moe is not None, wts is not None
    mod_spec = lambda lay, k: pl.BlockSpec((None, None, n_b, d), lambda b, t: (lay, k, 0, 0))
    tiles_spec = lambda rows: pl.BlockSpec((tl * rows, LANES), lambda b, t: (b * nt + t, 0))
    weights = [w[k] for k in ('gmix', 'w_in', 'w_out', 'conva', 'lnvg', 'lnvb', 'ws', 'bsfull',
                              'convc', 'convcb', 'lncg', 'lncb', 'poolw', 'poolscale', 'gout')]
    state_spec = lambda r: pl.BlockSpec((None, r, W_GROUP), lambda b, t: (b, 0, 0))
    ext = lambda hist: pltpu.VMEM(((hist + tl) * LANE_TILES, LANES), F32)

    args, in_specs = [x], [pl.BlockSpec((None, tl, d), lambda b, t: (b, t, 0))]
    if pre:
        y_prev, l_prev = prev_moe
        args += [y_prev, mod_p]
        in_specs += [tiles_spec(SUBLANES), mod_spec(l_prev, 5)]
    args += [mod_p] * 3
    in_specs += [mod_spec(l, 0), mod_spec(l, 1), mod_spec(l, 2)]
    if post:
        args += [mod_p] * 2
        in_specs += [mod_spec(l, 3), mod_spec(l, 4)]
    args += weights
    in_specs += [_layer_spec(a, l) for a in weights]
    out_specs = [pl.BlockSpec((None, tl, d), lambda b, t: (b, t, 0)),
                 state_spec(CONV_A - 1), state_spec(CONV_C - 1), state_spec(POOL_MAX - 1),
                 state_spec(CHUNK)]
    out_shape = [jax.ShapeDtypeStruct(x.shape, F32),
                 jax.ShapeDtypeStruct((n_b, CONV_A - 1, W_GROUP), F32),
                 jax.ShapeDtypeStruct((n_b, CONV_C - 1, W_GROUP), F32),
                 jax.ShapeDtypeStruct((n_b, POOL_MAX - 1, W_GROUP), F32),
                 jax.ShapeDtypeStruct((n_b, CHUNK, W_GROUP), F32)]
    if post:
        i = l // 2
        args += [wts['g_ffn'], wts['w_router'], wts['b_router']]
        in_specs += [_layer_spec(wts['g_ffn'], l), _layer_spec(wts['w_router'], i),
                     _layer_spec(wts['b_router'], i)]
        out_specs += [tiles_spec(SUBLANES), tiles_spec(1)]
        out_shape += [jax.ShapeDtypeStruct((n_b * seq * SUBLANES, LANES), F32),
                      jax.ShapeDtypeStruct((n_b * seq, LANES), F32)]
    return pl.pallas_call(
        functools.partial(_mix_prompt_kernel, tl=tl, start_pos=start_pos, pre=pre, post=post),
        grid=(n_b, nt),
        in_specs=in_specs,
        out_specs=out_specs,
        out_shape=out_shape,
        scratch_shapes=[pltpu.VMEM((d, IN_COLS), BF16), pltpu.VMEM((d, d), BF16),
                        ext(HIST_A), ext(HIST_C), ext(HIST_D), ext(HIST_D), ext(HIST_D),
                        ext(HIST_D)],
        compiler_params=pltpu.CompilerParams(
            dimension_semantics=("arbitrary", "arbitrary"), vmem_limit_bytes=VMEM_LIMIT),
        name=f"mix_prompt_{l}",
    )(*args)


def _mix_sample_kernel(x_ref, shift_ref, scale_ref, gate_ref, sa_ref, sc_ref, sd_ref, gmix_ref,
                       win_ref, wout_ref, convaw_ref, lnvg_ref, lnvb_ref, ws0_ref, bs0_ref,
                       convcw_ref, convcb_ref, lncg_ref, lncb_ref, poolw_ref, poolscale_ref,
                       gout_ref, xo_ref, na_ref, nc_ref, nd_ref, nv_ref, *, start_pos):
    x = x_ref[...]
    h = _rms(x, gmix_ref[...]) * (1.0 + scale_ref[...]) + shift_ref[...]
    proj = _dot(h.astype(BF16), win_ref[...].astype(BF16))
    a_b, a_c, a_h, b_u, b_v, c_a, c_g, d_p = [
        proj[:, i * W_GROUP:(i + 1) * W_GROUP] for i in range(8)]

    ch = a_c * a_h
    y_a = convaw_ref[CONV_A - 1:CONV_A, :] * ch
    for k in range(CONV_A - 1):
        y_a = y_a + convaw_ref[k:k + 1, :] * sa_ref[k]
    out_a = a_b * y_a
    for k in range(CONV_A - 2):
        na_ref[k] = sa_ref[k + 1]
    na_ref[CONV_A - 2] = ch

    v_n = _ln(b_v, lnvg_ref[...], lnvb_ref[...])
    out_b = b_u * (ws0_ref[...] * v_n + bs0_ref[...])
    nv_ref[...] = v_n

    glu = c_a * jax.nn.sigmoid(c_g)
    y_c = convcw_ref[CONV_C - 1:CONV_C, :] * glu + convcb_ref[...]
    for k in range(CONV_C - 1):
        y_c = y_c + convcw_ref[k:k + 1, :] * sc_ref[k]
    out_c = _silu(_ln(y_c, lncg_ref[...], lncb_ref[...]))
    for k in range(CONV_C - 2):
        nc_ref[k] = sc_ref[k + 1]
    nc_ref[CONV_C - 2] = glu

    hist = POOL_MAX - 1
    run = d_p
    taken = 0
    sums = []
    for w in POOL_WINDOWS:
        while taken < w - 1:
            run = run + sd_ref[hist - 1 - taken]
            taken += 1
        sums.append(run / float(min(start_pos + 1, w)))
    pooled = _lane_group_select(sums, d_p.shape) - d_p
    out_d = _dot(pooled.astype(BF16), poolw_ref[...].astype(BF16)) * poolscale_ref[...]
    for k in range(hist - 1):
        nd_ref[k] = sd_ref[k + 1]
    nd_ref[hist - 1] = d_p

    xo_ref[...] = _merge_and_project(x, gate_ref[...], [out_a, out_b, out_c, out_d], gout_ref,
                                     wout_ref[...].astype(BF16))


def _mix_sample(x, mod_s, sa_t, sc_t, sd_t, l, w, start_pos):
    n, d = x.shape
    assert start_pos + 1 >= POOL_MAX
    mod_spec = lambda k: pl.BlockSpec((None, None, n, d), lambda i: (l, k, 0, 0))
    st_spec = lambda r: pl.BlockSpec((None, r, n, W_GROUP), lambda i: (l, 0, 0, 0))
    weights = [w[k] for k in ('gmix', 'w_in', 'w_out', 'conva', 'lnvg', 'lnvb', 'ws0', 'bs0',
                              'convc', 'convcb', 'lncg', 'lncb', 'poolw', 'poolscale', 'gout')]
    full = lambda shape: pl.BlockSpec(shape, lambda i: (0,) * len(shape))
    return pl.pallas_call(
        functools.partial(_mix_sample_kernel, start_pos=start_pos),
        grid=(1,),
        in_specs=[full((n, d)), mod_spec(0), mod_spec(1), mod_spec(2),
                  st_spec(CONV_A - 1), st_spec(CONV_C - 1), st_spec(POOL_MAX - 1)]
                 + [_layer_spec(a, l) for a in weights],
        out_specs=[full((n, d)), full((CONV_A - 1, n, W_GROUP)), full((CONV_C - 1, n, W_GROUP)),
                   full((POOL_MAX - 1, n, W_GROUP)), full((n, W_GROUP))],
        out_shape=[jax.ShapeDtypeStruct((n, d), F32),
                   jax.ShapeDtypeStruct((CONV_A - 1, n, W_GROUP), F32),
                   jax.ShapeDtypeStruct((CONV_C - 1, n, W_GROUP), F32),
                   jax.ShapeDtypeStruct((POOL_MAX - 1, n, W_GROUP), F32),
                   jax.ShapeDtypeStruct((n, W_GROUP), F32)],
        compiler_params=pltpu.CompilerParams(
            dimension_semantics=("arbitrary",), vmem_limit_bytes=VMEM_LIMIT),
        name=f"mix_sample_{l}",
    )(x, mod_s, mod_s, mod_s, sa_t, sc_t, sd_t, *weights)


def _modulation(refs, rows_per_seq, tm):
    if rows_per_seq == 1:
        return [r[...] for r in refs]
    b = (pl.program_id(0) * tm) // rows_per_seq
    return [r[pl.ds(b, 1), :] for r in refs]


def _ffn_dense_kernel(x_ref, shift_ref, scale_ref, gate_ref, gffn_ref, gfin_ref, w1_ref, w3_ref,
                      w2_ref, *rest, rows_per_seq, tm, final_norm):
    n_cast = (len(rest) - 1) // 2
    o_ref = rest[n_cast]
    x = x_ref[...]
    shift, scale, gate = _modulation([shift_ref, scale_ref, gate_ref], rows_per_seq, tm)
    h = (_rms(x, gffn_ref[...]) * (1.0 + scale) + shift).astype(BF16)
    act = (_silu(_dot(h, w1_ref[...])) * _dot(h, w3_ref[...])).astype(BF16)
    y = x + gate * _dot(act, w2_ref[...])
    o_ref[...] = _rms(y, gfin_ref[...]) if final_norm else y
    for src, dst in zip(rest[:n_cast], rest[n_cast + 1:]):
        dst[...] = src[...].astype(BF16)


def _mod_specs(mod, l, ks, rows_per_seq, tm):
    d = mod.shape[-1]
    if rows_per_seq == 1:
        return [pl.BlockSpec((None, None, tm, d), lambda t, k=k: (l, k, t, 0)) for k in ks]
    n_seq = mod.shape[2]
    return [pl.BlockSpec((None, None, n_seq, d), lambda t, k=k: (l, k, 0, 0)) for k in ks]


def _ffn_dense(x2d, mod, l, wts, rows_per_seq, final_norm, cast=()):
    m, d = x2d.shape
    tm = min(TM_FFN, m)
    assert m % tm == 0 and (rows_per_seq == 1 or rows_per_seq % tm == 0)
    i = l // 2
    steps = m // tm
    x_spec = pl.BlockSpec((tm, d), lambda t: (t, 0))
    w1, w3, w2 = wts['w1_dense'], wts['w3_dense'], wts['w2_dense']
    cast_in, cast_specs, cast_out_specs, cast_shapes = [], [], [], []
    for a, j in cast:
        rows, cols = a.shape[1] * a.shape[2], a.shape[3]
        assert rows % steps == 0
        cast_in.append(a.reshape(a.shape[0], rows, cols))
        cast_specs.append(pl.BlockSpec((None, rows // steps, cols), lambda t, j=j: (j, t, 0)))
        cast_out_specs.append(pl.BlockSpec((rows // steps, cols), lambda t: (t, 0)))
        cast_shapes.append(jax.ShapeDtypeStruct((rows, cols), BF16))
    out = pl.pallas_call(
        functools.partial(_ffn_dense_kernel, rows_per_seq=rows_per_seq, tm=tm,
                          final_norm=final_norm),
        grid=(steps,),
        in_specs=[x_spec] + _mod_specs(mod, l, (3, 4, 5), rows_per_seq, tm)
                 + [_layer_spec(wts['g_ffn'], l), _const_spec(wts['g_final'].shape),
                    _layer_spec(w1, i), _layer_spec(w3, i), _layer_spec(w2, i)] + cast_specs,
        out_specs=[x_spec] + cast_out_specs,
        out_shape=[jax.ShapeDtypeStruct((m, d), F32)] + cast_shapes,
        compiler_params=pltpu.CompilerParams(
            dimension_semantics=("arbitrary",), vmem_limit_bytes=VMEM_LIMIT_DENSE),
        name=f"ffn_dense_{l}",
    )(x2d, mod, mod, mod, wts['g_ffn'], wts['g_final'], w1, w3, w2, *cast_in)
    return out[0], [o.reshape(a.shape[1:]) for o, (a, _) in zip(out[1:], cast)]


def _moe_route_kernel(x_ref, shift_ref, scale_ref, gffn_ref, wr_ref, br_ref, h_ref, meta_ref, *,
                      rows_per_seq, tm):
    shift, scale = _modulation([shift_ref, scale_ref], rows_per_seq, tm)
    h32 = _rms(x_ref[...], gffn_ref[...]) * (1.0 + scale) + shift
    _to_token_tiles(h_ref, h32, tm)
    meta_ref[...] = _router_meta(h32, wr_ref, br_ref)


def _moe_route(x2d, mod, l, wts, rows_per_seq):
    m, d = x2d.shape
    tm = min(TM_FFN, m)
    assert m % tm == 0 and (rows_per_seq == 1 or rows_per_seq % tm == 0)
    i = l // 2
    return pl.pallas_call(
        functools.partial(_moe_route_kernel, rows_per_seq=rows_per_seq, tm=tm),
        grid=(m // tm,),
        in_specs=[pl.BlockSpec((tm, d), lambda t: (t, 0))]
                 + _mod_specs(mod, l, (3, 4), rows_per_seq, tm)
                 + [_layer_spec(wts['g_ffn'], l), _layer_spec(wts['w_router'], i),
                    _layer_spec(wts['b_router'], i)],
        out_specs=[pl.BlockSpec((tm * SUBLANES, LANES), lambda t: (t, 0)),
                   pl.BlockSpec((tm, LANES), lambda t: (t, 0))],
        out_shape=[jax.ShapeDtypeStruct((m * SUBLANES, LANES), F32),
                   jax.ShapeDtypeStruct((m, LANES), F32)],
        compiler_params=pltpu.CompilerParams(
            dimension_semantics=("arbitrary",), vmem_limit_bytes=VMEM_LIMIT),
        name=f"moe_route_{l}_{m}",
    )(x2d, mod, mod, wts['g_ffn'], wts['w_router'], wts['b_router'])


def _super_block_pieces(m_p, m_s, s_tok):
    pieces = []
    for k in range(N_SUPER):
        lo, hi = k * s_tok, (k + 1) * s_tok
        ps = []
        if lo < m_p:
            ps.append((0, lo, 0, min(hi, m_p) - lo))
        if hi > m_p:
            s0 = max(lo, m_p)
            ps.append((1, s0 - m_p, s0 - lo, hi - s0))
        pieces.append(ps)
    return pieces


def _moe_expert_kernel(cnt_ref, off_ref, idx_hbm, g_hbm, hp_hbm, hs_hbm, w1_ref, w3_ref, w2_ref,
                       yp_hbm, ys_hbm, h_scr, y_scr, xbuf, obuf, idx_s, g_s, sem,
                       *, pieces, s_tok, s_pad, l_pad, tm):
    sb = pl.program_id(0)
    e = pl.program_id(1)
    seg = sb * N_EXPERTS + e
    rows = s_tok * SUBLANES

    def piece_copies(k, to_vmem):
        copies = []
        for j, (grp, src_tok, dst_tok, n) in enumerate(pieces[k]):
            hbm = ((hp_hbm, hs_hbm) if to_vmem else (yp_hbm, ys_hbm))[grp]
            hbm = hbm.at[pl.ds(src_tok * SUBLANES, n * SUBLANES)]
            if to_vmem:
                copies.append(pltpu.make_async_copy(
                    hbm, h_scr.at[pl.ds(dst_tok * SUBLANES, n * SUBLANES)], sem.at[j]))
            else:
                copies.append(pltpu.make_async_copy(
                    y_scr.at[pl.ds(dst_tok * SUBLANES, n * SUBLANES)], hbm, sem.at[2 + j]))
        return copies

    def gate_copy(s):
        slot = lax.rem(s, 2)
        src = pl.ds(pl.multiple_of(s * s_pad, IDX_ALIGN), s_pad)
        dst = pl.ds(pl.multiple_of(slot * s_pad, IDX_ALIGN), s_pad)
        return pltpu.make_async_copy(g_hbm.at[src], g_s.at[dst], sem.at[6 + slot])

    def list_copy():
        src = pl.ds(pl.multiple_of(sb * l_pad, IDX_ALIGN), l_pad)
        return pltpu.make_async_copy(idx_hbm.at[src], idx_s, sem.at[4])

    @pl.when(seg == 0)
    def _first_gates():
        gate_copy(seg).start()
        obuf[...] = jnp.zeros(obuf.shape, F32)

    @pl.when(seg + 1 < N_SUPER * N_EXPERTS)
    def _next_gates():
        gate_copy(seg + 1).start()

    for k in range(N_SUPER):
        @pl.when(jnp.logical_and(sb == k, e == 0))
        def _load_super_block(k=k):
            list_copy().start()
            for c in piece_copies(k, True):
                c.start()
            zrows = 256
            assert rows % zrows == 0
            def zero(i, carry):
                r0 = pl.multiple_of(i * zrows, zrows)
                y_scr[pl.ds(r0, zrows), :] = jnp.zeros((zrows, LANES), F32)
                return carry
            lax.fori_loop(0, rows // zrows, zero, 0)
            for c in piece_copies(k, True):
                c.wait()
            list_copy().wait()

    gate_copy(seg).wait()
    gts = lax.rem(seg, 2) * s_pad
    lst = off_ref[seg]

    def gather(base):
        for r in range(tm):
            t8 = pl.multiple_of(idx_s[lst + base + r] * SUBLANES, SUBLANES)
            xbuf[r * SUBLANES:(r + 1) * SUBLANES, :] = h_scr[pl.ds(t8, SUBLANES), :]

    def scatter_add(base, limit):
        for r0 in range(0, tm, SUBLANES):
            upd = []
            for r in range(r0, r0 + SUBLANES):
                tok = idx_s[lst + base + r]
                t8 = pl.multiple_of(tok * SUBLANES, SUBLANES)
                g = jnp.where(base + r < limit, g_s[gts + tok], 0.0)
                o = obuf[r * SUBLANES:(r + 1) * SUBLANES, :]
                upd.append((t8, y_scr[pl.ds(t8, SUBLANES), :] + g * o))
            for t8, v in upd:
                y_scr[pl.ds(t8, SUBLANES), :] = v

    n_sel = cnt_ref[seg]
    n_tiles = (n_sel + tm - 1) // tm

    def tile(i, carry):
        x = _from_token_tiles(xbuf, tm).astype(BF16)
        gather((i + 1) * tm)
        scatter_add(jnp.maximum(i - 1, 0) * tm, jnp.where(i > 0, n_sel, 0))
        act = (_silu(_dot(x, w1_ref[...])) * _dot(x, w3_ref[...])).astype(BF16)
        _to_token_tiles(obuf, _dot(act, w2_ref[...]), tm)
        return carry

    gather(0)
    lax.fori_loop(0, n_tiles, tile, 0)

    @pl.when(n_tiles > 0)
    def _last_scatter():
        scatter_add((n_tiles - 1) * tm, n_sel)

    for k in range(N_SUPER):
        @pl.when(jnp.logical_and(sb == k, e == N_EXPERTS - 1))
        def _store_super_block(k=k):
            for c in piece_copies(k, False):
                c.start()
            for c in piece_copies(k, False):
                c.wait()


def _moe_experts(counts, offs, idx, gates, h_p, h_s, l, w_bf, s_tok, s_pad, l_pad):
    tm = TM_EXPERT
    w1, w3, w2 = w_bf
    w_spec = lambda a: pl.BlockSpec((None,) + a.shape[1:], lambda sb, e, cnt, off: (e, 0, 0))
    any_spec = pl.BlockSpec(memory_space=pl.ANY)
    rows = s_tok * SUBLANES
    pieces = _super_block_pieces(h_p.shape[0] // SUBLANES, h_s.shape[0] // SUBLANES, s_tok)
    return pl.pallas_call(
        functools.partial(_moe_expert_kernel, pieces=pieces, s_tok=s_tok, s_pad=s_pad,
                          l_pad=l_pad, tm=tm),
        grid_spec=pltpu.PrefetchScalarGridSpec(
            num_scalar_prefetch=2,
            grid=(N_SUPER, N_EXPERTS),
            in_specs=[any_spec] * 4 + [w_spec(w1), w_spec(w3), w_spec(w2)],
            out_specs=[any_spec, any_spec],
            scratch_shapes=[pltpu.VMEM((rows, LANES), F32), pltpu.VMEM((rows, LANES), F32),
                            pltpu.VMEM((tm * SUBLANES, LANES), F32),
                            pltpu.VMEM((tm * SUBLANES, LANES), F32),
                            pltpu.SMEM((l_pad,), jnp.int32), pltpu.SMEM((2 * s_pad,), F32),
                            pltpu.SemaphoreType.DMA((8,))]),
        out_shape=[jax.ShapeDtypeStruct(h_p.shape, F32), jax.ShapeDtypeStruct(h_s.shape, F32)],
        compiler_params=pltpu.CompilerParams(
            dimension_semantics=("arbitrary", "arbitrary"), vmem_limit_bytes=VMEM_LIMIT_EXPERT),
        name=f"moe_experts_{l}",
    )(counts, offs, idx, gates, h_p, h_s, w1, w3, w2)


def _moe_residual_kernel(x_ref, gate_ref, gfin_ref, y_ref, o_ref, *, rows_per_seq, tm, final_norm):
    (gate,) = _modulation([gate_ref], rows_per_seq, tm)
    y = x_ref[...] + gate * _from_token_tiles(y_ref, tm)
    o_ref[...] = _rms(y, gfin_ref[...]) if final_norm else y


def _moe_residual(x2d, mod, y, l, wts, rows_per_seq, final_norm):
    m, d = x2d.shape
    tm = min(TM_FFN, m)
    assert m % tm == 0
    x_spec = pl.BlockSpec((tm, d), lambda t: (t, 0))
    return pl.pallas_call(
        functools.partial(_moe_residual_kernel, rows_per_seq=rows_per_seq, tm=tm,
                          final_norm=final_norm),
        grid=(m // tm,),
        in_specs=[x_spec] + _mod_specs(mod, l, (5,), rows_per_seq, tm)
                 + [_const_spec(wts['g_final'].shape),
                    pl.BlockSpec((tm * SUBLANES, LANES), lambda t: (t, 0))],
        out_specs=x_spec,
        out_shape=jax.ShapeDtypeStruct((m, d), F32),
        compiler_params=pltpu.CompilerParams(
            dimension_semantics=("arbitrary",), vmem_limit_bytes=VMEM_LIMIT),
        name=f"moe_residual_{l}_{m}",
    )(x2d, mod, wts['g_final'], y)


def _ffn_moe(routed_p, xs, mod_s, l, wts, w_bf):
    h_p, meta_p = routed_p
    m_p, m_s = meta_p.shape[0], xs.shape[0]
    n_tok = m_p + m_s
    s_tok = n_tok // N_SUPER
    assert s_tok * N_SUPER == n_tok and s_tok % SUBLANES == 0
    s_pad = -(-s_tok // IDX_ALIGN) * IDX_ALIGN

    h_s, meta_s = _moe_route(xs, mod_s, l, wts, 1)

    n_meta = 2 * N_EXPERTS + 2
    meta = jnp.concatenate([meta_p[:, :n_meta], meta_s[:, :n_meta]], axis=0)
    per_seg = lambda a: a.reshape(N_SUPER, s_tok, N_EXPERTS).transpose(0, 2, 1)
    gate = per_seg(meta[:, :N_EXPERTS])
    counts = jnp.sum(per_seg(meta[:, N_EXPERTS:2 * N_EXPERTS]).astype(jnp.int32), axis=2)
    offs = jnp.cumsum(counts, axis=1) - counts
    chosen = meta[:, 2 * N_EXPERTS:].astype(jnp.int32).reshape(N_SUPER, s_tok, 2)
    tok = lax.broadcasted_iota(jnp.int32, chosen.shape, 1)
    tok_bits = (s_tok - 1).bit_length()
    keys = (chosen * (1 << tok_bits) + tok).reshape(N_SUPER, 2 * s_tok)
    idx = lax.sort(keys, dimension=1) & ((1 << tok_bits) - 1)
    l_pad = -(-(2 * s_tok + 2 * TM_EXPERT) // IDX_ALIGN) * IDX_ALIGN
    idx = jnp.pad(idx, ((0, 0), (0, l_pad - 2 * s_tok))).reshape(-1)
    gate = jnp.pad(gate, ((0, 0), (0, 0), (0, s_pad - s_tok))).reshape(-1)

    return _moe_experts(counts.reshape(-1), offs.reshape(-1), idx, gate, h_p, h_s, l, w_bf,
                        s_tok, s_pad, l_pad)


def kernel(x_prompt, x_sample, state_conv_a, state_conv_c, state_pool_d, c_prompt, c_sample,
           w_ada, b_ada, g_mix, w_in, conv_a_w, ln_v_g, ln_v_b, w_s, b_s, conv_c_w, conv_c_b,
           ln_c_g, ln_c_b, pool_w, pool_scale, g_out, w_out, g_ffn, w1_dense, w3_dense, w2_dense,
           w_router, b_router, w1_moe, w3_moe, w2_moe, g_final):
    p = dict(g_mix=g_mix, w_in=w_in, conv_a_w=conv_a_w, ln_v_g=ln_v_g, ln_v_b=ln_v_b, w_s=w_s,
             b_s=b_s, conv_c_w=conv_c_w, conv_c_b=conv_c_b, ln_c_g=ln_c_g, ln_c_b=ln_c_b,
             pool_w=pool_w, pool_scale=pool_scale, g_out=g_out, w_out=w_out, g_ffn=g_ffn,
             w1_dense=w1_dense, w3_dense=w3_dense, w2_dense=w2_dense, w_router=w_router,
             b_router=b_router, w1_moe=w1_moe, w3_moe=w3_moe, w2_moe=w2_moe, g_final=g_final)
    n_p, seq, d = x_prompt.shape
    n_s, dec_seq, _ = x_sample.shape
    assert dec_seq == 1 and d == D_MODEL

    mod_p, mod_s = _ada(c_prompt, c_sample, w_ada, b_ada)
    kmajor = lambda s: jnp.transpose(s, (0, 2, 1, 3))
    sa_t, sc_t, sd_t = kmajor(state_conv_a), kmajor(state_conv_c), kmajor(state_pool_d)
    mixw = _mix_params(p)
    pad_e = LANES - N_EXPERTS
    wts = dict(
        g_ffn=g_ffn.reshape(DEPTH, 1, d), g_final=g_final.reshape(1, d),
        w1_dense=w1_dense.astype(BF16), w3_dense=w3_dense.astype(BF16),
        w2_dense=w2_dense.astype(BF16),
        w_router=jnp.pad(w_router, ((0, 0), (0, 0), (0, pad_e))),
        b_router=jnp.pad(b_router, ((0, 0), (0, pad_e))).reshape(-1, 1, LANES),
        w1_moe=w1_moe, w3_moe=w3_moe, w2_moe=w2_moe)

    xp = x_prompt
    xs = x_sample.reshape(n_s, d)
    states_p = [[], [], [], []]
    states_s = [[], [], [], []]
    pending = None
    moe_bf = None
    for l in range(DEPTH):
        last = l == DEPTH - 1
        moe = l % 2 == 1
        xp, *st_p = _mix_prompt(xp, mod_p, l, mixw, 0, pending, wts if moe else None)
        pending = None
        xs, *st_s = _mix_sample(xs, mod_s, sa_t, sc_t, sd_t, l, mixw, PAST_LEN)
        for acc, s in zip(states_p, st_p[:4]):
            acc.append(s)
        for acc, s in zip(states_s, st_s):
            acc.append(s)
        if moe:
            if moe_bf is None:
                moe_bf = [wts[k][l // 2].astype(BF16) for k in ('w1_moe', 'w3_moe', 'w2_moe')]
            y_p, y_s = _ffn_moe(st_p[4:], xs, mod_s, l, wts, moe_bf)
            moe_bf = None
            xs = _moe_residual(xs, mod_s, y_s, l, wts, 1, last)
            if last:
                xp = _moe_residual(xp.reshape(n_p * seq, d), mod_p, y_p, l, wts, seq,
                                   True).reshape(n_p, seq, d)
            else:
                pending = (y_p, l)
        else:
            cast = ([(wts[k], (l + 1) // 2) for k in ('w1_moe', 'w3_moe', 'w2_moe')]
                    if l + 1 < DEPTH else [])
            xp2d, moe_bf = _ffn_dense(xp.reshape(n_p * seq, d), mod_p, l, wts, seq, last, cast)
            xp = xp2d.reshape(n_p, seq, d)
            xs, _ = _ffn_dense(xs, mod_s, l, wts, 1, last)
            moe_bf = moe_bf or None

    a_p, c_p, d_p, v_p = (jnp.stack(s) for s in states_p)
    a_s, c_s, d_s = (kmajor(jnp.stack(s)) for s in states_s[:3])
    v_s = jnp.stack(states_s[3]).reshape(DEPTH, n_s, 1, W_GROUP)
    return (xp, xs.reshape(n_s, 1, d), a_p, c_p, d_p, v_p, a_s, c_s, d_s, v_s)
```

```python
import functools

import jax
import jax.numpy as jnp
from jax import lax
from jax.experimental import pallas as pl
from jax.experimental.pallas import tpu as pltpu

D_MODEL = 1024
DEPTH = 4
W_GROUP = 256
CONV_A = 3
CHUNK = 128
N_HEADS_B = 4
HEAD_B = 64
CONV_C = 31
POOL_WINDOWS = (2, 4, 8, 16)
POOL_MAX = 16
GROUP_D = 64
IN_COLS = 2048
N_EXPERTS = 8
PAST_LEN = 16384
EPS = 1e-6

LANES = 128
LANE_TILES = W_GROUP // LANES
HIST_A = 8
HIST_C = 32
HIST_D = 32
TL_MIX = 512
ROW_CHUNK = 64
TM_FFN = 512
SUBLANES = 8
N_SUPER = 4
TM_EXPERT = 256
IDX_ALIGN = 1024
VMEM_LIMIT = 56 * 1024 * 1024
VMEM_LIMIT_DENSE = 60 * 1024 * 1024
VMEM_LIMIT_EXPERT = 62 * 1024 * 1024

F32 = jnp.float32
BF16 = jnp.bfloat16


def _dot(a, b):
    return jnp.dot(a, b, preferred_element_type=F32)


def _rms(x, g):
    return x * lax.rsqrt(jnp.mean(x * x, axis=-1, keepdims=True) + EPS) * g


def _ln(x, g, b):
    xc = x - jnp.mean(x, axis=-1, keepdims=True)
    var = jnp.mean(xc * xc, axis=-1, keepdims=True)
    return xc * lax.rsqrt(var + EPS) * g + b


def _silu(x):
    return x * jax.nn.sigmoid(x)


def _lane_group_select(vals, shape):
    lane = lax.broadcasted_iota(jnp.int32, shape, 1)
    out = vals[3]
    for g in (2, 1, 0):
        out = jnp.where(lane < (g + 1) * GROUP_D, vals[g], out)
    return out


def _ada_kernel(cp_ref, cs_ref, w_ref, b_ref, op_ref, os_ref):
    w = w_ref[...].astype(BF16)
    b = b_ref[...]
    op_ref[...] = _dot(_silu(cp_ref[...]).astype(BF16), w) + b
    os_ref[...] = _dot(_silu(cs_ref[...]).astype(BF16), w) + b


def _ada(c_prompt, c_sample, w_ada, b_ada):
    n_p, n_s = c_prompt.shape[0], c_sample.shape[0]
    d = D_MODEL
    return pl.pallas_call(
        _ada_kernel,
        grid=(DEPTH, 6),
        in_specs=[
            pl.BlockSpec((n_p, d), lambda l, k: (0, 0)),
            pl.BlockSpec((n_s, d), lambda l, k: (0, 0)),
            pl.BlockSpec((None, d, d), lambda l, k: (l, 0, k)),
            pl.BlockSpec((None, 1, d), lambda l, k: (l, 0, k)),
        ],
        out_specs=[
            pl.BlockSpec((None, None, n_p, d), lambda l, k: (l, k, 0, 0)),
            pl.BlockSpec((None, None, n_s, d), lambda l, k: (l, k, 0, 0)),
        ],
        out_shape=[
            jax.ShapeDtypeStruct((DEPTH, 6, n_p, d), F32),
            jax.ShapeDtypeStruct((DEPTH, 6, n_s, d), F32),
        ],
        compiler_params=pltpu.CompilerParams(
            dimension_semantics=("arbitrary", "arbitrary"), vmem_limit_bytes=VMEM_LIMIT),
        name="ada",
    )(c_prompt, c_sample, w_ada, b_ada.reshape(DEPTH, 1, 6 * d))


def _masked_ws(ws_ref):
    r = lax.broadcasted_iota(jnp.int32, (CHUNK, CHUNK), 0)
    c = lax.broadcasted_iota(jnp.int32, (CHUNK, CHUNK), 1)
    return [jnp.where(c <= r, ws_ref[h], 0.0).astype(BF16) for h in range(N_HEADS_B)]


def _ext_rows(ref, j, row0, n):
    return ref[pl.ds(row0 * LANE_TILES + j, n, stride=LANE_TILES), :]


def _ext_store(ref, row0, val):
    for j in range(LANE_TILES):
        ref[pl.ds(row0 * LANE_TILES + j, val.shape[0], stride=LANE_TILES), :] = (
            val[:, j * LANES:(j + 1) * LANES])


def _ext_load(ref, row0, n):
    return jnp.concatenate([_ext_rows(ref, j, row0, n) for j in range(LANE_TILES)], axis=1)


def _merge_and_project(x, gate, outs, gout_ref, wout_bf):
    merged = jnp.concatenate(
        [_rms(o, gout_ref[i:i + 1, :]) for i, o in enumerate(outs)], axis=1).astype(BF16)
    return x + gate * _dot(merged, wout_bf)


def _to_token_tiles(ref, val, tm):
    for k in range(SUBLANES):
        ref[pl.ds(k, tm, stride=SUBLANES), :] = val[:, k * LANES:(k + 1) * LANES]


def _from_token_tiles(ref, tm):
    return jnp.concatenate(
        [ref[pl.ds(k, tm, stride=SUBLANES), :] for k in range(SUBLANES)], axis=1)


def _router_meta(h32, wr_ref, br_ref):
    wr = wr_ref[...]
    h_hi = h32.astype(BF16)
    h_lo = (h32 - h_hi.astype(F32)).astype(BF16)
    w_hi = wr.astype(BF16)
    w_lo = (wr - w_hi.astype(F32)).astype(BF16)
    logits = _dot(h_hi, w_hi) + (_dot(h_lo, w_hi) + _dot(h_hi, w_lo)) + br_ref[...]
    lane = lax.broadcasted_iota(jnp.int32, logits.shape, 1)
    lane_f = lane.astype(F32)
    neg = jnp.float32(-jnp.inf)
    logits = jnp.where(lane < N_EXPERTS, logits, neg)
    m1 = jnp.max(logits, axis=-1, keepdims=True)
    i1 = jnp.min(jnp.where(logits == m1, lane_f, float(LANES)), axis=-1, keepdims=True)
    rest = jnp.where(lane_f == i1, neg, logits)
    m2 = jnp.max(rest, axis=-1, keepdims=True)
    i2 = jnp.min(jnp.where(rest == m2, lane_f, float(LANES)), axis=-1, keepdims=True)
    e = jnp.exp(m2 - m1)
    g1 = 1.0 / (1.0 + e)
    g2 = e / (1.0 + e)
    comb = jnp.where(lane_f == i1, g1, 0.0) + jnp.where(lane_f == i2, g2, 0.0)
    flags = (jnp.where(lane_f == i1 + N_EXPERTS, 1.0, 0.0)
             + jnp.where(lane_f == i2 + N_EXPERTS, 1.0, 0.0))
    chosen = (jnp.where(lane == 2 * N_EXPERTS, i1, 0.0)
              + jnp.where(lane == 2 * N_EXPERTS + 1, i2, 0.0))
    return comb + flags + chosen


def _mix_prompt_kernel(*refs, tl, start_pos, pre, post, n_cast):
    refs = list(refs)
    take = lambda n: [refs.pop(0) for _ in range(n)]
    (x_ref,) = take(1)
    y_ref, gprev_ref = take(2) if pre else (None, None)
    shift_ref, scale_ref, gate_ref = take(3)
    shift2_ref, scale2_ref = take(2) if post else (None, None)
    (gmix_ref, win_ref, wout_ref, convaw_ref, lnvg_ref, lnvb_ref, ws_ref, bsfull_ref, convcw_ref,
     convcb_ref, lncg_ref, lncb_ref, poolw_ref, poolscale_ref, gout_ref) = take(15)
    gffn_ref, wr_ref, br_ref = take(3) if post else (None, None, None)
    cast_src = take(n_cast)
    xo_ref, na_ref, nc_ref, nd_ref, nv_ref = take(5)
    h_ref, meta_ref = take(2) if post else (None, None)
    cast_dst = take(n_cast)
    win_bf, wout_bf, exta, extc, extd, ext2, ext4, ext8 = take(8)
    assert not refs
    for src, dst in zip(cast_src, cast_dst):
        dst[...] = src[...].astype(BF16)
    b = pl.program_id(0)
    t = pl.program_id(1)
    last_t = pl.num_programs(1) - 1

    @pl.when(jnp.logical_and(b == 0, t == 0))
    def _cast_weights():
        rows = 128
        def body(i, carry):
            r0 = pl.multiple_of(i * rows, rows)
            win_bf[pl.ds(r0, rows), :] = win_ref[pl.ds(r0, rows), :].astype(BF16)
            wout_bf[pl.ds(r0, rows), :] = wout_ref[pl.ds(r0, rows), :].astype(BF16)
            return carry
        lax.fori_loop(0, D_MODEL // rows, body, 0)

    @pl.when(t == 0)
    def _zero_history():
        for ref, hist in ((exta, HIST_A), (extc, HIST_C), (extd, HIST_D)):
            ref[0:hist * LANE_TILES, :] = jnp.zeros((hist * LANE_TILES, LANES), F32)

    x = x_ref[...]
    if pre:
        x = x + gprev_ref[pl.ds(b, 1), :] * _from_token_tiles(y_ref, tl)
    shift = shift_ref[pl.ds(b, 1), :]
    scale = scale_ref[pl.ds(b, 1), :]
    gate = gate_ref[pl.ds(b, 1), :]
    h = _rms(x, gmix_ref[...] * (1.0 + scale)) + shift
    proj = _dot(h.astype(BF16), win_bf[...])
    a_b, a_c, a_h, b_u, b_v, c_a, c_g, d_p = [
        proj[:, i * W_GROUP:(i + 1) * W_GROUP] for i in range(8)]

    def dwconv(ext_ref, w_ref, hist, width):
        off = hist - (width - 1)
        halves = []
        for j in range(LANE_TILES):
            chunks = []
            for c0 in range(0, tl, ROW_CHUNK):
                acc = None
                for k in range(width):
                    term = (_ext_rows(ext_ref, j, off + c0 + k, ROW_CHUNK)
                            * w_ref[k:k + 1, j * LANES:(j + 1) * LANES])
                    acc = term if acc is None else acc + term
                chunks.append(acc)
            halves.append(jnp.concatenate(chunks, axis=0))
        return jnp.concatenate(halves, axis=1)

    _ext_store(exta, HIST_A, a_c * a_h)
    out_a = a_b * dwconv(exta, convaw_ref, HIST_A, CONV_A)

    v_n = _ln(b_v, lnvg_ref[...], lnvb_ref[...])
    v_bf = v_n.astype(BF16)
    wm = _masked_ws(ws_ref)
    lane = lax.broadcasted_iota(jnp.int32, (CHUNK, W_GROUP), 1)
    mixed_chunks = []
    for j in range(tl // CHUNK):
        vc = v_bf[j * CHUNK:(j + 1) * CHUNK, :]
        mixed = _dot(wm[3], vc)
        for hd in (2, 1, 0):
            mixed = jnp.where(lane < (hd + 1) * HEAD_B, _dot(wm[hd], vc), mixed)
        mixed_chunks.append(mixed + bsfull_ref[...])
    out_b = b_u * jnp.concatenate(mixed_chunks, axis=0)

    _ext_store(extc, HIST_C, c_a * jax.nn.sigmoid(c_g))
    y_c = dwconv(extc, convcw_ref, HIST_C, CONV_C) + convcb_ref[...]
    out_c = _silu(_ln(y_c, lncg_ref[...], lncb_ref[...]))

    n = HIST_D + tl
    _ext_store(extd, HIST_D, d_p)
    bufs = (extd, ext2, ext4, ext8)
    pos1 = start_pos + 1 + t * tl + lax.broadcasted_iota(jnp.int32, (tl, LANES), 0)
    low_group = lax.broadcasted_iota(jnp.int32, (tl, LANES), 1) < GROUP_D
    means = []
    for j in range(LANE_TILES):
        levels = 2 * (j + 1)
        for lv in range(levels - 1):
            first = 8 * (lv + 1)
            bufs[lv + 1][pl.ds(first * LANE_TILES + j, n - first, stride=LANE_TILES), :] = (
                _ext_rows(bufs[lv], j, first, n - first)
                + _ext_rows(bufs[lv], j, first - (1 << lv), n - first))
        prev = bufs[levels - 1]
        s_lo = _ext_rows(prev, j, HIST_D, tl)
        s_hi = s_lo + _ext_rows(prev, j, HIST_D - (1 << (levels - 1)), tl)
        w_lo, w_hi = POOL_WINDOWS[2 * j], POOL_WINDOWS[2 * j + 1]
        cnt = jnp.where(low_group, jnp.minimum(pos1, w_lo), jnp.minimum(pos1, w_hi)).astype(F32)
        means.append(jnp.where(low_group, s_lo, s_hi) / cnt)
    pooled = jnp.concatenate(means, axis=1) - d_p
    out_d = _dot(pooled.astype(BF16), poolw_ref[...].astype(BF16)) * poolscale_ref[...]

    x_new = _merge_and_project(x, gate, [out_a, out_b, out_c, out_d], gout_ref, wout_bf[...])
    xo_ref[...] = x_new
    if post:
        h2 = (_rms(x_new, gffn_ref[...]) * (1.0 + scale2_ref[pl.ds(b, 1), :])
              + shift2_ref[pl.ds(b, 1), :])
        _to_token_tiles(h_ref, h2, tl)
        meta_ref[...] = _router_meta(h2, wr_ref, br_ref)

    @pl.when(t == last_t)
    def _emit_state():
        na_ref[...] = _ext_load(exta, HIST_A + tl - (CONV_A - 1), CONV_A - 1)
        nc_ref[...] = _ext_load(extc, HIST_C + tl - (CONV_C - 1), CONV_C - 1)
        nd_ref[...] = _ext_load(extd, HIST_D + tl - (POOL_MAX - 1), POOL_MAX - 1)
        nv_ref[...] = v_n[tl - CHUNK:tl, :]

    for ref, hist in ((exta, HIST_A), (extc, HIST_C), (extd, HIST_D)):
        ref[0:hist * LANE_TILES, :] = ref[tl * LANE_TILES:(tl + hist) * LANE_TILES, :]


def _mix_params(p):
    r3 = lambda a: a.reshape(DEPTH, 1, -1)
    eye = jnp.eye(4, dtype=F32)
    pool_bd = (eye[None, :, None, :, None] * p['pool_w'][:, :, :, None, :]).reshape(
        DEPTH, W_GROUP, W_GROUP)
    return dict(
        gmix=r3(p['g_mix']), w_in=p['w_in'], w_out=p['w_out'], conva=p['conv_a_w'],
        lnvg=r3(p['ln_v_g']), lnvb=r3(p['ln_v_b']), ws=p['w_s'],
        bsfull=jnp.repeat(jnp.swapaxes(p['b_s'], 1, 2), HEAD_B, axis=2),
        ws0=r3(jnp.repeat(p['w_s'][:, :, 0, 0], HEAD_B, axis=1)),
        bs0=r3(jnp.repeat(p['b_s'][:, :, 0], HEAD_B, axis=1)),
        convc=p['conv_c_w'], convcb=r3(p['conv_c_b']), lncg=r3(p['ln_c_g']), lncb=r3(p['ln_c_b']),
        poolw=pool_bd, poolscale=r3(p['pool_scale']), gout=p['g_out'])


def _layer_spec(a, l):
    nd = a.ndim - 1
    return pl.BlockSpec((None,) + a.shape[1:], lambda *_: (l,) + (0,) * nd,
                        pipeline_mode=pl.Buffered(1))


def _const_spec(shape):
    nd = len(shape)
    return pl.BlockSpec(shape, lambda *_: (0,) * nd, pipeline_mode=pl.Buffered(1))


BF16_ROWS = 16


def _cast_job(cast, steps, step_of):
    args, in_specs, out_specs, out_shapes = [], [], [], []
    for a, j in cast:
        rows, cols = 1, a.shape[-1]
        for n in a.shape[1:-1]:
            rows *= n
        blocks = steps
        while rows % (blocks * BF16_ROWS):
            assert blocks % 2 == 0
            blocks //= 2
        rep = steps // blocks
        args.append(a.reshape(a.shape[0], rows, cols))
        in_specs.append(pl.BlockSpec((None, rows // blocks, cols),
                                     lambda *g, j=j, rep=rep: (j, step_of(*g) // rep, 0)))
        out_specs.append(pl.BlockSpec((rows // blocks, cols),
                                      lambda *g, rep=rep: (step_of(*g) // rep, 0)))
        out_shapes.append(jax.ShapeDtypeStruct((rows, cols), BF16))
    return args, in_specs, out_specs, out_shapes


def _mix_prompt(x, mod_p, l, w, start_pos, prev_moe=None, wts=None, cast=()):
    n_b, seq, d = x.shape
    tl = TL_MIX
    nt = seq // tl
    assert seq % tl == 0 and tl % CHUNK == 0 and seq >= CHUNK
    pre, post = prev_moe is not None, wts is not None
    mod_spec = lambda lay, k: pl.BlockSpec((None, None, n_b, d), lambda b, t: (lay, k, 0, 0))
    tiles_spec = lambda rows: pl.BlockSpec((tl * rows, LANES), lambda b, t: (b * nt + t, 0))
    weights = [w[k] for k in ('gmix', 'w_in', 'w_out', 'conva', 'lnvg', 'lnvb', 'ws', 'bsfull',
                              'convc', 'convcb', 'lncg', 'lncb', 'poolw', 'poolscale', 'gout')]
    state_spec = lambda r: pl.BlockSpec((None, r, W_GROUP), lambda b, t: (b, 0, 0))
    ext = lambda hist: pltpu.VMEM(((hist + tl) * LANE_TILES, LANES), F32)

    args, in_specs = [x], [pl.BlockSpec((None, tl, d), lambda b, t: (b, t, 0))]
    if pre:
        y_prev, l_prev = prev_moe
        args += [y_prev, mod_p]
        in_specs += [tiles_spec(SUBLANES), mod_spec(l_prev, 5)]
    args += [mod_p] * 3
    in_specs += [mod_spec(l, 0), mod_spec(l, 1), mod_spec(l, 2)]
    if post:
        args += [mod_p] * 2
        in_specs += [mod_spec(l, 3), mod_spec(l, 4)]
    args += weights
    in_specs += [_layer_spec(a, l) for a in weights]
    out_specs = [pl.BlockSpec((None, tl, d), lambda b, t: (b, t, 0)),
                 state_spec(CONV_A - 1), state_spec(CONV_C - 1), state_spec(POOL_MAX - 1),
                 state_spec(CHUNK)]
    out_shape = [jax.ShapeDtypeStruct(x.shape, F32),
                 jax.ShapeDtypeStruct((n_b, CONV_A - 1, W_GROUP), F32),
                 jax.ShapeDtypeStruct((n_b, CONV_C - 1, W_GROUP), F32),
                 jax.ShapeDtypeStruct((n_b, POOL_MAX - 1, W_GROUP), F32),
                 jax.ShapeDtypeStruct((n_b, CHUNK, W_GROUP), F32)]
    if post:
        i = l // 2
        args += [wts['g_ffn'], wts['w_router'], wts['b_router']]
        in_specs += [_layer_spec(wts['g_ffn'], l), _layer_spec(wts['w_router'], i),
                     _layer_spec(wts['b_router'], i)]
        out_specs += [tiles_spec(SUBLANES), tiles_spec(1)]
        out_shape += [jax.ShapeDtypeStruct((n_b * seq * SUBLANES, LANES), F32),
                      jax.ShapeDtypeStruct((n_b * seq, LANES), F32)]
    c_args, c_in, c_out, c_shapes = _cast_job(cast, n_b * nt, lambda b, t: b * nt + t)
    out = pl.pallas_call(
        functools.partial(_mix_prompt_kernel, tl=tl, start_pos=start_pos, pre=pre, post=post,
                          n_cast=len(cast)),
        grid=(n_b, nt),
        in_specs=in_specs + c_in,
        out_specs=out_specs + c_out,
        out_shape=out_shape + c_shapes,
        scratch_shapes=[pltpu.VMEM((d, IN_COLS), BF16), pltpu.VMEM((d, d), BF16),
                        ext(HIST_A), ext(HIST_C), ext(HIST_D), ext(HIST_D), ext(HIST_D),
                        ext(HIST_D)],
        compiler_params=pltpu.CompilerParams(
            dimension_semantics=("arbitrary", "arbitrary"), vmem_limit_bytes=VMEM_LIMIT),
        name=f"mix_prompt_{l}",
    )(*args, *c_args)
    n_out = len(out) - len(cast)
    return out[:n_out], [o.reshape(a.shape[1:]) for o, (a, _) in zip(out[n_out:], cast)]


def _mix_sample_kernel(x_ref, shift_ref, scale_ref, gate_ref, sa_ref, sc_ref, sd_ref, gmix_ref,
                       win_ref, wout_ref, convaw_ref, lnvg_ref, lnvb_ref, ws0_ref, bs0_ref,
                       convcw_ref, convcb_ref, lncg_ref, lncb_ref, poolw_ref, poolscale_ref,
                       gout_ref, xo_ref, na_ref, nc_ref, nd_ref, nv_ref, *, start_pos):
    x = x_ref[...]
    h = _rms(x, gmix_ref[...]) * (1.0 + scale_ref[...]) + shift_ref[...]
    proj = _dot(h.astype(BF16), win_ref[...].astype(BF16))
    a_b, a_c, a_h, b_u, b_v, c_a, c_g, d_p = [
        proj[:, i * W_GROUP:(i + 1) * W_GROUP] for i in range(8)]

    ch = a_c * a_h
    y_a = convaw_ref[CONV_A - 1:CONV_A, :] * ch
    for k in range(CONV_A - 1):
        y_a = y_a + convaw_ref[k:k + 1, :] * sa_ref[k]
    out_a = a_b * y_a
    for k in range(CONV_A - 2):
        na_ref[k] = sa_ref[k + 1]
    na_ref[CONV_A - 2] = ch

    v_n = _ln(b_v, lnvg_ref[...], lnvb_ref[...])
    out_b = b_u * (ws0_ref[...] * v_n + bs0_ref[...])
    nv_ref[...] = v_n

    glu = c_a * jax.nn.sigmoid(c_g)
    y_c = convcw_ref[CONV_C - 1:CONV_C, :] * glu + convcb_ref[...]
    for k in range(CONV_C - 1):
        y_c = y_c + convcw_ref[k:k + 1, :] * sc_ref[k]
    out_c = _silu(_ln(y_c, lncg_ref[...], lncb_ref[...]))
    for k in range(CONV_C - 2):
        nc_ref[k] = sc_ref[k + 1]
    nc_ref[CONV_C - 2] = glu

    hist = POOL_MAX - 1
    run = d_p
    taken = 0
    sums = []
    for w in POOL_WINDOWS:
        while taken < w - 1:
            run = run + sd_ref[hist - 1 - taken]
            taken += 1
        sums.append(run / float(min(start_pos + 1, w)))
    pooled = _lane_group_select(sums, d_p.shape) - d_p
    out_d = _dot(pooled.astype(BF16), poolw_ref[...].astype(BF16)) * poolscale_ref[...]
    for k in range(hist - 1):
        nd_ref[k] = sd_ref[k + 1]
    nd_ref[hist - 1] = d_p

    xo_ref[...] = _merge_and_project(x, gate_ref[...], [out_a, out_b, out_c, out_d], gout_ref,
                                     wout_ref[...].astype(BF16))


def _mix_sample(x, mod_s, sa_t, sc_t, sd_t, l, w, start_pos):
    n, d = x.shape
    assert start_pos + 1 >= POOL_MAX
    mod_spec = lambda k: pl.BlockSpec((None, None, n, d), lambda i: (l, k, 0, 0))
    st_spec = lambda r: pl.BlockSpec((None, r, n, W_GROUP), lambda i: (l, 0, 0, 0))
    weights = [w[k] for k in ('gmix', 'w_in', 'w_out', 'conva', 'lnvg', 'lnvb', 'ws0', 'bs0',
                              'convc', 'convcb', 'lncg', 'lncb', 'poolw', 'poolscale', 'gout')]
    full = lambda shape: pl.BlockSpec(shape, lambda i: (0,) * len(shape))
    return pl.pallas_call(
        functools.partial(_mix_sample_kernel, start_pos=start_pos),
        grid=(1,),
        in_specs=[full((n, d)), mod_spec(0), mod_spec(1), mod_spec(2),
                  st_spec(CONV_A - 1), st_spec(CONV_C - 1), st_spec(POOL_MAX - 1)]
                 + [_layer_spec(a, l) for a in weights],
        out_specs=[full((n, d)), full((CONV_A - 1, n, W_GROUP)), full((CONV_C - 1, n, W_GROUP)),
                   full((POOL_MAX - 1, n, W_GROUP)), full((n, W_GROUP))],
        out_shape=[jax.ShapeDtypeStruct((n, d), F32),
                   jax.ShapeDtypeStruct((CONV_A - 1, n, W_GROUP), F32),
                   jax.ShapeDtypeStruct((CONV_C - 1, n, W_GROUP), F32),
                   jax.ShapeDtypeStruct((POOL_MAX - 1, n, W_GROUP), F32),
                   jax.ShapeDtypeStruct((n, W_GROUP), F32)],
        compiler_params=pltpu.CompilerParams(
            dimension_semantics=("arbitrary",), vmem_limit_bytes=VMEM_LIMIT),
        name=f"mix_sample_{l}",
    )(x, mod_s, mod_s, mod_s, sa_t, sc_t, sd_t, *weights)


def _modulation(refs, rows_per_seq, tm):
    if rows_per_seq == 1:
        return [r[...] for r in refs]
    b = (pl.program_id(0) * tm) // rows_per_seq
    return [r[pl.ds(b, 1), :] for r in refs]


def _ffn_dense_kernel(x_ref, shift_ref, scale_ref, gate_ref, gffn_ref, gfin_ref, w1_ref, w3_ref,
                      w2_ref, *rest, rows_per_seq, tm, final_norm):
    n_cast = (len(rest) - 1) // 2
    o_ref = rest[n_cast]
    x = x_ref[...]
    shift, scale, gate = _modulation([shift_ref, scale_ref, gate_ref], rows_per_seq, tm)
    h = (_rms(x, gffn_ref[...]) * (1.0 + scale) + shift).astype(BF16)
    act = (_silu(_dot(h, w1_ref[...])) * _dot(h, w3_ref[...])).astype(BF16)
    y = x + gate * _dot(act, w2_ref[...])
    o_ref[...] = _rms(y, gfin_ref[...]) if final_norm else y
    for src, dst in zip(rest[:n_cast], rest[n_cast + 1:]):
        dst[...] = src[...].astype(BF16)


def _mod_specs(mod, l, ks, rows_per_seq, tm):
    d = mod.shape[-1]
    if rows_per_seq == 1:
        return [pl.BlockSpec((None, None, tm, d), lambda t, k=k: (l, k, t, 0)) for k in ks]
    n_seq = mod.shape[2]
    return [pl.BlockSpec((None, None, n_seq, d), lambda t, k=k: (l, k, 0, 0)) for k in ks]


def _ffn_dense(x2d, mod, l, wts, w_bf, rows_per_seq, final_norm, cast=()):
    m, d = x2d.shape
    tm = min(TM_FFN, m)
    assert m % tm == 0 and (rows_per_seq == 1 or rows_per_seq % tm == 0)
    steps = m // tm
    x_spec = pl.BlockSpec((tm, d), lambda t: (t, 0))
    c_args, c_in, c_out, c_shapes = _cast_job(cast, steps, lambda t: t)
    out = pl.pallas_call(
        functools.partial(_ffn_dense_kernel, rows_per_seq=rows_per_seq, tm=tm,
                          final_norm=final_norm),
        grid=(steps,),
        in_specs=[x_spec] + _mod_specs(mod, l, (3, 4, 5), rows_per_seq, tm)
                 + [_layer_spec(wts['g_ffn'], l), _const_spec(wts['g_final'].shape)]
                 + [_const_spec(a.shape) for a in w_bf] + c_in,
        out_specs=[x_spec] + c_out,
        out_shape=[jax.ShapeDtypeStruct((m, d), F32)] + c_shapes,
        compiler_params=pltpu.CompilerParams(
            dimension_semantics=("arbitrary",), vmem_limit_bytes=VMEM_LIMIT_DENSE),
        name=f"ffn_dense_{l}",
    )(x2d, mod, mod, mod, wts['g_ffn'], wts['g_final'], *w_bf, *c_args)
    return out[0], [o.reshape(a.shape[1:]) for o, (a, _) in zip(out[1:], cast)]


def _moe_route_kernel(x_ref, shift_ref, scale_ref, gffn_ref, wr_ref, br_ref, h_ref, meta_ref, *,
                      rows_per_seq, tm):
    shift, scale = _modulation([shift_ref, scale_ref], rows_per_seq, tm)
    h32 = _rms(x_ref[...], gffn_ref[...]) * (1.0 + scale) + shift
    _to_token_tiles(h_ref, h32, tm)
    meta_ref[...] = _router_meta(h32, wr_ref, br_ref)


def _moe_route(x2d, mod, l, wts, rows_per_seq):
    m, d = x2d.shape
    tm = min(TM_FFN, m)
    assert m % tm == 0 and (rows_per_seq == 1 or rows_per_seq % tm == 0)
    i = l // 2
    return pl.pallas_call(
        functools.partial(_moe_route_kernel, rows_per_seq=rows_per_seq, tm=tm),
        grid=(m // tm,),
        in_specs=[pl.BlockSpec((tm, d), lambda t: (t, 0))]
                 + _mod_specs(mod, l, (3, 4), rows_per_seq, tm)
                 + [_layer_spec(wts['g_ffn'], l), _layer_spec(wts['w_router'], i),
                    _layer_spec(wts['b_router'], i)],
        out_specs=[pl.BlockSpec((tm * SUBLANES, LANES), lambda t: (t, 0)),
                   pl.BlockSpec((tm, LANES), lambda t: (t, 0))],
        out_shape=[jax.ShapeDtypeStruct((m * SUBLANES, LANES), F32),
                   jax.ShapeDtypeStruct((m, LANES), F32)],
        compiler_params=pltpu.CompilerParams(
            dimension_semantics=("arbitrary",), vmem_limit_bytes=VMEM_LIMIT),
        name=f"moe_route_{l}_{m}",
    )(x2d, mod, mod, wts['g_ffn'], wts['w_router'], wts['b_router'])


def _super_block_pieces(m_p, m_s, s_tok):
    pieces = []
    for k in range(N_SUPER):
        lo, hi = k * s_tok, (k + 1) * s_tok
        ps = []
        if lo < m_p:
            ps.append((0, lo, 0, min(hi, m_p) - lo))
        if hi > m_p:
            s0 = max(lo, m_p)
            ps.append((1, s0 - m_p, s0 - lo, hi - s0))
        pieces.append(ps)
    return pieces


def _moe_expert_kernel(cnt_ref, off_ref, idx_hbm, g_hbm, hp_hbm, hs_hbm, w1_ref, w3_ref, w2_ref,
                       yp_hbm, ys_hbm, h_scr, y_scr, xbuf, obuf, idx_s, g_s, sem,
                       *, pieces, s_tok, s_pad, l_pad, tm):
    sb = pl.program_id(0)
    e = pl.program_id(1)
    seg = sb * N_EXPERTS + e
    rows = s_tok * SUBLANES

    def piece_copies(k, to_vmem):
        copies = []
        for j, (grp, src_tok, dst_tok, n) in enumerate(pieces[k]):
            hbm = ((hp_hbm, hs_hbm) if to_vmem else (yp_hbm, ys_hbm))[grp]
            hbm = hbm.at[pl.ds(src_tok * SUBLANES, n * SUBLANES)]
            if to_vmem:
                copies.append(pltpu.make_async_copy(
                    hbm, h_scr.at[pl.ds(dst_tok * SUBLANES, n * SUBLANES)], sem.at[j]))
            else:
                copies.append(pltpu.make_async_copy(
                    y_scr.at[pl.ds(dst_tok * SUBLANES, n * SUBLANES)], hbm, sem.at[2 + j]))
        return copies

    def gate_copy(s):
        slot = lax.rem(s, 2)
        src = pl.ds(pl.multiple_of(s * s_pad, IDX_ALIGN), s_pad)
        dst = pl.ds(pl.multiple_of(slot * s_pad, IDX_ALIGN), s_pad)
        return pltpu.make_async_copy(g_hbm.at[src], g_s.at[dst], sem.at[6 + slot])

    def list_copy():
        src = pl.ds(pl.multiple_of(sb * l_pad, IDX_ALIGN), l_pad)
        return pltpu.make_async_copy(idx_hbm.at[src], idx_s, sem.at[4])

    @pl.when(seg == 0)
    def _first_gates():
        gate_copy(seg).start()
        obuf[...] = jnp.zeros(obuf.shape, F32)

    @pl.when(seg + 1 < N_SUPER * N_EXPERTS)
    def _next_gates():
        gate_copy(seg + 1).start()

    for k in range(N_SUPER):
        @pl.when(jnp.logical_and(sb == k, e == 0))
        def _load_super_block(k=k):
            list_copy().start()
            for c in piece_copies(k, True):
                c.start()
            if k > 0:
                for c in piece_copies(k - 1, False):
                    c.wait()
            zrows = 256
            assert rows % zrows == 0
            def zero(i, carry):
                r0 = pl.multiple_of(i * zrows, zrows)
                y_scr[pl.ds(r0, zrows), :] = jnp.zeros((zrows, LANES), F32)
                return carry
            lax.fori_loop(0, rows // zrows, zero, 0)
            for c in piece_copies(k, True):
                c.wait()
            list_copy().wait()

    gate_copy(seg).wait()
    gts = lax.rem(seg, 2) * s_pad
    lst = off_ref[seg]

    def gather(base):
        for r in range(tm):
            t8 = pl.multiple_of(idx_s[lst + base + r] * SUBLANES, SUBLANES)
            xbuf[r * SUBLANES:(r + 1) * SUBLANES, :] = h_scr[pl.ds(t8, SUBLANES), :]

    def scatter_add(base, limit):
        for r0 in range(0, tm, SUBLANES):
            upd = []
            for r in range(r0, r0 + SUBLANES):
                tok = idx_s[lst + base + r]
                t8 = pl.multiple_of(tok * SUBLANES, SUBLANES)
                g = jnp.where(base + r < limit, g_s[gts + tok], 0.0)
                o = obuf[r * SUBLANES:(r + 1) * SUBLANES, :]
                upd.append((t8, y_scr[pl.ds(t8, SUBLANES), :] + g * o))
            for t8, v in upd:
                y_scr[pl.ds(t8, SUBLANES), :] = v

    n_sel = cnt_ref[seg]
    n_tiles = (n_sel + tm - 1) // tm

    def tile(i, carry):
        x = _from_token_tiles(xbuf, tm).astype(BF16)
        gather((i + 1) * tm)
        scatter_add(jnp.maximum(i - 1, 0) * tm, jnp.where(i > 0, n_sel, 0))
        act = (_silu(_dot(x, w1_ref[...])) * _dot(x, w3_ref[...])).astype(BF16)
        _to_token_tiles(obuf, _dot(act, w2_ref[...]), tm)
        return carry

    gather(0)
    lax.fori_loop(0, n_tiles, tile, 0)

    @pl.when(n_tiles > 0)
    def _last_scatter():
        scatter_add((n_tiles - 1) * tm, n_sel)

    for k in range(N_SUPER):
        @pl.when(jnp.logical_and(sb == k, e == N_EXPERTS - 1))
        def _store_super_block(k=k):
            for c in piece_copies(k, False):
                c.start()
            if k == N_SUPER - 1:
                for c in piece_copies(k, False):
                    c.wait()


def _moe_experts(counts, offs, idx, gates, h_p, h_s, l, w_bf, s_tok, s_pad, l_pad):
    tm = TM_EXPERT
    w1, w3, w2 = w_bf
    w_spec = lambda a: pl.BlockSpec((None,) + a.shape[1:], lambda sb, e, cnt, off: (e, 0, 0))
    any_spec = pl.BlockSpec(memory_space=pl.ANY)
    rows = s_tok * SUBLANES
    pieces = _super_block_pieces(h_p.shape[0] // SUBLANES, h_s.shape[0] // SUBLANES, s_tok)
    return pl.pallas_call(
        functools.partial(_moe_expert_kernel, pieces=pieces, s_tok=s_tok, s_pad=s_pad,
                          l_pad=l_pad, tm=tm),
        grid_spec=pltpu.PrefetchScalarGridSpec(
            num_scalar_prefetch=2,
            grid=(N_SUPER, N_EXPERTS),
            in_specs=[any_spec] * 4 + [w_spec(w1), w_spec(w3), w_spec(w2)],
            out_specs=[any_spec, any_spec],
            scratch_shapes=[pltpu.VMEM((rows, LANES), F32), pltpu.VMEM((rows, LANES), F32),
                            pltpu.VMEM((tm * SUBLANES, LANES), F32),
                            pltpu.VMEM((tm * SUBLANES, LANES), F32),
                            pltpu.SMEM((l_pad,), jnp.int32), pltpu.SMEM((2 * s_pad,), F32),
                            pltpu.SemaphoreType.DMA((8,))]),
        out_shape=[jax.ShapeDtypeStruct(h_p.shape, F32), jax.ShapeDtypeStruct(h_s.shape, F32)],
        compiler_params=pltpu.CompilerParams(
            dimension_semantics=("arbitrary", "arbitrary"), vmem_limit_bytes=VMEM_LIMIT_EXPERT),
        name=f"moe_experts_{l}",
    )(counts, offs, idx, gates, h_p, h_s, w1, w3, w2)


def _moe_residual_kernel(x_ref, gate_ref, gfin_ref, y_ref, o_ref, *, rows_per_seq, tm, final_norm):
    (gate,) = _modulation([gate_ref], rows_per_seq, tm)
    y = x_ref[...] + gate * _from_token_tiles(y_ref, tm)
    o_ref[...] = _rms(y, gfin_ref[...]) if final_norm else y


def _moe_residual(x2d, mod, y, l, wts, rows_per_seq, final_norm):
    m, d = x2d.shape
    tm = min(TM_FFN, m)
    assert m % tm == 0
    x_spec = pl.BlockSpec((tm, d), lambda t: (t, 0))
    return pl.pallas_call(
        functools.partial(_moe_residual_kernel, rows_per_seq=rows_per_seq, tm=tm,
                          final_norm=final_norm),
        grid=(m // tm,),
        in_specs=[x_spec] + _mod_specs(mod, l, (5,), rows_per_seq, tm)
                 + [_const_spec(wts['g_final'].shape),
                    pl.BlockSpec((tm * SUBLANES, LANES), lambda t: (t, 0))],
        out_specs=x_spec,
        out_shape=jax.ShapeDtypeStruct((m, d), F32),
        compiler_params=pltpu.CompilerParams(
            dimension_semantics=("arbitrary",), vmem_limit_bytes=VMEM_LIMIT),
        name=f"moe_residual_{l}_{m}",
    )(x2d, mod, wts['g_final'], y)


def _ffn_moe(routed_p, xs, mod_s, l, wts, w_bf):
    h_p, meta_p = routed_p
    m_p, m_s = meta_p.shape[0], xs.shape[0]
    n_tok = m_p + m_s
    s_tok = n_tok // N_SUPER
    assert s_tok * N_SUPER == n_tok and s_tok % SUBLANES == 0
    s_pad = -(-s_tok // IDX_ALIGN) * IDX_ALIGN

    h_s, meta_s = _moe_route(xs, mod_s, l, wts, 1)

    n_meta = 2 * N_EXPERTS + 2
    meta = jnp.concatenate([meta_p[:, :n_meta], meta_s[:, :n_meta]], axis=0)
    per_seg = lambda a: a.reshape(N_SUPER, s_tok, N_EXPERTS).transpose(0, 2, 1)
    gate = per_seg(meta[:, :N_EXPERTS])
    counts = jnp.sum(per_seg(meta[:, N_EXPERTS:2 * N_EXPERTS]).astype(jnp.int32), axis=2)
    offs = jnp.cumsum(counts, axis=1) - counts
    chosen = meta[:, 2 * N_EXPERTS:].astype(jnp.int32).reshape(N_SUPER, s_tok, 2)
    tok = lax.broadcasted_iota(jnp.int32, chosen.shape, 1)
    tok_bits = (s_tok - 1).bit_length()
    keys = (chosen * (1 << tok_bits) + tok).reshape(N_SUPER, 2 * s_tok)
    idx = lax.sort(keys, dimension=1) & ((1 << tok_bits) - 1)
    l_pad = -(-(2 * s_tok + 2 * TM_EXPERT) // IDX_ALIGN) * IDX_ALIGN
    idx = jnp.pad(idx, ((0, 0), (0, l_pad - 2 * s_tok))).reshape(-1)
    gate = jnp.pad(gate, ((0, 0), (0, 0), (0, s_pad - s_tok))).reshape(-1)

    return _moe_experts(counts.reshape(-1), offs.reshape(-1), idx, gate, h_p, h_s, l, w_bf,
                        s_tok, s_pad, l_pad)


def kernel(x_prompt, x_sample, state_conv_a, state_conv_c, state_pool_d, c_prompt, c_sample,
           w_ada, b_ada, g_mix, w_in, conv_a_w, ln_v_g, ln_v_b, w_s, b_s, conv_c_w, conv_c_b,
           ln_c_g, ln_c_b, pool_w, pool_scale, g_out, w_out, g_ffn, w1_dense, w3_dense, w2_dense,
           w_router, b_router, w1_moe, w3_moe, w2_moe, g_final):
    p = dict(g_mix=g_mix, w_in=w_in, conv_a_w=conv_a_w, ln_v_g=ln_v_g, ln_v_b=ln_v_b, w_s=w_s,
             b_s=b_s, conv_c_w=conv_c_w, conv_c_b=conv_c_b, ln_c_g=ln_c_g, ln_c_b=ln_c_b,
             pool_w=pool_w, pool_scale=pool_scale, g_out=g_out, w_out=w_out, g_ffn=g_ffn,
             w1_dense=w1_dense, w3_dense=w3_dense, w2_dense=w2_dense, w_router=w_router,
             b_router=b_router, w1_moe=w1_moe, w3_moe=w3_moe, w2_moe=w2_moe, g_final=g_final)
    n_p, seq, d = x_prompt.shape
    n_s, dec_seq, _ = x_sample.shape
    assert dec_seq == 1 and d == D_MODEL

    mod_p, mod_s = _ada(c_prompt, c_sample, w_ada, b_ada)
    kmajor = lambda s: jnp.transpose(s, (0, 2, 1, 3))
    sa_t, sc_t, sd_t = kmajor(state_conv_a), kmajor(state_conv_c), kmajor(state_pool_d)
    mixw = _mix_params(p)
    pad_e = LANES - N_EXPERTS
    wts = dict(
        g_ffn=g_ffn.reshape(DEPTH, 1, d), g_final=g_final.reshape(1, d),
        w1_dense=w1_dense, w3_dense=w3_dense, w2_dense=w2_dense,
        w_router=jnp.pad(w_router, ((0, 0), (0, 0), (0, pad_e))),
        b_router=jnp.pad(b_router, ((0, 0), (0, pad_e))).reshape(-1, 1, LANES),
        w1_moe=w1_moe, w3_moe=w3_moe, w2_moe=w2_moe)

    xp = x_prompt
    xs = x_sample.reshape(n_s, d)
    states_p = [[], [], [], []]
    states_s = [[], [], [], []]
    pending = None
    moe_bf = None
    for l in range(DEPTH):
        last = l == DEPTH - 1
        moe = l % 2 == 1
        cast = [] if moe else [(wts[k], l // 2) for k in ('w1_dense', 'w3_dense', 'w2_dense')]
        st_p, dense_bf = _mix_prompt(xp, mod_p, l, mixw, 0, pending, wts if moe else None, cast)
        xp, *st_p = st_p
        pending = None
        xs, *st_s = _mix_sample(xs, mod_s, sa_t, sc_t, sd_t, l, mixw, PAST_LEN)
        for acc, s in zip(states_p, st_p[:4]):
            acc.append(s)
        for acc, s in zip(states_s, st_s):
            acc.append(s)
        if moe:
            if moe_bf is None:
                moe_bf = [wts[k][l // 2].astype(BF16) for k in ('w1_moe', 'w3_moe', 'w2_moe')]
            y_p, y_s = _ffn_moe(st_p[4:], xs, mod_s, l, wts, moe_bf)
            moe_bf = None
            xs = _moe_residual(xs, mod_s, y_s, l, wts, 1, last)
            if last:
                xp = _moe_residual(xp.reshape(n_p * seq, d), mod_p, y_p, l, wts, seq,
                                   True).reshape(n_p, seq, d)
            else:
                pending = (y_p, l)
        else:
            cast = ([(wts[k], (l + 1) // 2) for k in ('w1_moe', 'w3_moe', 'w2_moe')]
                    if l + 1 < DEPTH else [])
            xp2d, moe_bf = _ffn_dense(xp.reshape(n_p * seq, d), mod_p, l, wts, dense_bf, seq, last,
                                      cast)
            xp = xp2d.reshape(n_p, seq, d)
            xs, _ = _ffn_dense(xs, mod_s, l, wts, dense_bf, 1, last)
            moe_bf = moe_bf or None

    a_p, c_p, d_p, v_p = (jnp.stack(s) for s in states_p)
    a_s, c_s, d_s = (kmajor(jnp.stack(s)) for s in states_s[:3])
    v_s = jnp.stack(states_s[3]).reshape(DEPTH, n_s, 1, W_GROUP)
    return (xp, xs.reshape(n_s, 1, d), a_p, c_p, d_p, v_p, a_s, c_s, d_s, v_s)
```

```python
import functools

import jax
import jax.numpy as jnp
from jax import lax
from jax.experimental import pallas as pl
from jax.experimental.pallas import tpu as pltpu

D_MODEL = 1024
DEPTH = 4
W_GROUP = 256
CONV_A = 3
CHUNK = 128
N_HEADS_B = 4
HEAD_B = 64
CONV_C = 31
POOL_WINDOWS = (2, 4, 8, 16)
POOL_MAX = 16
GROUP_D = 64
IN_COLS = 2048
N_EXPERTS = 8
PAST_LEN = 16384
EPS = 1e-6

LANES = 128
LANE_TILES = W_GROUP // LANES
HIST_A = 8
HIST_C = 32
HIST_D = 32
TL_MIX = 512
ROW_CHUNK = 64
TM_FFN = 512
SUBLANES = 8
N_SUPER = 4
TM_EXPERT = 256
IDX_ALIGN = 1024
VMEM_LIMIT = 56 * 1024 * 1024
VMEM_LIMIT_DENSE = 60 * 1024 * 1024
VMEM_LIMIT_EXPERT = 62 * 1024 * 1024

F32 = jnp.float32
BF16 = jnp.bfloat16


def _dot(a, b):
    return jnp.dot(a, b, preferred_element_type=F32)


def _rms(x, g):
    return x * lax.rsqrt(jnp.mean(x * x, axis=-1, keepdims=True) + EPS) * g


def _ln(x, g, b):
    xc = x - jnp.mean(x, axis=-1, keepdims=True)
    var = jnp.mean(xc * xc, axis=-1, keepdims=True)
    return xc * lax.rsqrt(var + EPS) * g + b


def _silu(x):
    return x * jax.nn.sigmoid(x)


def _lane_group_select(vals, shape):
    lane = lax.broadcasted_iota(jnp.int32, shape, 1)
    out = vals[3]
    for g in (2, 1, 0):
        out = jnp.where(lane < (g + 1) * GROUP_D, vals[g], out)
    return out


def _ada_kernel(cp_ref, cs_ref, w_ref, b_ref, op_ref, os_ref):
    w = w_ref[...].astype(BF16)
    b = b_ref[...]
    op_ref[...] = _dot(_silu(cp_ref[...]).astype(BF16), w) + b
    os_ref[...] = _dot(_silu(cs_ref[...]).astype(BF16), w) + b


def _ada(c_prompt, c_sample, w_ada, b_ada):
    n_p, n_s = c_prompt.shape[0], c_sample.shape[0]
    d = D_MODEL
    return pl.pallas_call(
        _ada_kernel,
        grid=(DEPTH, 6),
        in_specs=[
            pl.BlockSpec((n_p, d), lambda l, k: (0, 0)),
            pl.BlockSpec((n_s, d), lambda l, k: (0, 0)),
            pl.BlockSpec((None, d, d), lambda l, k: (l, 0, k)),
            pl.BlockSpec((None, 1, d), lambda l, k: (l, 0, k)),
        ],
        out_specs=[
            pl.BlockSpec((None, None, n_p, d), lambda l, k: (l, k, 0, 0)),
            pl.BlockSpec((None, None, n_s, d), lambda l, k: (l, k, 0, 0)),
        ],
        out_shape=[
            jax.ShapeDtypeStruct((DEPTH, 6, n_p, d), F32),
            jax.ShapeDtypeStruct((DEPTH, 6, n_s, d), F32),
        ],
        compiler_params=pltpu.CompilerParams(
            dimension_semantics=("arbitrary", "arbitrary"), vmem_limit_bytes=VMEM_LIMIT),
        name="ada",
    )(c_prompt, c_sample, w_ada, b_ada.reshape(DEPTH, 1, 6 * d))


def _masked_ws(ws_ref):
    r = lax.broadcasted_iota(jnp.int32, (CHUNK, CHUNK), 0)
    c = lax.broadcasted_iota(jnp.int32, (CHUNK, CHUNK), 1)
    return [jnp.where(c <= r, ws_ref[h], 0.0).astype(BF16) for h in range(N_HEADS_B)]


def _ext_rows(ref, j, row0, n):
    return ref[pl.ds(row0 * LANE_TILES + j, n, stride=LANE_TILES), :]


def _ext_store(ref, row0, val):
    for j in range(LANE_TILES):
        ref[pl.ds(row0 * LANE_TILES + j, val.shape[0], stride=LANE_TILES), :] = (
            val[:, j * LANES:(j + 1) * LANES])


def _ext_load(ref, row0, n):
    return jnp.concatenate([_ext_rows(ref, j, row0, n) for j in range(LANE_TILES)], axis=1)


def _merge_and_project(x, gate, outs, gout_ref, wout_bf):
    merged = jnp.concatenate(
        [_rms(o, gout_ref[i:i + 1, :]) for i, o in enumerate(outs)], axis=1).astype(BF16)
    return x + gate * _dot(merged, wout_bf)


def _to_token_tiles(ref, val, tm):
    for k in range(SUBLANES):
        ref[pl.ds(k, tm, stride=SUBLANES), :] = val[:, k * LANES:(k + 1) * LANES]


def _from_token_tiles(ref, tm):
    return jnp.concatenate(
        [ref[pl.ds(k, tm, stride=SUBLANES), :] for k in range(SUBLANES)], axis=1)


def _router_meta(h32, wr_ref, br_ref):
    wr = wr_ref[...]
    h_hi = h32.astype(BF16)
    h_lo = (h32 - h_hi.astype(F32)).astype(BF16)
    w_hi = wr.astype(BF16)
    w_lo = (wr - w_hi.astype(F32)).astype(BF16)
    logits = _dot(h_hi, w_hi) + (_dot(h_lo, w_hi) + _dot(h_hi, w_lo)) + br_ref[...]
    lane = lax.broadcasted_iota(jnp.int32, logits.shape, 1)
    lane_f = lane.astype(F32)
    neg = jnp.float32(-jnp.inf)
    logits = jnp.where(lane < N_EXPERTS, logits, neg)
    m1 = jnp.max(logits, axis=-1, keepdims=True)
    i1 = jnp.min(jnp.where(logits == m1, lane_f, float(LANES)), axis=-1, keepdims=True)
    rest = jnp.where(lane_f == i1, neg, logits)
    m2 = jnp.max(rest, axis=-1, keepdims=True)
    i2 = jnp.min(jnp.where(rest == m2, lane_f, float(LANES)), axis=-1, keepdims=True)
    e = jnp.exp(m2 - m1)
    g1 = 1.0 / (1.0 + e)
    g2 = e / (1.0 + e)
    comb = jnp.where(lane_f == i1, g1, 0.0) + jnp.where(lane_f == i2, g2, 0.0)
    flags = (jnp.where(lane_f == i1 + N_EXPERTS, 1.0, 0.0)
             + jnp.where(lane_f == i2 + N_EXPERTS, 1.0, 0.0))
    chosen = (jnp.where(lane == 2 * N_EXPERTS, i1, 0.0)
              + jnp.where(lane == 2 * N_EXPERTS + 1, i2, 0.0))
    return comb + flags + chosen


def _mix_prompt_kernel(*refs, tl, start_pos, pre, post, n_cast):
    refs = list(refs)
    take = lambda n: [refs.pop(0) for _ in range(n)]
    (x_ref,) = take(1)
    y_ref, gprev_ref = take(2) if pre else (None, None)
    shift_ref, scale_ref, gate_ref = take(3)
    shift2_ref, scale2_ref = take(2) if post else (None, None)
    (gmix_ref, win_ref, wout_ref, convaw_ref, lnvg_ref, lnvb_ref, ws_ref, bsfull_ref, convcw_ref,
     convcb_ref, lncg_ref, lncb_ref, poolw_ref, poolscale_ref, gout_ref) = take(15)
    gffn_ref, wr_ref, br_ref = take(3) if post else (None, None, None)
    cast_src = take(n_cast)
    xo_ref, na_ref, nc_ref, nd_ref, nv_ref = take(5)
    h_ref, meta_ref = take(2) if post else (None, None)
    cast_dst = take(n_cast)
    win_bf, wout_bf, exta, extc, extd, ext2, ext4, ext8 = take(8)
    assert not refs
    for src, dst in zip(cast_src, cast_dst):
        dst[...] = src[...].astype(BF16)
    b = pl.program_id(0)
    t = pl.program_id(1)
    last_t = pl.num_programs(1) - 1

    @pl.when(jnp.logical_and(b == 0, t == 0))
    def _cast_weights():
        rows = 128
        def body(i, carry):
            r0 = pl.multiple_of(i * rows, rows)
            win_bf[pl.ds(r0, rows), :] = win_ref[pl.ds(r0, rows), :].astype(BF16)
            wout_bf[pl.ds(r0, rows), :] = wout_ref[pl.ds(r0, rows), :].astype(BF16)
            return carry
        lax.fori_loop(0, D_MODEL // rows, body, 0)

    @pl.when(t == 0)
    def _zero_history():
        for ref, hist in ((exta, HIST_A), (extc, HIST_C), (extd, HIST_D)):
            ref[0:hist * LANE_TILES, :] = jnp.zeros((hist * LANE_TILES, LANES), F32)

    x = x_ref[...]
    if pre:
        x = x + gprev_ref[pl.ds(b, 1), :] * _from_token_tiles(y_ref, tl)
    shift = shift_ref[pl.ds(b, 1), :]
    scale = scale_ref[pl.ds(b, 1), :]
    gate = gate_ref[pl.ds(b, 1), :]
    h = _rms(x, gmix_ref[...] * (1.0 + scale)) + shift
    proj = _dot(h.astype(BF16), win_bf[...])
    a_b, a_c, a_h, b_u, b_v, c_a, c_g, d_p = [
        proj[:, i * W_GROUP:(i + 1) * W_GROUP] for i in range(8)]

    def dwconv(ext_ref, w_ref, hist, width):
        off = hist - (width - 1)
        halves = []
        for j in range(LANE_TILES):
            chunks = []
            for c0 in range(0, tl, ROW_CHUNK):
                acc = None
                for k in range(width):
                    term = (_ext_rows(ext_ref, j, off + c0 + k, ROW_CHUNK)
                            * w_ref[k:k + 1, j * LANES:(j + 1) * LANES])
                    acc = term if acc is None else acc + term
                chunks.append(acc)
            halves.append(jnp.concatenate(chunks, axis=0))
        return jnp.concatenate(halves, axis=1)

    _ext_store(exta, HIST_A, a_c * a_h)
    out_a = a_b * dwconv(exta, convaw_ref, HIST_A, CONV_A)

    v_n = _ln(b_v, lnvg_ref[...], lnvb_ref[...])
    v_bf = v_n.astype(BF16)
    wm = _masked_ws(ws_ref)
    lane = lax.broadcasted_iota(jnp.int32, (CHUNK, W_GROUP), 1)
    mixed_chunks = []
    for j in range(tl // CHUNK):
        vc = v_bf[j * CHUNK:(j + 1) * CHUNK, :]
        mixed = _dot(wm[3], vc)
        for hd in (2, 1, 0):
            mixed = jnp.where(lane < (hd + 1) * HEAD_B, _dot(wm[hd], vc), mixed)
        mixed_chunks.append(mixed + bsfull_ref[...])
    out_b = b_u * jnp.concatenate(mixed_chunks, axis=0)

    _ext_store(extc, HIST_C, c_a * jax.nn.sigmoid(c_g))
    y_c = dwconv(extc, convcw_ref, HIST_C, CONV_C) + convcb_ref[...]
    out_c = _silu(_ln(y_c, lncg_ref[...], lncb_ref[...]))

    n = HIST_D + tl
    _ext_store(extd, HIST_D, d_p)
    bufs = (extd, ext2, ext4, ext8)
    pos1 = start_pos + 1 + t * tl + lax.broadcasted_iota(jnp.int32, (tl, LANES), 0)
    low_group = lax.broadcasted_iota(jnp.int32, (tl, LANES), 1) < GROUP_D
    means = []
    for j in range(LANE_TILES):
        levels = 2 * (j + 1)
        for lv in range(levels - 1):
            first = 8 * (lv + 1)
            bufs[lv + 1][pl.ds(first * LANE_TILES + j, n - first, stride=LANE_TILES), :] = (
                _ext_rows(bufs[lv], j, first, n - first)
                + _ext_rows(bufs[lv], j, first - (1 << lv), n - first))
        prev = bufs[levels - 1]
        s_lo = _ext_rows(prev, j, HIST_D, tl)
        s_hi = s_lo + _ext_rows(prev, j, HIST_D - (1 << (levels - 1)), tl)
        w_lo, w_hi = POOL_WINDOWS[2 * j], POOL_WINDOWS[2 * j + 1]
        cnt = jnp.where(low_group, jnp.minimum(pos1, w_lo), jnp.minimum(pos1, w_hi)).astype(F32)
        means.append(jnp.where(low_group, s_lo, s_hi) / cnt)
    pooled = jnp.concatenate(means, axis=1) - d_p
    out_d = _dot(pooled.astype(BF16), poolw_ref[...].astype(BF16)) * poolscale_ref[...]

    x_new = _merge_and_project(x, gate, [out_a, out_b, out_c, out_d], gout_ref, wout_bf[...])
    xo_ref[...] = x_new
    if post:
        h2 = (_rms(x_new, gffn_ref[...]) * (1.0 + scale2_ref[pl.ds(b, 1), :])
              + shift2_ref[pl.ds(b, 1), :])
        _to_token_tiles(h_ref, h2, tl)
        meta_ref[...] = _router_meta(h2, wr_ref, br_ref)

    @pl.when(t == last_t)
    def _emit_state():
        na_ref[...] = _ext_load(exta, HIST_A + tl - (CONV_A - 1), CONV_A - 1)
        nc_ref[...] = _ext_load(extc, HIST_C + tl - (CONV_C - 1), CONV_C - 1)
        nd_ref[...] = _ext_load(extd, HIST_D + tl - (POOL_MAX - 1), POOL_MAX - 1)
        nv_ref[...] = v_n[tl - CHUNK:tl, :]

    for ref, hist in ((exta, HIST_A), (extc, HIST_C), (extd, HIST_D)):
        ref[0:hist * LANE_TILES, :] = ref[tl * LANE_TILES:(tl + hist) * LANE_TILES, :]


def _mix_params(p):
    r3 = lambda a: a.reshape(DEPTH, 1, -1)
    eye = jnp.eye(4, dtype=F32)
    pool_bd = (eye[None, :, None, :, None] * p['pool_w'][:, :, :, None, :]).reshape(
        DEPTH, W_GROUP, W_GROUP)
    return dict(
        gmix=r3(p['g_mix']), w_in=p['w_in'], w_out=p['w_out'], conva=p['conv_a_w'],
        lnvg=r3(p['ln_v_g']), lnvb=r3(p['ln_v_b']), ws=p['w_s'],
        bsfull=jnp.repeat(jnp.swapaxes(p['b_s'], 1, 2), HEAD_B, axis=2),
        ws0=r3(jnp.repeat(p['w_s'][:, :, 0, 0], HEAD_B, axis=1)),
        bs0=r3(jnp.repeat(p['b_s'][:, :, 0], HEAD_B, axis=1)),
        convc=p['conv_c_w'], convcb=r3(p['conv_c_b']), lncg=r3(p['ln_c_g']), lncb=r3(p['ln_c_b']),
        poolw=pool_bd, poolscale=r3(p['pool_scale']), gout=p['g_out'])


def _layer_spec(a, l):
    nd = a.ndim - 1
    return pl.BlockSpec((None,) + a.shape[1:], lambda *_: (l,) + (0,) * nd,
                        pipeline_mode=pl.Buffered(1))


def _const_spec(shape):
    nd = len(shape)
    return pl.BlockSpec(shape, lambda *_: (0,) * nd, pipeline_mode=pl.Buffered(1))


BF16_ROWS = 16


def _cast_job(cast, steps, step_of):
    args, in_specs, out_specs, out_shapes = [], [], [], []
    for a, j in cast:
        groups = a.shape[1] if a.ndim == 4 else 1
        rows, cols = a.shape[-2:]
        assert steps % groups == 0
        blocks = steps // groups
        while rows % (blocks * BF16_ROWS):
            assert blocks % 2 == 0
            blocks //= 2
        rep = steps // (groups * blocks)

        def block_of(*g, rep=rep, blocks=blocks, grouped=a.ndim == 4):
            q = step_of(*g) // rep
            return (q // blocks, q % blocks, 0) if grouped else (q, 0)

        lead = (None,) * (a.ndim - 2)
        args.append(a)
        in_specs.append(pl.BlockSpec(lead + (rows // blocks, cols),
                                     lambda *g, j=j, f=block_of: (j,) + f(*g)))
        out_specs.append(pl.BlockSpec(lead[1:] + (rows // blocks, cols), block_of))
        out_shapes.append(jax.ShapeDtypeStruct(a.shape[1:], BF16))
    return args, in_specs, out_specs, out_shapes


def _mix_prompt(x, mod_p, l, w, start_pos, prev_moe=None, wts=None, cast=()):
    n_b, seq, d = x.shape
    tl = TL_MIX
    nt = seq // tl
    assert seq % tl == 0 and tl % CHUNK == 0 and seq >= CHUNK
    pre, post = prev_moe is not None, wts is not None
    mod_spec = lambda lay, k: pl.BlockSpec((None, None, n_b, d), lambda b, t: (lay, k, 0, 0))
    tiles_spec = lambda rows: pl.BlockSpec((tl * rows, LANES), lambda b, t: (b * nt + t, 0))
    weights = [w[k] for k in ('gmix', 'w_in', 'w_out', 'conva', 'lnvg', 'lnvb', 'ws', 'bsfull',
                              'convc', 'convcb', 'lncg', 'lncb', 'poolw', 'poolscale', 'gout')]
    state_spec = lambda r: pl.BlockSpec((None, r, W_GROUP), lambda b, t: (b, 0, 0))
    ext = lambda hist: pltpu.VMEM(((hist + tl) * LANE_TILES, LANES), F32)

    args, in_specs = [x], [pl.BlockSpec((None, tl, d), lambda b, t: (b, t, 0))]
    if pre:
        y_prev, l_prev = prev_moe
        args += [y_prev, mod_p]
        in_specs += [tiles_spec(SUBLANES), mod_spec(l_prev, 5)]
    args += [mod_p] * 3
    in_specs += [mod_spec(l, 0), mod_spec(l, 1), mod_spec(l, 2)]
    if post:
        args += [mod_p] * 2
        in_specs += [mod_spec(l, 3), mod_spec(l, 4)]
    args += weights
    in_specs += [_layer_spec(a, l) for a in weights]
    out_specs = [pl.BlockSpec((None, tl, d), lambda b, t: (b, t, 0)),
                 state_spec(CONV_A - 1), state_spec(CONV_C - 1), state_spec(POOL_MAX - 1),
                 state_spec(CHUNK)]
    out_shape = [jax.ShapeDtypeStruct(x.shape, F32),
                 jax.ShapeDtypeStruct((n_b, CONV_A - 1, W_GROUP), F32),
                 jax.ShapeDtypeStruct((n_b, CONV_C - 1, W_GROUP), F32),
                 jax.ShapeDtypeStruct((n_b, POOL_MAX - 1, W_GROUP), F32),
                 jax.ShapeDtypeStruct((n_b, CHUNK, W_GROUP), F32)]
    if post:
        i = l // 2
        args += [wts['g_ffn'], wts['w_router'], wts['b_router']]
        in_specs += [_layer_spec(wts['g_ffn'], l), _layer_spec(wts['w_router'], i),
                     _layer_spec(wts['b_router'], i)]
        out_specs += [tiles_spec(SUBLANES), tiles_spec(1)]
        out_shape += [jax.ShapeDtypeStruct((n_b * seq * SUBLANES, LANES), F32),
                      jax.ShapeDtypeStruct((n_b * seq, LANES), F32)]
    c_args, c_in, c_out, c_shapes = _cast_job(cast, n_b * nt, lambda b, t: b * nt + t)
    out = pl.pallas_call(
        functools.partial(_mix_prompt_kernel, tl=tl, start_pos=start_pos, pre=pre, post=post,
                          n_cast=len(cast)),
        grid=(n_b, nt),
        in_specs=in_specs + c_in,
        out_specs=out_specs + c_out,
        out_shape=out_shape + c_shapes,
        scratch_shapes=[pltpu.VMEM((d, IN_COLS), BF16), pltpu.VMEM((d, d), BF16),
                        ext(HIST_A), ext(HIST_C), ext(HIST_D), ext(HIST_D), ext(HIST_D),
                        ext(HIST_D)],
        compiler_params=pltpu.CompilerParams(
            dimension_semantics=("arbitrary", "arbitrary"), vmem_limit_bytes=VMEM_LIMIT),
        name=f"mix_prompt_{l}",
    )(*args, *c_args)
    n_out = len(out) - len(cast)
    return out[:n_out], [o.reshape(a.shape[1:]) for o, (a, _) in zip(out[n_out:], cast)]


def _mix_sample_kernel(x_ref, shift_ref, scale_ref, gate_ref, sa_ref, sc_ref, sd_ref, gmix_ref,
                       win_ref, wout_ref, convaw_ref, lnvg_ref, lnvb_ref, ws0_ref, bs0_ref,
                       convcw_ref, convcb_ref, lncg_ref, lncb_ref, poolw_ref, poolscale_ref,
                       gout_ref, xo_ref, na_ref, nc_ref, nd_ref, nv_ref, *, start_pos):
    x = x_ref[...]
    h = _rms(x, gmix_ref[...]) * (1.0 + scale_ref[...]) + shift_ref[...]
    proj = _dot(h.astype(BF16), win_ref[...].astype(BF16))
    a_b, a_c, a_h, b_u, b_v, c_a, c_g, d_p = [
        proj[:, i * W_GROUP:(i + 1) * W_GROUP] for i in range(8)]

    ch = a_c * a_h
    y_a = convaw_ref[CONV_A - 1:CONV_A, :] * ch
    for k in range(CONV_A - 1):
        y_a = y_a + convaw_ref[k:k + 1, :] * sa_ref[k]
    out_a = a_b * y_a
    for k in range(CONV_A - 2):
        na_ref[k] = sa_ref[k + 1]
    na_ref[CONV_A - 2] = ch

    v_n = _ln(b_v, lnvg_ref[...], lnvb_ref[...])
    out_b = b_u * (ws0_ref[...] * v_n + bs0_ref[...])
    nv_ref[...] = v_n

    glu = c_a * jax.nn.sigmoid(c_g)
    y_c = convcw_ref[CONV_C - 1:CONV_C, :] * glu + convcb_ref[...]
    for k in range(CONV_C - 1):
        y_c = y_c + convcw_ref[k:k + 1, :] * sc_ref[k]
    out_c = _silu(_ln(y_c, lncg_ref[...], lncb_ref[...]))
    for k in range(CONV_C - 2):
        nc_ref[k] = sc_ref[k + 1]
    nc_ref[CONV_C - 2] = glu

    hist = POOL_MAX - 1
    run = d_p
    taken = 0
    sums = []
    for w in POOL_WINDOWS:
        while taken < w - 1:
            run = run + sd_ref[hist - 1 - taken]
            taken += 1
        sums.append(run / float(min(start_pos + 1, w)))
    pooled = _lane_group_select(sums, d_p.shape) - d_p
    out_d = _dot(pooled.astype(BF16), poolw_ref[...].astype(BF16)) * poolscale_ref[...]
    for k in range(hist - 1):
        nd_ref[k] = sd_ref[k + 1]
    nd_ref[hist - 1] = d_p

    xo_ref[...] = _merge_and_project(x, gate_ref[...], [out_a, out_b, out_c, out_d], gout_ref,
                                     wout_ref[...].astype(BF16))


def _mix_sample(x, mod_s, sa_t, sc_t, sd_t, l, w, start_pos):
    n, d = x.shape
    assert start_pos + 1 >= POOL_MAX
    mod_spec = lambda k: pl.BlockSpec((None, None, n, d), lambda i: (l, k, 0, 0))
    st_spec = lambda r: pl.BlockSpec((None, r, n, W_GROUP), lambda i: (l, 0, 0, 0))
    weights = [w[k] for k in ('gmix', 'w_in', 'w_out', 'conva', 'lnvg', 'lnvb', 'ws0', 'bs0',
                              'convc', 'convcb', 'lncg', 'lncb', 'poolw', 'poolscale', 'gout')]
    full = lambda shape: pl.BlockSpec(shape, lambda i: (0,) * len(shape))
    return pl.pallas_call(
        functools.partial(_mix_sample_kernel, start_pos=start_pos),
        grid=(1,),
        in_specs=[full((n, d)), mod_spec(0), mod_spec(1), mod_spec(2),
                  st_spec(CONV_A - 1), st_spec(CONV_C - 1), st_spec(POOL_MAX - 1)]
                 + [_layer_spec(a, l) for a in weights],
        out_specs=[full((n, d)), full((CONV_A - 1, n, W_GROUP)), full((CONV_C - 1, n, W_GROUP)),
                   full((POOL_MAX - 1, n, W_GROUP)), full((n, W_GROUP))],
        out_shape=[jax.ShapeDtypeStruct((n, d), F32),
                   jax.ShapeDtypeStruct((CONV_A - 1, n, W_GROUP), F32),
                   jax.ShapeDtypeStruct((CONV_C - 1, n, W_GROUP), F32),
                   jax.ShapeDtypeStruct((POOL_MAX - 1, n, W_GROUP), F32),
                   jax.ShapeDtypeStruct((n, W_GROUP), F32)],
        compiler_params=pltpu.CompilerParams(
            dimension_semantics=("arbitrary",), vmem_limit_bytes=VMEM_LIMIT),
        name=f"mix_sample_{l}",
    )(x, mod_s, mod_s, mod_s, sa_t, sc_t, sd_t, *weights)


def _modulation(refs, rows_per_seq, tm):
    if rows_per_seq == 1:
        return [r[...] for r in refs]
    b = (pl.program_id(0) * tm) // rows_per_seq
    return [r[pl.ds(b, 1), :] for r in refs]


def _ffn_dense_kernel(x_ref, shift_ref, scale_ref, gate_ref, gffn_ref, gfin_ref, w1_ref, w3_ref,
                      w2_ref, *rest, rows_per_seq, tm, final_norm):
    n_cast = (len(rest) - 1) // 2
    o_ref = rest[n_cast]
    x = x_ref[...]
    shift, scale, gate = _modulation([shift_ref, scale_ref, gate_ref], rows_per_seq, tm)
    h = (_rms(x, gffn_ref[...]) * (1.0 + scale) + shift).astype(BF16)
    act = (_silu(_dot(h, w1_ref[...])) * _dot(h, w3_ref[...])).astype(BF16)
    y = x + gate * _dot(act, w2_ref[...])
    o_ref[...] = _rms(y, gfin_ref[...]) if final_norm else y
    for src, dst in zip(rest[:n_cast], rest[n_cast + 1:]):
        dst[...] = src[...].astype(BF16)


def _mod_specs(mod, l, ks, rows_per_seq, tm):
    d = mod.shape[-1]
    if rows_per_seq == 1:
        return [pl.BlockSpec((None, None, tm, d), lambda t, k=k: (l, k, t, 0)) for k in ks]
    n_seq = mod.shape[2]
    return [pl.BlockSpec((None, None, n_seq, d), lambda t, k=k: (l, k, 0, 0)) for k in ks]


def _ffn_dense(x2d, mod, l, wts, w_bf, rows_per_seq, final_norm, cast=()):
    m, d = x2d.shape
    tm = min(TM_FFN, m)
    assert m % tm == 0 and (rows_per_seq == 1 or rows_per_seq % tm == 0)
    steps = m // tm
    x_spec = pl.BlockSpec((tm, d), lambda t: (t, 0))
    c_args, c_in, c_out, c_shapes = _cast_job(cast, steps, lambda t: t)
    out = pl.pallas_call(
        functools.partial(_ffn_dense_kernel, rows_per_seq=rows_per_seq, tm=tm,
                          final_norm=final_norm),
        grid=(steps,),
        in_specs=[x_spec] + _mod_specs(mod, l, (3, 4, 5), rows_per_seq, tm)
                 + [_layer_spec(wts['g_ffn'], l), _const_spec(wts['g_final'].shape)]
                 + [_const_spec(a.shape) for a in w_bf] + c_in,
        out_specs=[x_spec] + c_out,
        out_shape=[jax.ShapeDtypeStruct((m, d), F32)] + c_shapes,
        compiler_params=pltpu.CompilerParams(
            dimension_semantics=("arbitrary",), vmem_limit_bytes=VMEM_LIMIT_DENSE),
        name=f"ffn_dense_{l}",
    )(x2d, mod, mod, mod, wts['g_ffn'], wts['g_final'], *w_bf, *c_args)
    return out[0], [o.reshape(a.shape[1:]) for o, (a, _) in zip(out[1:], cast)]


def _moe_route_kernel(x_ref, shift_ref, scale_ref, gffn_ref, wr_ref, br_ref, h_ref, meta_ref, *,
                      rows_per_seq, tm):
    shift, scale = _modulation([shift_ref, scale_ref], rows_per_seq, tm)
    h32 = _rms(x_ref[...], gffn_ref[...]) * (1.0 + scale) + shift
    _to_token_tiles(h_ref, h32, tm)
    meta_ref[...] = _router_meta(h32, wr_ref, br_ref)


def _moe_route(x2d, mod, l, wts, rows_per_seq):
    m, d = x2d.shape
    tm = min(TM_FFN, m)
    assert m % tm == 0 and (rows_per_seq == 1 or rows_per_seq % tm == 0)
    i = l // 2
    return pl.pallas_call(
        functools.partial(_moe_route_kernel, rows_per_seq=rows_per_seq, tm=tm),
        grid=(m // tm,),
        in_specs=[pl.BlockSpec((tm, d), lambda t: (t, 0))]
                 + _mod_specs(mod, l, (3, 4), rows_per_seq, tm)
                 + [_layer_spec(wts['g_ffn'], l), _layer_spec(wts['w_router'], i),
                    _layer_spec(wts['b_router'], i)],
        out_specs=[pl.BlockSpec((tm * SUBLANES, LANES), lambda t: (t, 0)),
                   pl.BlockSpec((tm, LANES), lambda t: (t, 0))],
        out_shape=[jax.ShapeDtypeStruct((m * SUBLANES, LANES), F32),
                   jax.ShapeDtypeStruct((m, LANES), F32)],
        compiler_params=pltpu.CompilerParams(
            dimension_semantics=("arbitrary",), vmem_limit_bytes=VMEM_LIMIT),
        name=f"moe_route_{l}_{m}",
    )(x2d, mod, mod, wts['g_ffn'], wts['w_router'], wts['b_router'])


def _super_block_pieces(m_p, m_s, s_tok):
    pieces = []
    for k in range(N_SUPER):
        lo, hi = k * s_tok, (k + 1) * s_tok
        ps = []
        if lo < m_p:
            ps.append((0, lo, 0, min(hi, m_p) - lo))
        if hi > m_p:
            s0 = max(lo, m_p)
            ps.append((1, s0 - m_p, s0 - lo, hi - s0))
        pieces.append(ps)
    return pieces


def _moe_expert_kernel(cnt_ref, off_ref, idx_hbm, g_hbm, hp_hbm, hs_hbm, w1_ref, w3_ref, w2_ref,
                       yp_hbm, ys_hbm, h_scr, y_scr, xbuf, obuf, idx_s, g_s, sem,
                       *, pieces, s_tok, s_pad, l_pad, tm):
    sb = pl.program_id(0)
    e = pl.program_id(1)
    seg = sb * N_EXPERTS + e
    rows = s_tok * SUBLANES

    def piece_copies(k, to_vmem):
        copies = []
        for j, (grp, src_tok, dst_tok, n) in enumerate(pieces[k]):
            hbm = ((hp_hbm, hs_hbm) if to_vmem else (yp_hbm, ys_hbm))[grp]
            hbm = hbm.at[pl.ds(src_tok * SUBLANES, n * SUBLANES)]
            if to_vmem:
                copies.append(pltpu.make_async_copy(
                    hbm, h_scr.at[pl.ds(dst_tok * SUBLANES, n * SUBLANES)], sem.at[j]))
            else:
                copies.append(pltpu.make_async_copy(
                    y_scr.at[pl.ds(dst_tok * SUBLANES, n * SUBLANES)], hbm, sem.at[2 + j]))
        return copies

    def gate_copy(s):
        slot = lax.rem(s, 2)
        src = pl.ds(pl.multiple_of(s * s_pad, IDX_ALIGN), s_pad)
        dst = pl.ds(pl.multiple_of(slot * s_pad, IDX_ALIGN), s_pad)
        return pltpu.make_async_copy(g_hbm.at[src], g_s.at[dst], sem.at[6 + slot])

    def list_copy():
        src = pl.ds(pl.multiple_of(sb * l_pad, IDX_ALIGN), l_pad)
        return pltpu.make_async_copy(idx_hbm.at[src], idx_s, sem.at[4])

    @pl.when(seg == 0)
    def _first_gates():
        gate_copy(seg).start()
        obuf[...] = jnp.zeros(obuf.shape, F32)

    @pl.when(seg + 1 < N_SUPER * N_EXPERTS)
    def _next_gates():
        gate_copy(seg + 1).start()

    for k in range(N_SUPER):
        @pl.when(jnp.logical_and(sb == k, e == 0))
        def _load_super_block(k=k):
            list_copy().start()
            for c in piece_copies(k, True):
                c.start()
            if k > 0:
                for c in piece_copies(k - 1, False):
                    c.wait()
            zrows = 256
            assert rows % zrows == 0
            def zero(i, carry):
                r0 = pl.multiple_of(i * zrows, zrows)
                y_scr[pl.ds(r0, zrows), :] = jnp.zeros((zrows, LANES), F32)
                return carry
            lax.fori_loop(0, rows // zrows, zero, 0)
            for c in piece_copies(k, True):
                c.wait()
            list_copy().wait()

    gate_copy(seg).wait()
    gts = lax.rem(seg, 2) * s_pad
    lst = off_ref[seg]

    def gather(base):
        for r in range(tm):
            t8 = pl.multiple_of(idx_s[lst + base + r] * SUBLANES, SUBLANES)
            xbuf[r * SUBLANES:(r + 1) * SUBLANES, :] = h_scr[pl.ds(t8, SUBLANES), :]

    def scatter_add(base, limit):
        for r0 in range(0, tm, SUBLANES):
            upd = []
            for r in range(r0, r0 + SUBLANES):
                tok = idx_s[lst + base + r]
                t8 = pl.multiple_of(tok * SUBLANES, SUBLANES)
                g = jnp.where(base + r < limit, g_s[gts + tok], 0.0)
                o = obuf[r * SUBLANES:(r + 1) * SUBLANES, :]
                upd.append((t8, y_scr[pl.ds(t8, SUBLANES), :] + g * o))
            for t8, v in upd:
                y_scr[pl.ds(t8, SUBLANES), :] = v

    n_sel = cnt_ref[seg]
    n_tiles = (n_sel + tm - 1) // tm

    def tile(i, carry):
        x = _from_token_tiles(xbuf, tm).astype(BF16)
        gather((i + 1) * tm)
        scatter_add(jnp.maximum(i - 1, 0) * tm, jnp.where(i > 0, n_sel, 0))
        act = (_silu(_dot(x, w1_ref[...])) * _dot(x, w3_ref[...])).astype(BF16)
        _to_token_tiles(obuf, _dot(act, w2_ref[...]), tm)
        return carry

    gather(0)
    lax.fori_loop(0, n_tiles, tile, 0)

    @pl.when(n_tiles > 0)
    def _last_scatter():
        scatter_add((n_tiles - 1) * tm, n_sel)

    for k in range(N_SUPER):
        @pl.when(jnp.logical_and(sb == k, e == N_EXPERTS - 1))
        def _store_super_block(k=k):
            for c in piece_copies(k, False):
                c.start()
            if k == N_SUPER - 1:
                for c in piece_copies(k, False):
                    c.wait()


def _moe_experts(counts, offs, idx, gates, h_p, h_s, l, w_bf, s_tok, s_pad, l_pad):
    tm = TM_EXPERT
    w1, w3, w2 = w_bf
    w_spec = lambda a: pl.BlockSpec((None,) + a.shape[1:], lambda sb, e, cnt, off: (e, 0, 0))
    any_spec = pl.BlockSpec(memory_space=pl.ANY)
    rows = s_tok * SUBLANES
    pieces = _super_block_pieces(h_p.shape[0] // SUBLANES, h_s.shape[0] // SUBLANES, s_tok)
    return pl.pallas_call(
        functools.partial(_moe_expert_kernel, pieces=pieces, s_tok=s_tok, s_pad=s_pad,
                          l_pad=l_pad, tm=tm),
        grid_spec=pltpu.PrefetchScalarGridSpec(
            num_scalar_prefetch=2,
            grid=(N_SUPER, N_EXPERTS),
            in_specs=[any_spec] * 4 + [w_spec(w1), w_spec(w3), w_spec(w2)],
            out_specs=[any_spec, any_spec],
            scratch_shapes=[pltpu.VMEM((rows, LANES), F32), pltpu.VMEM((rows, LANES), F32),
                            pltpu.VMEM((tm * SUBLANES, LANES), F32),
                            pltpu.VMEM((tm * SUBLANES, LANES), F32),
                            pltpu.SMEM((l_pad,), jnp.int32), pltpu.SMEM((2 * s_pad,), F32),
                            pltpu.SemaphoreType.DMA((8,))]),
        out_shape=[jax.ShapeDtypeStruct(h_p.shape, F32), jax.ShapeDtypeStruct(h_s.shape, F32)],
        compiler_params=pltpu.CompilerParams(
            dimension_semantics=("arbitrary", "arbitrary"), vmem_limit_bytes=VMEM_LIMIT_EXPERT),
        name=f"moe_experts_{l}",
    )(counts, offs, idx, gates, h_p, h_s, w1, w3, w2)


def _moe_residual_kernel(x_ref, gate_ref, gfin_ref, y_ref, o_ref, *, rows_per_seq, tm, final_norm):
    (gate,) = _modulation([gate_ref], rows_per_seq, tm)
    y = x_ref[...] + gate * _from_token_tiles(y_ref, tm)
    o_ref[...] = _rms(y, gfin_ref[...]) if final_norm else y


def _moe_residual(x2d, mod, y, l, wts, rows_per_seq, final_norm):
    m, d = x2d.shape
    tm = min(TM_FFN, m)
    assert m % tm == 0
    x_spec = pl.BlockSpec((tm, d), lambda t: (t, 0))
    return pl.pallas_call(
        functools.partial(_moe_residual_kernel, rows_per_seq=rows_per_seq, tm=tm,
                          final_norm=final_norm),
        grid=(m // tm,),
        in_specs=[x_spec] + _mod_specs(mod, l, (5,), rows_per_seq, tm)
                 + [_const_spec(wts['g_final'].shape),
                    pl.BlockSpec((tm * SUBLANES, LANES), lambda t: (t, 0))],
        out_specs=x_spec,
        out_shape=jax.ShapeDtypeStruct((m, d), F32),
        compiler_params=pltpu.CompilerParams(
            dimension_semantics=("arbitrary",), vmem_limit_bytes=VMEM_LIMIT),
        name=f"moe_residual_{l}_{m}",
    )(x2d, mod, wts['g_final'], y)


def _ffn_moe(routed_p, xs, mod_s, l, wts, w_bf):
    h_p, meta_p = routed_p
    m_p, m_s = meta_p.shape[0], xs.shape[0]
    n_tok = m_p + m_s
    s_tok = n_tok // N_SUPER
    assert s_tok * N_SUPER == n_tok and s_tok % SUBLANES == 0
    s_pad = -(-s_tok // IDX_ALIGN) * IDX_ALIGN

    h_s, meta_s = _moe_route(xs, mod_s, l, wts, 1)

    n_meta = 2 * N_EXPERTS + 2
    meta = jnp.concatenate([meta_p[:, :n_meta], meta_s[:, :n_meta]], axis=0)
    per_seg = lambda a: a.reshape(N_SUPER, s_tok, N_EXPERTS).transpose(0, 2, 1)
    gate = per_seg(meta[:, :N_EXPERTS])
    counts = jnp.sum(per_seg(meta[:, N_EXPERTS:2 * N_EXPERTS]).astype(jnp.int32), axis=2)
    offs = jnp.cumsum(counts, axis=1) - counts
    chosen = meta[:, 2 * N_EXPERTS:].astype(jnp.int32).reshape(N_SUPER, s_tok, 2)
    tok = lax.broadcasted_iota(jnp.int32, chosen.shape, 1)
    tok_bits = (s_tok - 1).bit_length()
    keys = (chosen * (1 << tok_bits) + tok).reshape(N_SUPER, 2 * s_tok)
    idx = lax.sort(keys, dimension=1) & ((1 << tok_bits) - 1)
    l_pad = -(-(2 * s_tok + 2 * TM_EXPERT) // IDX_ALIGN) * IDX_ALIGN
    idx = jnp.pad(idx, ((0, 0), (0, l_pad - 2 * s_tok))).reshape(-1)
    gate = jnp.pad(gate, ((0, 0), (0, 0), (0, s_pad - s_tok))).reshape(-1)

    return _moe_experts(counts.reshape(-1), offs.reshape(-1), idx, gate, h_p, h_s, l, w_bf,
                        s_tok, s_pad, l_pad)


def kernel(x_prompt, x_sample, state_conv_a, state_conv_c, state_pool_d, c_prompt, c_sample,
           w_ada, b_ada, g_mix, w_in, conv_a_w, ln_v_g, ln_v_b, w_s, b_s, conv_c_w, conv_c_b,
           ln_c_g, ln_c_b, pool_w, pool_scale, g_out, w_out, g_ffn, w1_dense, w3_dense, w2_dense,
           w_router, b_router, w1_moe, w3_moe, w2_moe, g_final):
    p = dict(g_mix=g_mix, w_in=w_in, conv_a_w=conv_a_w, ln_v_g=ln_v_g, ln_v_b=ln_v_b, w_s=w_s,
             b_s=b_s, conv_c_w=conv_c_w, conv_c_b=conv_c_b, ln_c_g=ln_c_g, ln_c_b=ln_c_b,
             pool_w=pool_w, pool_scale=pool_scale, g_out=g_out, w_out=w_out, g_ffn=g_ffn,
             w1_dense=w1_dense, w3_dense=w3_dense, w2_dense=w2_dense, w_router=w_router,
             b_router=b_router, w1_moe=w1_moe, w3_moe=w3_moe, w2_moe=w2_moe, g_final=g_final)
    n_p, seq, d = x_prompt.shape
    n_s, dec_seq, _ = x_sample.shape
    assert dec_seq == 1 and d == D_MODEL

    mod_p, mod_s = _ada(c_prompt, c_sample, w_ada, b_ada)
    kmajor = lambda s: jnp.transpose(s, (0, 2, 1, 3))
    sa_t, sc_t, sd_t = kmajor(state_conv_a), kmajor(state_conv_c), kmajor(state_pool_d)
    mixw = _mix_params(p)
    pad_e = LANES - N_EXPERTS
    wts = dict(
        g_ffn=g_ffn.reshape(DEPTH, 1, d), g_final=g_final.reshape(1, d),
        w1_dense=w1_dense, w3_dense=w3_dense, w2_dense=w2_dense,
        w_router=jnp.pad(w_router, ((0, 0), (0, 0), (0, pad_e))),
        b_router=jnp.pad(b_router, ((0, 0), (0, pad_e))).reshape(-1, 1, LANES),
        w1_moe=w1_moe, w3_moe=w3_moe, w2_moe=w2_moe)

    xp = x_prompt
    xs = x_sample.reshape(n_s, d)
    states_p = [[], [], [], []]
    states_s = [[], [], [], []]
    pending = None
    moe_bf = None
    for l in range(DEPTH):
        last = l == DEPTH - 1
        moe = l % 2 == 1
        cast = [] if moe else [(wts[k], l // 2) for k in ('w1_dense', 'w3_dense', 'w2_dense')]
        st_p, dense_bf = _mix_prompt(xp, mod_p, l, mixw, 0, pending, wts if moe else None, cast)
        xp, *st_p = st_p
        pending = None
        xs, *st_s = _mix_sample(xs, mod_s, sa_t, sc_t, sd_t, l, mixw, PAST_LEN)
        for acc, s in zip(states_p, st_p[:4]):
            acc.append(s)
        for acc, s in zip(states_s, st_s):
            acc.append(s)
        if moe:
            if moe_bf is None:
                moe_bf = [wts[k][l // 2].astype(BF16) for k in ('w1_moe', 'w3_moe', 'w2_moe')]
            y_p, y_s = _ffn_moe(st_p[4:], xs, mod_s, l, wts, moe_bf)
            moe_bf = None
            xs = _moe_residual(xs, mod_s, y_s, l, wts, 1, last)
            if last:
                xp = _moe_residual(xp.reshape(n_p * seq, d), mod_p, y_p, l, wts, seq,
                                   True).reshape(n_p, seq, d)
            else:
                pending = (y_p, l)
        else:
            cast = ([(wts[k], (l + 1) // 2) for k in ('w1_moe', 'w3_moe', 'w2_moe')]
                    if l + 1 < DEPTH else [])
            xp2d, moe_bf = _ffn_dense(xp.reshape(n_p * seq, d), mod_p, l, wts, dense_bf, seq, last,
                                      cast)
            xp = xp2d.reshape(n_p, seq, d)
            xs, _ = _ffn_dense(xs, mod_s, l, wts, dense_bf, 1, last)
            moe_bf = moe_bf or None

    a_p, c_p, d_p, v_p = (jnp.stack(s) for s in states_p)
    a_s, c_s, d_s = (kmajor(jnp.stack(s)) for s in states_s[:3])
    v_s = jnp.stack(states_s[3]).reshape(DEPTH, n_s, 1, W_GROUP)
    return (xp, xs.reshape(n_s, 1, d), a_p, c_p, d_p, v_p, a_s, c_s, d_s, v_s)
```

```python
import functools

import jax
import jax.numpy as jnp
from jax import lax
from jax.experimental import pallas as pl
from jax.experimental.pallas import tpu as pltpu

D_MODEL = 1024
DEPTH = 4
W_GROUP = 256
CONV_A = 3
CHUNK = 128
N_HEADS_B = 4
HEAD_B = 64
CONV_C = 31
POOL_WINDOWS = (2, 4, 8, 16)
POOL_MAX = 16
GROUP_D = 64
IN_COLS = 2048
N_EXPERTS = 8
PAST_LEN = 16384
EPS = 1e-6

LANES = 128
LANE_TILES = W_GROUP // LANES
HIST_A = 8
HIST_C = 32
HIST_D = 32
TL_MIX = 512
ROW_CHUNK = 64
TM_FFN = 512
SUBLANES = 8
N_SUPER = 4
TM_EXPERT = 256
IDX_ALIGN = 1024
VMEM_LIMIT = 56 * 1024 * 1024
VMEM_LIMIT_DENSE = 60 * 1024 * 1024
VMEM_LIMIT_EXPERT = 62 * 1024 * 1024

F32 = jnp.float32
BF16 = jnp.bfloat16


def _dot(a, b):
    return jnp.dot(a, b, preferred_element_type=F32)


def _rms(x, g):
    return x * lax.rsqrt(jnp.mean(x * x, axis=-1, keepdims=True) + EPS) * g


def _ln(x, g, b):
    xc = x - jnp.mean(x, axis=-1, keepdims=True)
    var = jnp.mean(xc * xc, axis=-1, keepdims=True)
    return xc * lax.rsqrt(var + EPS) * g + b


def _silu(x):
    return x * jax.nn.sigmoid(x)


def _lane_group_select(vals, shape):
    lane = lax.broadcasted_iota(jnp.int32, shape, 1)
    out = vals[3]
    for g in (2, 1, 0):
        out = jnp.where(lane < (g + 1) * GROUP_D, vals[g], out)
    return out


ADA_CHUNKS = 2


def _ada_kernel(cp_ref, cs_ref, w_ref, b_ref, op_ref, os_ref):
    cp = _silu(cp_ref[...]).astype(BF16)
    cs = _silu(cs_ref[...]).astype(BF16)
    for k in range(ADA_CHUNKS):
        cols = slice(k * D_MODEL, (k + 1) * D_MODEL)
        w = w_ref[:, cols].astype(BF16)
        op_ref[k] = _dot(cp, w) + b_ref[:, cols]
        os_ref[k] = _dot(cs, w) + b_ref[:, cols]


def _ada(c_prompt, c_sample, w_ada, b_ada):
    n_p, n_s = c_prompt.shape[0], c_sample.shape[0]
    d = D_MODEL
    wide = ADA_CHUNKS * d
    return pl.pallas_call(
        _ada_kernel,
        grid=(DEPTH, 6 // ADA_CHUNKS),
        in_specs=[
            pl.BlockSpec((n_p, d), lambda l, k: (0, 0)),
            pl.BlockSpec((n_s, d), lambda l, k: (0, 0)),
            pl.BlockSpec((None, d, wide), lambda l, k: (l, 0, k)),
            pl.BlockSpec((None, 1, wide), lambda l, k: (l, 0, k)),
        ],
        out_specs=[
            pl.BlockSpec((None, ADA_CHUNKS, n_p, d), lambda l, k: (l, k, 0, 0)),
            pl.BlockSpec((None, ADA_CHUNKS, n_s, d), lambda l, k: (l, k, 0, 0)),
        ],
        out_shape=[
            jax.ShapeDtypeStruct((DEPTH, 6, n_p, d), F32),
            jax.ShapeDtypeStruct((DEPTH, 6, n_s, d), F32),
        ],
        compiler_params=pltpu.CompilerParams(
            dimension_semantics=("arbitrary", "arbitrary"), vmem_limit_bytes=VMEM_LIMIT),
        name="ada",
    )(c_prompt, c_sample, w_ada, b_ada.reshape(DEPTH, 1, 6 * d))


def _masked_ws(ws_ref):
    r = lax.broadcasted_iota(jnp.int32, (CHUNK, CHUNK), 0)
    c = lax.broadcasted_iota(jnp.int32, (CHUNK, CHUNK), 1)
    return [jnp.where(c <= r, ws_ref[h], 0.0).astype(BF16) for h in range(N_HEADS_B)]


def _ext_rows(ref, j, row0, n):
    return ref[pl.ds(row0 * LANE_TILES + j, n, stride=LANE_TILES), :]


def _ext_store(ref, row0, val):
    for j in range(LANE_TILES):
        ref[pl.ds(row0 * LANE_TILES + j, val.shape[0], stride=LANE_TILES), :] = (
            val[:, j * LANES:(j + 1) * LANES])


def _ext_load(ref, row0, n):
    return jnp.concatenate([_ext_rows(ref, j, row0, n) for j in range(LANE_TILES)], axis=1)


def _merge_and_project(x, gate, outs, gout_ref, wout_bf):
    merged = jnp.concatenate(
        [_rms(o, gout_ref[i:i + 1, :]) for i, o in enumerate(outs)], axis=1).astype(BF16)
    return x + gate * _dot(merged, wout_bf)


def _to_token_tiles(ref, val, tm):
    for k in range(SUBLANES):
        ref[pl.ds(k, tm, stride=SUBLANES), :] = val[:, k * LANES:(k + 1) * LANES]


def _from_token_tiles(ref, tm):
    return jnp.concatenate(
        [ref[pl.ds(k, tm, stride=SUBLANES), :] for k in range(SUBLANES)], axis=1)


def _router_meta(h32, wr_ref, br_ref):
    wr = wr_ref[...]
    h_hi = h32.astype(BF16)
    h_lo = (h32 - h_hi.astype(F32)).astype(BF16)
    w_hi = wr.astype(BF16)
    w_lo = (wr - w_hi.astype(F32)).astype(BF16)
    logits = _dot(h_hi, w_hi) + (_dot(h_lo, w_hi) + _dot(h_hi, w_lo)) + br_ref[...]
    lane = lax.broadcasted_iota(jnp.int32, logits.shape, 1)
    lane_f = lane.astype(F32)
    neg = jnp.float32(-jnp.inf)
    logits = jnp.where(lane < N_EXPERTS, logits, neg)
    m1 = jnp.max(logits, axis=-1, keepdims=True)
    i1 = jnp.min(jnp.where(logits == m1, lane_f, float(LANES)), axis=-1, keepdims=True)
    rest = jnp.where(lane_f == i1, neg, logits)
    m2 = jnp.max(rest, axis=-1, keepdims=True)
    i2 = jnp.min(jnp.where(rest == m2, lane_f, float(LANES)), axis=-1, keepdims=True)
    e = jnp.exp(m2 - m1)
    g1 = 1.0 / (1.0 + e)
    g2 = e / (1.0 + e)
    comb = jnp.where(lane_f == i1, g1, 0.0) + jnp.where(lane_f == i2, g2, 0.0)
    flags = (jnp.where(lane_f == i1 + N_EXPERTS, 1.0, 0.0)
             + jnp.where(lane_f == i2 + N_EXPERTS, 1.0, 0.0))
    chosen = (jnp.where(lane == 2 * N_EXPERTS, i1, 0.0)
              + jnp.where(lane == 2 * N_EXPERTS + 1, i2, 0.0))
    return comb + flags + chosen


def _mix_prompt_kernel(*refs, tl, start_pos, pre, post, n_cast):
    refs = list(refs)
    take = lambda n: [refs.pop(0) for _ in range(n)]
    (x_ref,) = take(1)
    y_ref, gprev_ref = take(2) if pre else (None, None)
    shift_ref, scale_ref, gate_ref = take(3)
    shift2_ref, scale2_ref = take(2) if post else (None, None)
    (gmix_ref, win_ref, wout_ref, convaw_ref, lnvg_ref, lnvb_ref, ws_ref, bsfull_ref, convcw_ref,
     convcb_ref, lncg_ref, lncb_ref, poolw_ref, poolscale_ref, gout_ref) = take(15)
    gffn_ref, wr_ref, br_ref = take(3) if post else (None, None, None)
    cast_src = take(n_cast)
    xo_ref, na_ref, nc_ref, nd_ref, nv_ref = take(5)
    h_ref, meta_ref = take(2) if post else (None, None)
    cast_dst = take(n_cast)
    win_bf, wout_bf, exta, extc, extd, ext2, ext4, ext8 = take(8)
    assert not refs
    for src, dst in zip(cast_src, cast_dst):
        dst[...] = src[...].astype(BF16)
    b = pl.program_id(0)
    t = pl.program_id(1)
    last_t = pl.num_programs(1) - 1

    @pl.when(jnp.logical_and(b == 0, t == 0))
    def _cast_weights():
        rows = 128
        def body(i, carry):
            r0 = pl.multiple_of(i * rows, rows)
            win_bf[pl.ds(r0, rows), :] = win_ref[pl.ds(r0, rows), :].astype(BF16)
            wout_bf[pl.ds(r0, rows), :] = wout_ref[pl.ds(r0, rows), :].astype(BF16)
            return carry
        lax.fori_loop(0, D_MODEL // rows, body, 0)

    @pl.when(t == 0)
    def _zero_history():
        for ref, hist in ((exta, HIST_A), (extc, HIST_C), (extd, HIST_D)):
            ref[0:hist * LANE_TILES, :] = jnp.zeros((hist * LANE_TILES, LANES), F32)

    x = x_ref[...]
    if pre:
        x = x + gprev_ref[pl.ds(b, 1), :] * _from_token_tiles(y_ref, tl)
    shift = shift_ref[pl.ds(b, 1), :]
    scale = scale_ref[pl.ds(b, 1), :]
    gate = gate_ref[pl.ds(b, 1), :]
    h = _rms(x, gmix_ref[...] * (1.0 + scale)) + shift
    proj = _dot(h.astype(BF16), win_bf[...])
    a_b, a_c, a_h, b_u, b_v, c_a, c_g, d_p = [
        proj[:, i * W_GROUP:(i + 1) * W_GROUP] for i in range(8)]

    def dwconv(ext_ref, w_ref, hist, width):
        off = hist - (width - 1)
        halves = []
        for j in range(LANE_TILES):
            chunks = []
            for c0 in range(0, tl, ROW_CHUNK):
                acc = None
                for k in range(width):
                    term = (_ext_rows(ext_ref, j, off + c0 + k, ROW_CHUNK)
                            * w_ref[k:k + 1, j * LANES:(j + 1) * LANES])
                    acc = term if acc is None else acc + term
                chunks.append(acc)
            halves.append(jnp.concatenate(chunks, axis=0))
        return jnp.concatenate(halves, axis=1)

    _ext_store(exta, HIST_A, a_c * a_h)
    out_a = a_b * dwconv(exta, convaw_ref, HIST_A, CONV_A)

    v_n = _ln(b_v, lnvg_ref[...], lnvb_ref[...])
    v_bf = v_n.astype(BF16)
    wm = _masked_ws(ws_ref)
    lane = lax.broadcasted_iota(jnp.int32, (CHUNK, W_GROUP), 1)
    mixed_chunks = []
    for j in range(tl // CHUNK):
        vc = v_bf[j * CHUNK:(j + 1) * CHUNK, :]
        mixed = _dot(wm[3], vc)
        for hd in (2, 1, 0):
            mixed = jnp.where(lane < (hd + 1) * HEAD_B, _dot(wm[hd], vc), mixed)
        mixed_chunks.append(mixed + bsfull_ref[...])
    out_b = b_u * jnp.concatenate(mixed_chunks, axis=0)

    _ext_store(extc, HIST_C, c_a * jax.nn.sigmoid(c_g))
    y_c = dwconv(extc, convcw_ref, HIST_C, CONV_C) + convcb_ref[...]
    out_c = _silu(_ln(y_c, lncg_ref[...], lncb_ref[...]))

    n = HIST_D + tl
    _ext_store(extd, HIST_D, d_p)
    bufs = (extd, ext2, ext4, ext8)
    pos1 = start_pos + 1 + t * tl + lax.broadcasted_iota(jnp.int32, (tl, LANES), 0)
    low_group = lax.broadcasted_iota(jnp.int32, (tl, LANES), 1) < GROUP_D
    means = []
    for j in range(LANE_TILES):
        levels = 2 * (j + 1)
        for lv in range(levels - 1):
            first = 8 * (lv + 1)
            bufs[lv + 1][pl.ds(first * LANE_TILES + j, n - first, stride=LANE_TILES), :] = (
                _ext_rows(bufs[lv], j, first, n - first)
                + _ext_rows(bufs[lv], j, first - (1 << lv), n - first))
        prev = bufs[levels - 1]
        s_lo = _ext_rows(prev, j, HIST_D, tl)
        s_hi = s_lo + _ext_rows(prev, j, HIST_D - (1 << (levels - 1)), tl)
        w_lo, w_hi = POOL_WINDOWS[2 * j], POOL_WINDOWS[2 * j + 1]
        cnt = jnp.where(low_group, jnp.minimum(pos1, w_lo), jnp.minimum(pos1, w_hi)).astype(F32)
        means.append(jnp.where(low_group, s_lo, s_hi) / cnt)
    pooled = jnp.concatenate(means, axis=1) - d_p
    out_d = _dot(pooled.astype(BF16), poolw_ref[...].astype(BF16)) * poolscale_ref[...]

    x_new = _merge_and_project(x, gate, [out_a, out_b, out_c, out_d], gout_ref, wout_bf[...])
    xo_ref[...] = x_new
    if post:
        h2 = (_rms(x_new, gffn_ref[...]) * (1.0 + scale2_ref[pl.ds(b, 1), :])
              + shift2_ref[pl.ds(b, 1), :])
        _to_token_tiles(h_ref, h2, tl)
        meta_ref[...] = _router_meta(h2, wr_ref, br_ref)

    @pl.when(t == last_t)
    def _emit_state():
        na_ref[...] = _ext_load(exta, HIST_A + tl - (CONV_A - 1), CONV_A - 1)
        nc_ref[...] = _ext_load(extc, HIST_C + tl - (CONV_C - 1), CONV_C - 1)
        nd_ref[...] = _ext_load(extd, HIST_D + tl - (POOL_MAX - 1), POOL_MAX - 1)
        nv_ref[...] = v_n[tl - CHUNK:tl, :]

    for ref, hist in ((exta, HIST_A), (extc, HIST_C), (extd, HIST_D)):
        ref[0:hist * LANE_TILES, :] = ref[tl * LANE_TILES:(tl + hist) * LANE_TILES, :]


def _mix_params(p):
    r3 = lambda a: a.reshape(DEPTH, 1, -1)
    eye = jnp.eye(4, dtype=F32)
    pool_bd = (eye[None, :, None, :, None] * p['pool_w'][:, :, :, None, :]).reshape(
        DEPTH, W_GROUP, W_GROUP)
    return dict(
        gmix=r3(p['g_mix']), w_in=p['w_in'], w_out=p['w_out'], conva=p['conv_a_w'],
        lnvg=r3(p['ln_v_g']), lnvb=r3(p['ln_v_b']), ws=p['w_s'],
        bsfull=jnp.repeat(jnp.swapaxes(p['b_s'], 1, 2), HEAD_B, axis=2),
        ws0=r3(jnp.repeat(p['w_s'][:, :, 0, 0], HEAD_B, axis=1)),
        bs0=r3(jnp.repeat(p['b_s'][:, :, 0], HEAD_B, axis=1)),
        convc=p['conv_c_w'], convcb=r3(p['conv_c_b']), lncg=r3(p['ln_c_g']), lncb=r3(p['ln_c_b']),
        poolw=pool_bd, poolscale=r3(p['pool_scale']), gout=p['g_out'])


def _layer_spec(a, l):
    nd = a.ndim - 1
    return pl.BlockSpec((None,) + a.shape[1:], lambda *_: (l,) + (0,) * nd,
                        pipeline_mode=pl.Buffered(1))


def _const_spec(shape):
    nd = len(shape)
    return pl.BlockSpec(shape, lambda *_: (0,) * nd, pipeline_mode=pl.Buffered(1))


BF16_ROWS = 16


def _cast_job(cast, steps, step_of):
    args, in_specs, out_specs, out_shapes = [], [], [], []
    for a, j in cast:
        groups = a.shape[1] if a.ndim == 4 else 1
        rows, cols = a.shape[-2:]
        assert steps % groups == 0
        blocks = steps // groups
        while rows % (blocks * BF16_ROWS):
            assert blocks % 2 == 0
            blocks //= 2
        rep = steps // (groups * blocks)

        def block_of(*g, rep=rep, blocks=blocks, grouped=a.ndim == 4):
            q = step_of(*g) // rep
            return (q // blocks, q % blocks, 0) if grouped else (q, 0)

        lead = (None,) * (a.ndim - 2)
        args.append(a)
        in_specs.append(pl.BlockSpec(lead + (rows // blocks, cols),
                                     lambda *g, j=j, f=block_of: (j,) + f(*g)))
        out_specs.append(pl.BlockSpec(lead[1:] + (rows // blocks, cols), block_of))
        out_shapes.append(jax.ShapeDtypeStruct(a.shape[1:], BF16))
    return args, in_specs, out_specs, out_shapes


def _mix_prompt(x, mod_p, l, w, start_pos, prev_moe=None, wts=None, cast=()):
    n_b, seq, d = x.shape
    tl = TL_MIX
    nt = seq // tl
    assert seq % tl == 0 and tl % CHUNK == 0 and seq >= CHUNK
    pre, post = prev_moe is not None, wts is not None
    mod_spec = lambda lay, k: pl.BlockSpec((None, None, n_b, d), lambda b, t: (lay, k, 0, 0))
    tiles_spec = lambda rows: pl.BlockSpec((tl * rows, LANES), lambda b, t: (b * nt + t, 0))
    weights = [w[k] for k in ('gmix', 'w_in', 'w_out', 'conva', 'lnvg', 'lnvb', 'ws', 'bsfull',
                              'convc', 'convcb', 'lncg', 'lncb', 'poolw', 'poolscale', 'gout')]
    state_spec = lambda r: pl.BlockSpec((None, r, W_GROUP), lambda b, t: (b, 0, 0))
    ext = lambda hist: pltpu.VMEM(((hist + tl) * LANE_TILES, LANES), F32)

    args, in_specs = [x], [pl.BlockSpec((None, tl, d), lambda b, t: (b, t, 0))]
    if pre:
        y_prev, l_prev = prev_moe
        args += [y_prev, mod_p]
        in_specs += [tiles_spec(SUBLANES), mod_spec(l_prev, 5)]
    args += [mod_p] * 3
    in_specs += [mod_spec(l, 0), mod_spec(l, 1), mod_spec(l, 2)]
    if post:
        args += [mod_p] * 2
        in_specs += [mod_spec(l, 3), mod_spec(l, 4)]
    args += weights
    in_specs += [_layer_spec(a, l) for a in weights]
    out_specs = [pl.BlockSpec((None, tl, d), lambda b, t: (b, t, 0)),
                 state_spec(CONV_A - 1), state_spec(CONV_C - 1), state_spec(POOL_MAX - 1),
                 state_spec(CHUNK)]
    out_shape = [jax.ShapeDtypeStruct(x.shape, F32),
                 jax.ShapeDtypeStruct((n_b, CONV_A - 1, W_GROUP), F32),
                 jax.ShapeDtypeStruct((n_b, CONV_C - 1, W_GROUP), F32),
                 jax.ShapeDtypeStruct((n_b, POOL_MAX - 1, W_GROUP), F32),
                 jax.ShapeDtypeStruct((n_b, CHUNK, W_GROUP), F32)]
    if post:
        i = l // 2
        args += [wts['g_ffn'], wts['w_router'], wts['b_router']]
        in_specs += [_layer_spec(wts['g_ffn'], l), _layer_spec(wts['w_router'], i),
                     _layer_spec(wts['b_router'], i)]
        out_specs += [tiles_spec(SUBLANES), tiles_spec(1)]
        out_shape += [jax.ShapeDtypeStruct((n_b * seq * SUBLANES, LANES), F32),
                      jax.ShapeDtypeStruct((n_b * seq, LANES), F32)]
    c_args, c_in, c_out, c_shapes = _cast_job(cast, n_b * nt, lambda b, t: b * nt + t)
    out = pl.pallas_call(
        functools.partial(_mix_prompt_kernel, tl=tl, start_pos=start_pos, pre=pre, post=post,
                          n_cast=len(cast)),
        grid=(n_b, nt),
        in_specs=in_specs + c_in,
        out_specs=out_specs + c_out,
        out_shape=out_shape + c_shapes,
        scratch_shapes=[pltpu.VMEM((d, IN_COLS), BF16), pltpu.VMEM((d, d), BF16),
                        ext(HIST_A), ext(HIST_C), ext(HIST_D), ext(HIST_D), ext(HIST_D),
                        ext(HIST_D)],
        compiler_params=pltpu.CompilerParams(
            dimension_semantics=("arbitrary", "arbitrary"), vmem_limit_bytes=VMEM_LIMIT),
        name=f"mix_prompt_{l}",
    )(*args, *c_args)
    n_out = len(out) - len(cast)
    return out[:n_out], [o.reshape(a.shape[1:]) for o, (a, _) in zip(out[n_out:], cast)]


def _mix_sample_kernel(x_ref, shift_ref, scale_ref, gate_ref, sa_ref, sc_ref, sd_ref, gmix_ref,
                       win_ref, wout_ref, convaw_ref, lnvg_ref, lnvb_ref, ws0_ref, bs0_ref,
                       convcw_ref, convcb_ref, lncg_ref, lncb_ref, poolw_ref, poolscale_ref,
                       gout_ref, xo_ref, na_ref, nc_ref, nd_ref, nv_ref, *, start_pos):
    x = x_ref[...]
    h = _rms(x, gmix_ref[...]) * (1.0 + scale_ref[...]) + shift_ref[...]
    proj = _dot(h.astype(BF16), win_ref[...].astype(BF16))
    a_b, a_c, a_h, b_u, b_v, c_a, c_g, d_p = [
        proj[:, i * W_GROUP:(i + 1) * W_GROUP] for i in range(8)]

    ch = a_c * a_h
    y_a = convaw_ref[CONV_A - 1:CONV_A, :] * ch
    for k in range(CONV_A - 1):
        y_a = y_a + convaw_ref[k:k + 1, :] * sa_ref[k]
    out_a = a_b * y_a
    for k in range(CONV_A - 2):
        na_ref[k] = sa_ref[k + 1]
    na_ref[CONV_A - 2] = ch

    v_n = _ln(b_v, lnvg_ref[...], lnvb_ref[...])
    out_b = b_u * (ws0_ref[...] * v_n + bs0_ref[...])
    nv_ref[...] = v_n

    glu = c_a * jax.nn.sigmoid(c_g)
    y_c = convcw_ref[CONV_C - 1:CONV_C, :] * glu + convcb_ref[...]
    for k in range(CONV_C - 1):
        y_c = y_c + convcw_ref[k:k + 1, :] * sc_ref[k]
    out_c = _silu(_ln(y_c, lncg_ref[...], lncb_ref[...]))
    for k in range(CONV_C - 2):
        nc_ref[k] = sc_ref[k + 1]
    nc_ref[CONV_C - 2] = glu

    hist = POOL_MAX - 1
    run = d_p
    taken = 0
    sums = []
    for w in POOL_WINDOWS:
        while taken < w - 1:
            run = run + sd_ref[hist - 1 - taken]
            taken += 1
        sums.append(run / float(min(start_pos + 1, w)))
    pooled = _lane_group_select(sums, d_p.shape) - d_p
    out_d = _dot(pooled.astype(BF16), poolw_ref[...].astype(BF16)) * poolscale_ref[...]
    for k in range(hist - 1):
        nd_ref[k] = sd_ref[k + 1]
    nd_ref[hist - 1] = d_p

    xo_ref[...] = _merge_and_project(x, gate_ref[...], [out_a, out_b, out_c, out_d], gout_ref,
                                     wout_ref[...].astype(BF16))


def _mix_sample(x, mod_s, sa_t, sc_t, sd_t, l, w, start_pos):
    n, d = x.shape
    assert start_pos + 1 >= POOL_MAX
    mod_spec = lambda k: pl.BlockSpec((None, None, n, d), lambda i: (l, k, 0, 0))
    st_spec = lambda r: pl.BlockSpec((None, r, n, W_GROUP), lambda i: (l, 0, 0, 0))
    weights = [w[k] for k in ('gmix', 'w_in', 'w_out', 'conva', 'lnvg', 'lnvb', 'ws0', 'bs0',
                              'convc', 'convcb', 'lncg', 'lncb', 'poolw', 'poolscale', 'gout')]
    full = lambda shape: pl.BlockSpec(shape, lambda i: (0,) * len(shape))
    return pl.pallas_call(
        functools.partial(_mix_sample_kernel, start_pos=start_pos),
        grid=(1,),
        in_specs=[full((n, d)), mod_spec(0), mod_spec(1), mod_spec(2),
                  st_spec(CONV_A - 1), st_spec(CONV_C - 1), st_spec(POOL_MAX - 1)]
                 + [_layer_spec(a, l) for a in weights],
        out_specs=[full((n, d)), full((CONV_A - 1, n, W_GROUP)), full((CONV_C - 1, n, W_GROUP)),
                   full((POOL_MAX - 1, n, W_GROUP)), full((n, W_GROUP))],
        out_shape=[jax.ShapeDtypeStruct((n, d), F32),
                   jax.ShapeDtypeStruct((CONV_A - 1, n, W_GROUP), F32),
                   jax.ShapeDtypeStruct((CONV_C - 1, n, W_GROUP), F32),
                   jax.ShapeDtypeStruct((POOL_MAX - 1, n, W_GROUP), F32),
                   jax.ShapeDtypeStruct((n, W_GROUP), F32)],
        compiler_params=pltpu.CompilerParams(
            dimension_semantics=("arbitrary",), vmem_limit_bytes=VMEM_LIMIT),
        name=f"mix_sample_{l}",
    )(x, mod_s, mod_s, mod_s, sa_t, sc_t, sd_t, *weights)


def _modulation(refs, rows_per_seq, tm):
    if rows_per_seq == 1:
        return [r[...] for r in refs]
    b = (pl.program_id(0) * tm) // rows_per_seq
    return [r[pl.ds(b, 1), :] for r in refs]


def _ffn_dense_kernel(x_ref, shift_ref, scale_ref, gate_ref, gffn_ref, gfin_ref, w1_ref, w3_ref,
                      w2_ref, *rest, rows_per_seq, tm, final_norm):
    n_cast = (len(rest) - 1) // 2
    o_ref = rest[n_cast]
    x = x_ref[...]
    shift, scale, gate = _modulation([shift_ref, scale_ref, gate_ref], rows_per_seq, tm)
    h = (_rms(x, gffn_ref[...]) * (1.0 + scale) + shift).astype(BF16)
    act = (_silu(_dot(h, w1_ref[...])) * _dot(h, w3_ref[...])).astype(BF16)
    y = x + gate * _dot(act, w2_ref[...])
    o_ref[...] = _rms(y, gfin_ref[...]) if final_norm else y
    for src, dst in zip(rest[:n_cast], rest[n_cast + 1:]):
        dst[...] = src[...].astype(BF16)


def _mod_specs(mod, l, ks, rows_per_seq, tm):
    d = mod.shape[-1]
    if rows_per_seq == 1:
        return [pl.BlockSpec((None, None, tm, d), lambda t, k=k: (l, k, t, 0)) for k in ks]
    n_seq = mod.shape[2]
    return [pl.BlockSpec((None, None, n_seq, d), lambda t, k=k: (l, k, 0, 0)) for k in ks]


def _ffn_dense(x2d, mod, l, wts, w_bf, rows_per_seq, final_norm, cast=()):
    m, d = x2d.shape
    tm = min(TM_FFN, m)
    assert m % tm == 0 and (rows_per_seq == 1 or rows_per_seq % tm == 0)
    steps = m // tm
    x_spec = pl.BlockSpec((tm, d), lambda t: (t, 0))
    c_args, c_in, c_out, c_shapes = _cast_job(cast, steps, lambda t: t)
    out = pl.pallas_call(
        functools.partial(_ffn_dense_kernel, rows_per_seq=rows_per_seq, tm=tm,
                          final_norm=final_norm),
        grid=(steps,),
        in_specs=[x_spec] + _mod_specs(mod, l, (3, 4, 5), rows_per_seq, tm)
                 + [_layer_spec(wts['g_ffn'], l), _const_spec(wts['g_final'].shape)]
                 + [_const_spec(a.shape) for a in w_bf] + c_in,
        out_specs=[x_spec] + c_out,
        out_shape=[jax.ShapeDtypeStruct((m, d), F32)] + c_shapes,
        compiler_params=pltpu.CompilerParams(
            dimension_semantics=("arbitrary",), vmem_limit_bytes=VMEM_LIMIT_DENSE),
        name=f"ffn_dense_{l}",
    )(x2d, mod, mod, mod, wts['g_ffn'], wts['g_final'], *w_bf, *c_args)
    return out[0], [o.reshape(a.shape[1:]) for o, (a, _) in zip(out[1:], cast)]


def _moe_route_kernel(x_ref, shift_ref, scale_ref, gffn_ref, wr_ref, br_ref, h_ref, meta_ref, *,
                      rows_per_seq, tm):
    shift, scale = _modulation([shift_ref, scale_ref], rows_per_seq, tm)
    h32 = _rms(x_ref[...], gffn_ref[...]) * (1.0 + scale) + shift
    _to_token_tiles(h_ref, h32, tm)
    meta_ref[...] = _router_meta(h32, wr_ref, br_ref)


def _moe_route(x2d, mod, l, wts, rows_per_seq):
    m, d = x2d.shape
    tm = min(TM_FFN, m)
    assert m % tm == 0 and (rows_per_seq == 1 or rows_per_seq % tm == 0)
    i = l // 2
    return pl.pallas_call(
        functools.partial(_moe_route_kernel, rows_per_seq=rows_per_seq, tm=tm),
        grid=(m // tm,),
        in_specs=[pl.BlockSpec((tm, d), lambda t: (t, 0))]
                 + _mod_specs(mod, l, (3, 4), rows_per_seq, tm)
                 + [_layer_spec(wts['g_ffn'], l), _layer_spec(wts['w_router'], i),
                    _layer_spec(wts['b_router'], i)],
        out_specs=[pl.BlockSpec((tm * SUBLANES, LANES), lambda t: (t, 0)),
                   pl.BlockSpec((tm, LANES), lambda t: (t, 0))],
        out_shape=[jax.ShapeDtypeStruct((m * SUBLANES, LANES), F32),
                   jax.ShapeDtypeStruct((m, LANES), F32)],
        compiler_params=pltpu.CompilerParams(
            dimension_semantics=("arbitrary",), vmem_limit_bytes=VMEM_LIMIT),
        name=f"moe_route_{l}_{m}",
    )(x2d, mod, mod, wts['g_ffn'], wts['w_router'], wts['b_router'])


def _super_block_pieces(m_p, m_s, s_tok):
    pieces = []
    for k in range(N_SUPER):
        lo, hi = k * s_tok, (k + 1) * s_tok
        ps = []
        if lo < m_p:
            ps.append((0, lo, 0, min(hi, m_p) - lo))
        if hi > m_p:
            s0 = max(lo, m_p)
            ps.append((1, s0 - m_p, s0 - lo, hi - s0))
        pieces.append(ps)
    return pieces


def _moe_expert_kernel(cnt_ref, off_ref, idx_hbm, g_hbm, hp_hbm, hs_hbm, w1_ref, w3_ref, w2_ref,
                       yp_hbm, ys_hbm, h_scr, y_scr, xbuf, obuf, idx_s, g_s, sem,
                       *, pieces, s_tok, s_pad, l_pad, tm):
    sb = pl.program_id(0)
    e = pl.program_id(1)
    seg = sb * N_EXPERTS + e
    rows = s_tok * SUBLANES

    def piece_copies(k, to_vmem):
        copies = []
        for j, (grp, src_tok, dst_tok, n) in enumerate(pieces[k]):
            hbm = ((hp_hbm, hs_hbm) if to_vmem else (yp_hbm, ys_hbm))[grp]
            hbm = hbm.at[pl.ds(src_tok * SUBLANES, n * SUBLANES)]
            if to_vmem:
                copies.append(pltpu.make_async_copy(
                    hbm, h_scr.at[pl.ds(dst_tok * SUBLANES, n * SUBLANES)], sem.at[j]))
            else:
                copies.append(pltpu.make_async_copy(
                    y_scr.at[pl.ds(dst_tok * SUBLANES, n * SUBLANES)], hbm, sem.at[2 + j]))
        return copies

    def gate_copy(s):
        slot = lax.rem(s, 2)
        src = pl.ds(pl.multiple_of(s * s_pad, IDX_ALIGN), s_pad)
        dst = pl.ds(pl.multiple_of(slot * s_pad, IDX_ALIGN), s_pad)
        return pltpu.make_async_copy(g_hbm.at[src], g_s.at[dst], sem.at[6 + slot])

    def list_copy():
        src = pl.ds(pl.multiple_of(sb * l_pad, IDX_ALIGN), l_pad)
        return pltpu.make_async_copy(idx_hbm.at[src], idx_s, sem.at[4])

    @pl.when(seg == 0)
    def _first_gates():
        gate_copy(seg).start()
        obuf[...] = jnp.zeros(obuf.shape, F32)

    @pl.when(seg + 1 < N_SUPER * N_EXPERTS)
    def _next_gates():
        gate_copy(seg + 1).start()

    for k in range(N_SUPER):
        @pl.when(jnp.logical_and(sb == k, e == 0))
        def _load_super_block(k=k):
            list_copy().start()
            for c in piece_copies(k, True):
                c.start()
            if k > 0:
                for c in piece_copies(k - 1, False):
                    c.wait()
            zrows = 256
            assert rows % zrows == 0
            def zero(i, carry):
                r0 = pl.multiple_of(i * zrows, zrows)
                y_scr[pl.ds(r0, zrows), :] = jnp.zeros((zrows, LANES), F32)
                return carry
            lax.fori_loop(0, rows // zrows, zero, 0)
            for c in piece_copies(k, True):
                c.wait()
            list_copy().wait()

    gate_copy(seg).wait()
    gts = lax.rem(seg, 2) * s_pad
    lst = off_ref[seg]

    def gather(base):
        for r in range(tm):
            t8 = pl.multiple_of(idx_s[lst + base + r] * SUBLANES, SUBLANES)
            xbuf[r * SUBLANES:(r + 1) * SUBLANES, :] = h_scr[pl.ds(t8, SUBLANES), :]

    def scatter_add(base, limit):
        for r0 in range(0, tm, SUBLANES):
            upd = []
            for r in range(r0, r0 + SUBLANES):
                tok = idx_s[lst + base + r]
                t8 = pl.multiple_of(tok * SUBLANES, SUBLANES)
                g = jnp.where(base + r < limit, g_s[gts + tok], 0.0)
                o = obuf[r * SUBLANES:(r + 1) * SUBLANES, :]
                upd.append((t8, y_scr[pl.ds(t8, SUBLANES), :] + g * o))
            for t8, v in upd:
                y_scr[pl.ds(t8, SUBLANES), :] = v

    n_sel = cnt_ref[seg]
    n_tiles = (n_sel + tm - 1) // tm

    def tile(i, carry):
        x = _from_token_tiles(xbuf, tm).astype(BF16)
        gather((i + 1) * tm)
        scatter_add(jnp.maximum(i - 1, 0) * tm, jnp.where(i > 0, n_sel, 0))
        act = (_silu(_dot(x, w1_ref[...])) * _dot(x, w3_ref[...])).astype(BF16)
        _to_token_tiles(obuf, _dot(act, w2_ref[...]), tm)
        return carry

    gather(0)
    lax.fori_loop(0, n_tiles, tile, 0)

    @pl.when(n_tiles > 0)
    def _last_scatter():
        scatter_add((n_tiles - 1) * tm, n_sel)

    for k in range(N_SUPER):
        @pl.when(jnp.logical_and(sb == k, e == N_EXPERTS - 1))
        def _store_super_block(k=k):
            for c in piece_copies(k, False):
                c.start()
            if k == N_SUPER - 1:
                for c in piece_copies(k, False):
                    c.wait()


def _moe_experts(counts, offs, idx, gates, h_p, h_s, l, w_bf, s_tok, s_pad, l_pad):
    tm = TM_EXPERT
    w1, w3, w2 = w_bf
    w_spec = lambda a: pl.BlockSpec((None,) + a.shape[1:], lambda sb, e, cnt, off: (e, 0, 0))
    any_spec = pl.BlockSpec(memory_space=pl.ANY)
    rows = s_tok * SUBLANES
    pieces = _super_block_pieces(h_p.shape[0] // SUBLANES, h_s.shape[0] // SUBLANES, s_tok)
    return pl.pallas_call(
        functools.partial(_moe_expert_kernel, pieces=pieces, s_tok=s_tok, s_pad=s_pad,
                          l_pad=l_pad, tm=tm),
        grid_spec=pltpu.PrefetchScalarGridSpec(
            num_scalar_prefetch=2,
            grid=(N_SUPER, N_EXPERTS),
            in_specs=[any_spec] * 4 + [w_spec(w1), w_spec(w3), w_spec(w2)],
            out_specs=[any_spec, any_spec],
            scratch_shapes=[pltpu.VMEM((rows, LANES), F32), pltpu.VMEM((rows, LANES), F32),
                            pltpu.VMEM((tm * SUBLANES, LANES), F32),
                            pltpu.VMEM((tm * SUBLANES, LANES), F32),
                            pltpu.SMEM((l_pad,), jnp.int32), pltpu.SMEM((2 * s_pad,), F32),
                            pltpu.SemaphoreType.DMA((8,))]),
        out_shape=[jax.ShapeDtypeStruct(h_p.shape, F32), jax.ShapeDtypeStruct(h_s.shape, F32)],
        compiler_params=pltpu.CompilerParams(
            dimension_semantics=("arbitrary", "arbitrary"), vmem_limit_bytes=VMEM_LIMIT_EXPERT),
        name=f"moe_experts_{l}",
    )(counts, offs, idx, gates, h_p, h_s, w1, w3, w2)


def _moe_residual_kernel(x_ref, gate_ref, gfin_ref, y_ref, o_ref, *, rows_per_seq, tm, final_norm):
    (gate,) = _modulation([gate_ref], rows_per_seq, tm)
    y = x_ref[...] + gate * _from_token_tiles(y_ref, tm)
    o_ref[...] = _rms(y, gfin_ref[...]) if final_norm else y


def _moe_residual(x2d, mod, y, l, wts, rows_per_seq, final_norm):
    m, d = x2d.shape
    tm = min(TM_FFN, m)
    assert m % tm == 0
    x_spec = pl.BlockSpec((tm, d), lambda t: (t, 0))
    return pl.pallas_call(
        functools.partial(_moe_residual_kernel, rows_per_seq=rows_per_seq, tm=tm,
                          final_norm=final_norm),
        grid=(m // tm,),
        in_specs=[x_spec] + _mod_specs(mod, l, (5,), rows_per_seq, tm)
                 + [_const_spec(wts['g_final'].shape),
                    pl.BlockSpec((tm * SUBLANES, LANES), lambda t: (t, 0))],
        out_specs=x_spec,
        out_shape=jax.ShapeDtypeStruct((m, d), F32),
        compiler_params=pltpu.CompilerParams(
            dimension_semantics=("arbitrary",), vmem_limit_bytes=VMEM_LIMIT),
        name=f"moe_residual_{l}_{m}",
    )(x2d, mod, wts['g_final'], y)


def _ffn_moe(routed_p, xs, mod_s, l, wts, w_bf):
    h_p, meta_p = routed_p
    m_p, m_s = meta_p.shape[0], xs.shape[0]
    n_tok = m_p + m_s
    s_tok = n_tok // N_SUPER
    assert s_tok * N_SUPER == n_tok and s_tok % SUBLANES == 0
    s_pad = -(-s_tok // IDX_ALIGN) * IDX_ALIGN

    h_s, meta_s = _moe_route(xs, mod_s, l, wts, 1)

    n_meta = 2 * N_EXPERTS + 2
    meta = jnp.concatenate([meta_p[:, :n_meta], meta_s[:, :n_meta]], axis=0)
    per_seg = lambda a: a.reshape(N_SUPER, s_tok, N_EXPERTS).transpose(0, 2, 1)
    gate = per_seg(meta[:, :N_EXPERTS])
    counts = jnp.sum(per_seg(meta[:, N_EXPERTS:2 * N_EXPERTS]).astype(jnp.int32), axis=2)
    offs = jnp.cumsum(counts, axis=1) - counts
    tok = lax.broadcasted_iota(jnp.int32, (N_SUPER, s_tok), 1)
    tok_bits = (s_tok - 1).bit_length()
    keys = jnp.concatenate(
        [meta[:, 2 * N_EXPERTS + k].astype(jnp.int32).reshape(N_SUPER, s_tok) * (1 << tok_bits) + tok
         for k in range(2)], axis=1)
    idx = lax.sort(keys, dimension=1) & ((1 << tok_bits) - 1)
    l_pad = -(-(2 * s_tok + 2 * TM_EXPERT) // IDX_ALIGN) * IDX_ALIGN
    idx = jnp.pad(idx, ((0, 0), (0, l_pad - 2 * s_tok))).reshape(-1)
    gate = jnp.pad(gate, ((0, 0), (0, 0), (0, s_pad - s_tok))).reshape(-1)

    return _moe_experts(counts.reshape(-1), offs.reshape(-1), idx, gate, h_p, h_s, l, w_bf,
                        s_tok, s_pad, l_pad)


def kernel(x_prompt, x_sample, state_conv_a, state_conv_c, state_pool_d, c_prompt, c_sample,
           w_ada, b_ada, g_mix, w_in, conv_a_w, ln_v_g, ln_v_b, w_s, b_s, conv_c_w, conv_c_b,
           ln_c_g, ln_c_b, pool_w, pool_scale, g_out, w_out, g_ffn, w1_dense, w3_dense, w2_dense,
           w_router, b_router, w1_moe, w3_moe, w2_moe, g_final):
    p = dict(g_mix=g_mix, w_in=w_in, conv_a_w=conv_a_w, ln_v_g=ln_v_g, ln_v_b=ln_v_b, w_s=w_s,
             b_s=b_s, conv_c_w=conv_c_w, conv_c_b=conv_c_b, ln_c_g=ln_c_g, ln_c_b=ln_c_b,
             pool_w=pool_w, pool_scale=pool_scale, g_out=g_out, w_out=w_out, g_ffn=g_ffn,
             w1_dense=w1_dense, w3_dense=w3_dense, w2_dense=w2_dense, w_router=w_router,
             b_router=b_router, w1_moe=w1_moe, w3_moe=w3_moe, w2_moe=w2_moe, g_final=g_final)
    n_p, seq, d = x_prompt.shape
    n_s, dec_seq, _ = x_sample.shape
    assert dec_seq == 1 and d == D_MODEL

    mod_p, mod_s = _ada(c_prompt, c_sample, w_ada, b_ada)
    kmajor = lambda s: jnp.transpose(s, (0, 2, 1, 3))
    sa_t, sc_t, sd_t = kmajor(state_conv_a), kmajor(state_conv_c), kmajor(state_pool_d)
    mixw = _mix_params(p)
    pad_e = LANES - N_EXPERTS
    wts = dict(
        g_ffn=g_ffn.reshape(DEPTH, 1, d), g_final=g_final.reshape(1, d),
        w1_dense=w1_dense, w3_dense=w3_dense, w2_dense=w2_dense,
        w_router=jnp.pad(w_router, ((0, 0), (0, 0), (0, pad_e))),
        b_router=jnp.pad(b_router, ((0, 0), (0, pad_e))).reshape(-1, 1, LANES),
        w1_moe=w1_moe, w3_moe=w3_moe, w2_moe=w2_moe)

    xp = x_prompt
    xs = x_sample.reshape(n_s, d)
    states_p = [[], [], [], []]
    states_s = [[], [], [], []]
    pending = None
    moe_bf = None
    for l in range(DEPTH):
        last = l == DEPTH - 1
        moe = l % 2 == 1
        cast = [] if moe else [(wts[k], l // 2) for k in ('w1_dense', 'w3_dense', 'w2_dense')]
        st_p, dense_bf = _mix_prompt(xp, mod_p, l, mixw, 0, pending, wts if moe else None, cast)
        xp, *st_p = st_p
        pending = None
        xs, *st_s = _mix_sample(xs, mod_s, sa_t, sc_t, sd_t, l, mixw, PAST_LEN)
        for acc, s in zip(states_p, st_p[:4]):
            acc.append(s)
        for acc, s in zip(states_s, st_s):
            acc.append(s)
        if moe:
            if moe_bf is None:
                moe_bf = [wts[k][l // 2].astype(BF16) for k in ('w1_moe', 'w3_moe', 'w2_moe')]
            y_p, y_s = _ffn_moe(st_p[4:], xs, mod_s, l, wts, moe_bf)
            moe_bf = None
            xs = _moe_residual(xs, mod_s, y_s, l, wts, 1, last)
            if last:
                xp = _moe_residual(xp.reshape(n_p * seq, d), mod_p, y_p, l, wts, seq,
                                   True).reshape(n_p, seq, d)
            else:
                pending = (y_p, l)
        else:
            cast = ([(wts[k], (l + 1) // 2) for k in ('w1_moe', 'w3_moe', 'w2_moe')]
                    if l + 1 < DEPTH else [])
            xp2d, moe_bf = _ffn_dense(xp.reshape(n_p * seq, d), mod_p, l, wts, dense_bf, seq, last,
                                      cast)
            xp = xp2d.reshape(n_p, seq, d)
            xs, _ = _ffn_dense(xs, mod_s, l, wts, dense_bf, 1, last)
            moe_bf = moe_bf or None

    a_p, c_p, d_p, v_p = (jnp.stack(s) for s in states_p)
    a_s, c_s, d_s = (kmajor(jnp.stack(s)) for s in states_s[:3])
    v_s = jnp.stack(states_s[3]).reshape(DEPTH, n_s, 1, W_GROUP)
    return (xp, xs.reshape(n_s, 1, d), a_p, c_p, d_p, v_p, a_s, c_s, d_s, v_s)
```

```python
import functools

import jax
import jax.numpy as jnp
from jax import lax
from jax.experimental import pallas as pl
from jax.experimental.pallas import tpu as pltpu

D_MODEL = 1024
DEPTH = 4
W_GROUP = 256
CONV_A = 3
CHUNK = 128
N_HEADS_B = 4
HEAD_B = 64
CONV_C = 31
POOL_WINDOWS = (2, 4, 8, 16)
POOL_MAX = 16
GROUP_D = 64
IN_COLS = 2048
N_EXPERTS = 8
PAST_LEN = 16384
EPS = 1e-6

LANES = 128
LANE_TILES = W_GROUP // LANES
HIST_A = 8
HIST_C = 32
HIST_D = 32
TL_MIX = 512
ROW_CHUNK = 64
TM_FFN = 512
SUBLANES = 8
N_SUPER = 4
TM_EXPERT = 256
IDX_ALIGN = 1024
VMEM_LIMIT = 56 * 1024 * 1024
VMEM_LIMIT_DENSE = 60 * 1024 * 1024
VMEM_LIMIT_EXPERT = 62 * 1024 * 1024

F32 = jnp.float32
BF16 = jnp.bfloat16


def _dot(a, b):
    return jnp.dot(a, b, preferred_element_type=F32)


def _rms(x, g):
    return x * lax.rsqrt(jnp.mean(x * x, axis=-1, keepdims=True) + EPS) * g


def _ln(x, g, b):
    xc = x - jnp.mean(x, axis=-1, keepdims=True)
    var = jnp.mean(xc * xc, axis=-1, keepdims=True)
    return xc * lax.rsqrt(var + EPS) * g + b


def _silu(x):
    return x * jax.nn.sigmoid(x)


def _lane_group_select(vals, shape):
    lane = lax.broadcasted_iota(jnp.int32, shape, 1)
    out = vals[3]
    for g in (2, 1, 0):
        out = jnp.where(lane < (g + 1) * GROUP_D, vals[g], out)
    return out


ADA_CHUNKS = 2


def _ada_kernel(cp_ref, cs_ref, w_ref, b_ref, op_ref, os_ref):
    cp = _silu(cp_ref[...]).astype(BF16)
    cs = _silu(cs_ref[...]).astype(BF16)
    for k in range(ADA_CHUNKS):
        cols = slice(k * D_MODEL, (k + 1) * D_MODEL)
        w = w_ref[:, cols].astype(BF16)
        op_ref[k] = _dot(cp, w) + b_ref[:, cols]
        os_ref[k] = _dot(cs, w) + b_ref[:, cols]


def _ada(c_prompt, c_sample, w_ada, b_ada):
    n_p, n_s = c_prompt.shape[0], c_sample.shape[0]
    d = D_MODEL
    wide = ADA_CHUNKS * d
    return pl.pallas_call(
        _ada_kernel,
        grid=(DEPTH, 6 // ADA_CHUNKS),
        in_specs=[
            pl.BlockSpec((n_p, d), lambda l, k: (0, 0)),
            pl.BlockSpec((n_s, d), lambda l, k: (0, 0)),
            pl.BlockSpec((None, d, wide), lambda l, k: (l, 0, k)),
            pl.BlockSpec((None, 1, wide), lambda l, k: (l, 0, k)),
        ],
        out_specs=[
            pl.BlockSpec((None, ADA_CHUNKS, n_p, d), lambda l, k: (l, k, 0, 0)),
            pl.BlockSpec((None, ADA_CHUNKS, n_s, d), lambda l, k: (l, k, 0, 0)),
        ],
        out_shape=[
            jax.ShapeDtypeStruct((DEPTH, 6, n_p, d), F32),
            jax.ShapeDtypeStruct((DEPTH, 6, n_s, d), F32),
        ],
        compiler_params=pltpu.CompilerParams(
            dimension_semantics=("arbitrary", "arbitrary"), vmem_limit_bytes=VMEM_LIMIT),
        name="ada",
    )(c_prompt, c_sample, w_ada, b_ada.reshape(DEPTH, 1, 6 * d))


def _masked_ws(ws_ref):
    r = lax.broadcasted_iota(jnp.int32, (CHUNK, CHUNK), 0)
    c = lax.broadcasted_iota(jnp.int32, (CHUNK, CHUNK), 1)
    return [jnp.where(c <= r, ws_ref[h], 0.0).astype(BF16) for h in range(N_HEADS_B)]


def _ext_rows(ref, j, row0, n):
    return ref[pl.ds(row0 * LANE_TILES + j, n, stride=LANE_TILES), :]


def _ext_store(ref, row0, val):
    for j in range(LANE_TILES):
        ref[pl.ds(row0 * LANE_TILES + j, val.shape[0], stride=LANE_TILES), :] = (
            val[:, j * LANES:(j + 1) * LANES])


def _ext_load(ref, row0, n):
    return jnp.concatenate([_ext_rows(ref, j, row0, n) for j in range(LANE_TILES)], axis=1)


def _merge_and_project(x, gate, outs, gout_ref, wout_bf):
    merged = jnp.concatenate(
        [_rms(o, gout_ref[i:i + 1, :]) for i, o in enumerate(outs)], axis=1).astype(BF16)
    return x + gate * _dot(merged, wout_bf)


def _to_token_tiles(ref, val, tm):
    for k in range(SUBLANES):
        ref[pl.ds(k, tm, stride=SUBLANES), :] = val[:, k * LANES:(k + 1) * LANES]


def _from_token_tiles(ref, tm):
    return jnp.concatenate(
        [ref[pl.ds(k, tm, stride=SUBLANES), :] for k in range(SUBLANES)], axis=1)


def _router_meta(h32, wr_ref, br_ref):
    wr = wr_ref[...]
    h_hi = h32.astype(BF16)
    h_lo = (h32 - h_hi.astype(F32)).astype(BF16)
    w_hi = wr.astype(BF16)
    w_lo = (wr - w_hi.astype(F32)).astype(BF16)
    logits = _dot(h_hi, w_hi) + (_dot(h_lo, w_hi) + _dot(h_hi, w_lo)) + br_ref[...]
    lane = lax.broadcasted_iota(jnp.int32, logits.shape, 1)
    lane_f = lane.astype(F32)
    neg = jnp.float32(-jnp.inf)
    logits = jnp.where(lane < N_EXPERTS, logits, neg)
    m1 = jnp.max(logits, axis=-1, keepdims=True)
    i1 = jnp.min(jnp.where(logits == m1, lane_f, float(LANES)), axis=-1, keepdims=True)
    rest = jnp.where(lane_f == i1, neg, logits)
    m2 = jnp.max(rest, axis=-1, keepdims=True)
    i2 = jnp.min(jnp.where(rest == m2, lane_f, float(LANES)), axis=-1, keepdims=True)
    e = jnp.exp(m2 - m1)
    g1 = 1.0 / (1.0 + e)
    g2 = e / (1.0 + e)
    comb = jnp.where(lane_f == i1, g1, 0.0) + jnp.where(lane_f == i2, g2, 0.0)
    flags = (jnp.where(lane_f == i1 + N_EXPERTS, 1.0, 0.0)
             + jnp.where(lane_f == i2 + N_EXPERTS, 1.0, 0.0))
    chosen = (jnp.where(lane == 2 * N_EXPERTS, i1, 0.0)
              + jnp.where(lane == 2 * N_EXPERTS + 1, i2, 0.0))
    return comb + flags + chosen


def _mix_prompt_kernel(*refs, tl, start_pos, pre, post, cast_groups):
    refs = list(refs)
    take = lambda n: [refs.pop(0) for _ in range(n)]
    (x_ref,) = take(1)
    y_ref, gprev_ref = take(2) if pre else (None, None)
    shift_ref, scale_ref, gate_ref = take(3)
    shift2_ref, scale2_ref = take(2) if post else (None, None)
    (gmix_ref, win_ref, wout_ref, convaw_ref, lnvg_ref, lnvb_ref, ws_ref, bsfull_ref, convcw_ref,
     convcb_ref, lncg_ref, lncb_ref, poolw_ref, poolscale_ref, gout_ref) = take(15)
    gffn_ref, wr_ref, br_ref = take(3) if post else (None, None, None)
    cast_src = take(sum(cast_groups))
    xo_ref, na_ref, nc_ref, nd_ref, nv_ref = take(5)
    h_ref, meta_ref = take(2) if post else (None, None)
    cast_dst = take(len(cast_groups))
    win_bf, wout_bf, exta, extc, extd, ext2, ext4, ext8 = take(8)
    assert not refs
    _run_cast(cast_src, cast_dst, cast_groups)
    b = pl.program_id(0)
    t = pl.program_id(1)
    last_t = pl.num_programs(1) - 1

    @pl.when(jnp.logical_and(b == 0, t == 0))
    def _cast_weights():
        rows = 128
        def body(i, carry):
            r0 = pl.multiple_of(i * rows, rows)
            win_bf[pl.ds(r0, rows), :] = win_ref[pl.ds(r0, rows), :].astype(BF16)
            wout_bf[pl.ds(r0, rows), :] = wout_ref[pl.ds(r0, rows), :].astype(BF16)
            return carry
        lax.fori_loop(0, D_MODEL // rows, body, 0)

    @pl.when(t == 0)
    def _zero_history():
        for ref, hist in ((exta, HIST_A), (extc, HIST_C), (extd, HIST_D)):
            ref[0:hist * LANE_TILES, :] = jnp.zeros((hist * LANE_TILES, LANES), F32)

    x = x_ref[...]
    if pre:
        x = x + gprev_ref[pl.ds(b, 1), :] * _from_token_tiles(y_ref, tl)
    shift = shift_ref[pl.ds(b, 1), :]
    scale = scale_ref[pl.ds(b, 1), :]
    gate = gate_ref[pl.ds(b, 1), :]
    h = _rms(x, gmix_ref[...] * (1.0 + scale)) + shift
    proj = _dot(h.astype(BF16), win_bf[...])
    a_b, a_c, a_h, b_u, b_v, c_a, c_g, d_p = [
        proj[:, i * W_GROUP:(i + 1) * W_GROUP] for i in range(8)]

    def dwconv(ext_ref, w_ref, hist, width):
        off = hist - (width - 1)
        halves = []
        for j in range(LANE_TILES):
            chunks = []
            for c0 in range(0, tl, ROW_CHUNK):
                acc = None
                for k in range(width):
                    term = (_ext_rows(ext_ref, j, off + c0 + k, ROW_CHUNK)
                            * w_ref[k:k + 1, j * LANES:(j + 1) * LANES])
                    acc = term if acc is None else acc + term
                chunks.append(acc)
            halves.append(jnp.concatenate(chunks, axis=0))
        return jnp.concatenate(halves, axis=1)

    _ext_store(exta, HIST_A, a_c * a_h)
    out_a = a_b * dwconv(exta, convaw_ref, HIST_A, CONV_A)

    v_n = _ln(b_v, lnvg_ref[...], lnvb_ref[...])
    v_bf = v_n.astype(BF16)
    wm = _masked_ws(ws_ref)
    lane = lax.broadcasted_iota(jnp.int32, (CHUNK, W_GROUP), 1)
    mixed_chunks = []
    for j in range(tl // CHUNK):
        vc = v_bf[j * CHUNK:(j + 1) * CHUNK, :]
        mixed = _dot(wm[3], vc)
        for hd in (2, 1, 0):
            mixed = jnp.where(lane < (hd + 1) * HEAD_B, _dot(wm[hd], vc), mixed)
        mixed_chunks.append(mixed + bsfull_ref[...])
    out_b = b_u * jnp.concatenate(mixed_chunks, axis=0)

    _ext_store(extc, HIST_C, c_a * jax.nn.sigmoid(c_g))
    y_c = dwconv(extc, convcw_ref, HIST_C, CONV_C) + convcb_ref[...]
    out_c = _silu(_ln(y_c, lncg_ref[...], lncb_ref[...]))

    n = HIST_D + tl
    _ext_store(extd, HIST_D, d_p)
    bufs = (extd, ext2, ext4, ext8)
    pos1 = start_pos + 1 + t * tl + lax.broadcasted_iota(jnp.int32, (tl, LANES), 0)
    low_group = lax.broadcasted_iota(jnp.int32, (tl, LANES), 1) < GROUP_D
    means = []
    for j in range(LANE_TILES):
        levels = 2 * (j + 1)
        for lv in range(levels - 1):
            first = 8 * (lv + 1)
            bufs[lv + 1][pl.ds(first * LANE_TILES + j, n - first, stride=LANE_TILES), :] = (
                _ext_rows(bufs[lv], j, first, n - first)
                + _ext_rows(bufs[lv], j, first - (1 << lv), n - first))
        prev = bufs[levels - 1]
        s_lo = _ext_rows(prev, j, HIST_D, tl)
        s_hi = s_lo + _ext_rows(prev, j, HIST_D - (1 << (levels - 1)), tl)
        w_lo, w_hi = POOL_WINDOWS[2 * j], POOL_WINDOWS[2 * j + 1]
        cnt = jnp.where(low_group, jnp.minimum(pos1, w_lo), jnp.minimum(pos1, w_hi)).astype(F32)
        means.append(jnp.where(low_group, s_lo, s_hi) / cnt)
    pooled = jnp.concatenate(means, axis=1) - d_p
    out_d = _dot(pooled.astype(BF16), poolw_ref[...].astype(BF16)) * poolscale_ref[...]

    x_new = _merge_and_project(x, gate, [out_a, out_b, out_c, out_d], gout_ref, wout_bf[...])
    xo_ref[...] = x_new
    if post:
        h2 = (_rms(x_new, gffn_ref[...]) * (1.0 + scale2_ref[pl.ds(b, 1), :])
              + shift2_ref[pl.ds(b, 1), :])
        _to_token_tiles(h_ref, h2, tl)
        meta_ref[...] = _router_meta(h2, wr_ref, br_ref)

    @pl.when(t == last_t)
    def _emit_state():
        na_ref[...] = _ext_load(exta, HIST_A + tl - (CONV_A - 1), CONV_A - 1)
        nc_ref[...] = _ext_load(extc, HIST_C + tl - (CONV_C - 1), CONV_C - 1)
        nd_ref[...] = _ext_load(extd, HIST_D + tl - (POOL_MAX - 1), POOL_MAX - 1)
        nv_ref[...] = v_n[tl - CHUNK:tl, :]

    for ref, hist in ((exta, HIST_A), (extc, HIST_C), (extd, HIST_D)):
        ref[0:hist * LANE_TILES, :] = ref[tl * LANE_TILES:(tl + hist) * LANE_TILES, :]


def _mix_params(p):
    r3 = lambda a: a.reshape(DEPTH, 1, -1)
    eye = jnp.eye(4, dtype=F32)
    pool_bd = (eye[None, :, None, :, None] * p['pool_w'][:, :, :, None, :]).reshape(
        DEPTH, W_GROUP, W_GROUP)
    return dict(
        gmix=r3(p['g_mix']), w_in=p['w_in'], w_out=p['w_out'], conva=p['conv_a_w'],
        lnvg=r3(p['ln_v_g']), lnvb=r3(p['ln_v_b']), ws=p['w_s'],
        bsfull=jnp.repeat(jnp.swapaxes(p['b_s'], 1, 2), HEAD_B, axis=2),
        ws0=r3(jnp.repeat(p['w_s'][:, :, 0, 0], HEAD_B, axis=1)),
        bs0=r3(jnp.repeat(p['b_s'][:, :, 0], HEAD_B, axis=1)),
        convc=p['conv_c_w'], convcb=r3(p['conv_c_b']), lncg=r3(p['ln_c_g']), lncb=r3(p['ln_c_b']),
        poolw=pool_bd, poolscale=r3(p['pool_scale']), gout=p['g_out'])


def _layer_spec(a, l):
    nd = a.ndim - 1
    return pl.BlockSpec((None,) + a.shape[1:], lambda *_: (l,) + (0,) * nd,
                        pipeline_mode=pl.Buffered(1))


def _const_spec(shape):
    nd = len(shape)
    return pl.BlockSpec(shape, lambda *_: (0,) * nd, pipeline_mode=pl.Buffered(1))


BF16_ROWS = 16


def _cast_job(cast, steps, step_of):
    args, in_specs, out_specs, out_shapes = [], [], [], []
    for arrays, j in cast:
        a = arrays[0]
        assert all(o.shape == a.shape for o in arrays) and a.shape[-1] % LANES == 0
        groups = a.shape[1] if a.ndim == 4 else 1
        rows, cols = a.shape[-2:]
        assert steps % groups == 0
        blocks = steps // groups
        while rows % (blocks * BF16_ROWS):
            assert blocks % 2 == 0
            blocks //= 2
        rep = steps // (groups * blocks)

        def block_of(*g, rep=rep, blocks=blocks, grouped=a.ndim == 4):
            q = step_of(*g) // rep
            return (q // blocks, q % blocks, 0) if grouped else (q, 0)

        lead = (None,) * (a.ndim - 2)
        wide = cols * len(arrays)
        for o in arrays:
            args.append(o)
            in_specs.append(pl.BlockSpec(lead + (rows // blocks, cols),
                                         lambda *g, j=j, f=block_of: (j,) + f(*g)))
        out_specs.append(pl.BlockSpec(lead[1:] + (rows // blocks, wide), block_of))
        out_shapes.append(jax.ShapeDtypeStruct(a.shape[1:-1] + (wide,), BF16))
    return args, in_specs, out_specs, out_shapes, [len(arrays) for arrays, _ in cast]


def _run_cast(srcs, dsts, group_sizes):
    srcs = list(srcs)
    for dst, n in zip(dsts, group_sizes):
        parts = [srcs.pop(0)[...].astype(BF16) for _ in range(n)]
        dst[...] = parts[0] if n == 1 else jnp.concatenate(parts, axis=-1)


def _mix_prompt(x, mod_p, l, w, start_pos, prev_moe=None, wts=None, cast=()):
    n_b, seq, d = x.shape
    tl = TL_MIX
    nt = seq // tl
    assert seq % tl == 0 and tl % CHUNK == 0 and seq >= CHUNK
    pre, post = prev_moe is not None, wts is not None
    mod_spec = lambda lay, k: pl.BlockSpec((None, None, n_b, d), lambda b, t: (lay, k, 0, 0))
    tiles_spec = lambda rows: pl.BlockSpec((tl * rows, LANES), lambda b, t: (b * nt + t, 0))
    weights = [w[k] for k in ('gmix', 'w_in', 'w_out', 'conva', 'lnvg', 'lnvb', 'ws', 'bsfull',
                              'convc', 'convcb', 'lncg', 'lncb', 'poolw', 'poolscale', 'gout')]
    state_spec = lambda r: pl.BlockSpec((None, r, W_GROUP), lambda b, t: (b, 0, 0))
    ext = lambda hist: pltpu.VMEM(((hist + tl) * LANE_TILES, LANES), F32)

    args, in_specs = [x], [pl.BlockSpec((None, tl, d), lambda b, t: (b, t, 0))]
    if pre:
        y_prev, l_prev = prev_moe
        args += [y_prev, mod_p]
        in_specs += [tiles_spec(SUBLANES), mod_spec(l_prev, 5)]
    args += [mod_p] * 3
    in_specs += [mod_spec(l, 0), mod_spec(l, 1), mod_spec(l, 2)]
    if post:
        args += [mod_p] * 2
        in_specs += [mod_spec(l, 3), mod_spec(l, 4)]
    args += weights
    in_specs += [_layer_spec(a, l) for a in weights]
    out_specs = [pl.BlockSpec((None, tl, d), lambda b, t: (b, t, 0)),
                 state_spec(CONV_A - 1), state_spec(CONV_C - 1), state_spec(POOL_MAX - 1),
                 state_spec(CHUNK)]
    out_shape = [jax.ShapeDtypeStruct(x.shape, F32),
                 jax.ShapeDtypeStruct((n_b, CONV_A - 1, W_GROUP), F32),
                 jax.ShapeDtypeStruct((n_b, CONV_C - 1, W_GROUP), F32),
                 jax.ShapeDtypeStruct((n_b, POOL_MAX - 1, W_GROUP), F32),
                 jax.ShapeDtypeStruct((n_b, CHUNK, W_GROUP), F32)]
    if post:
        i = l // 2
        args += [wts['g_ffn'], wts['w_router'], wts['b_router']]
        in_specs += [_layer_spec(wts['g_ffn'], l), _layer_spec(wts['w_router'], i),
                     _layer_spec(wts['b_router'], i)]
        out_specs += [tiles_spec(SUBLANES), tiles_spec(1)]
        out_shape += [jax.ShapeDtypeStruct((n_b * seq * SUBLANES, LANES), F32),
                      jax.ShapeDtypeStruct((n_b * seq, LANES), F32)]
    c_args, c_in, c_out, c_shapes, c_groups = _cast_job(cast, n_b * nt,
                                                        lambda b, t: b * nt + t)
    out = pl.pallas_call(
        functools.partial(_mix_prompt_kernel, tl=tl, start_pos=start_pos, pre=pre, post=post,
                          cast_groups=tuple(c_groups)),
        grid=(n_b, nt),
        in_specs=in_specs + c_in,
        out_specs=out_specs + c_out,
        out_shape=out_shape + c_shapes,
        scratch_shapes=[pltpu.VMEM((d, IN_COLS), BF16), pltpu.VMEM((d, d), BF16),
                        ext(HIST_A), ext(HIST_C), ext(HIST_D), ext(HIST_D), ext(HIST_D),
                        ext(HIST_D)],
        compiler_params=pltpu.CompilerParams(
            dimension_semantics=("arbitrary", "arbitrary"), vmem_limit_bytes=VMEM_LIMIT),
        name=f"mix_prompt_{l}",
    )(*args, *c_args)
    n_out = len(out) - len(cast)
    return out[:n_out], list(out[n_out:])


def _mix_sample_kernel(x_ref, shift_ref, scale_ref, gate_ref, sa_ref, sc_ref, sd_ref, gmix_ref,
                       win_ref, wout_ref, convaw_ref, lnvg_ref, lnvb_ref, ws0_ref, bs0_ref,
                       convcw_ref, convcb_ref, lncg_ref, lncb_ref, poolw_ref, poolscale_ref,
                       gout_ref, xo_ref, na_ref, nc_ref, nd_ref, nv_ref, *, start_pos):
    x = x_ref[...]
    h = _rms(x, gmix_ref[...]) * (1.0 + scale_ref[...]) + shift_ref[...]
    proj = _dot(h.astype(BF16), win_ref[...].astype(BF16))
    a_b, a_c, a_h, b_u, b_v, c_a, c_g, d_p = [
        proj[:, i * W_GROUP:(i + 1) * W_GROUP] for i in range(8)]

    ch = a_c * a_h
    y_a = convaw_ref[CONV_A - 1:CONV_A, :] * ch
    for k in range(CONV_A - 1):
        y_a = y_a + convaw_ref[k:k + 1, :] * sa_ref[k]
    out_a = a_b * y_a
    for k in range(CONV_A - 2):
        na_ref[k] = sa_ref[k + 1]
    na_ref[CONV_A - 2] = ch

    v_n = _ln(b_v, lnvg_ref[...], lnvb_ref[...])
    out_b = b_u * (ws0_ref[...] * v_n + bs0_ref[...])
    nv_ref[...] = v_n

    glu = c_a * jax.nn.sigmoid(c_g)
    y_c = convcw_ref[CONV_C - 1:CONV_C, :] * glu + convcb_ref[...]
    for k in range(CONV_C - 1):
        y_c = y_c + convcw_ref[k:k + 1, :] * sc_ref[k]
    out_c = _silu(_ln(y_c, lncg_ref[...], lncb_ref[...]))
    for k in range(CONV_C - 2):
        nc_ref[k] = sc_ref[k + 1]
    nc_ref[CONV_C - 2] = glu

    hist = POOL_MAX - 1
    run = d_p
    taken = 0
    sums = []
    for w in POOL_WINDOWS:
        while taken < w - 1:
            run = run + sd_ref[hist - 1 - taken]
            taken += 1
        sums.append(run / float(min(start_pos + 1, w)))
    pooled = _lane_group_select(sums, d_p.shape) - d_p
    out_d = _dot(pooled.astype(BF16), poolw_ref[...].astype(BF16)) * poolscale_ref[...]
    for k in range(hist - 1):
        nd_ref[k] = sd_ref[k + 1]
    nd_ref[hist - 1] = d_p

    xo_ref[...] = _merge_and_project(x, gate_ref[...], [out_a, out_b, out_c, out_d], gout_ref,
                                     wout_ref[...].astype(BF16))


def _mix_sample(x, mod_s, sa_t, sc_t, sd_t, l, w, start_pos):
    n, d = x.shape
    assert start_pos + 1 >= POOL_MAX
    mod_spec = lambda k: pl.BlockSpec((None, None, n, d), lambda i: (l, k, 0, 0))
    st_spec = lambda r: pl.BlockSpec((None, r, n, W_GROUP), lambda i: (l, 0, 0, 0))
    weights = [w[k] for k in ('gmix', 'w_in', 'w_out', 'conva', 'lnvg', 'lnvb', 'ws0', 'bs0',
                              'convc', 'convcb', 'lncg', 'lncb', 'poolw', 'poolscale', 'gout')]
    full = lambda shape: pl.BlockSpec(shape, lambda i: (0,) * len(shape))
    return pl.pallas_call(
        functools.partial(_mix_sample_kernel, start_pos=start_pos),
        grid=(1,),
        in_specs=[full((n, d)), mod_spec(0), mod_spec(1), mod_spec(2),
                  st_spec(CONV_A - 1), st_spec(CONV_C - 1), st_spec(POOL_MAX - 1)]
                 + [_layer_spec(a, l) for a in weights],
        out_specs=[full((n, d)), full((CONV_A - 1, n, W_GROUP)), full((CONV_C - 1, n, W_GROUP)),
                   full((POOL_MAX - 1, n, W_GROUP)), full((n, W_GROUP))],
        out_shape=[jax.ShapeDtypeStruct((n, d), F32),
                   jax.ShapeDtypeStruct((CONV_A - 1, n, W_GROUP), F32),
                   jax.ShapeDtypeStruct((CONV_C - 1, n, W_GROUP), F32),
                   jax.ShapeDtypeStruct((POOL_MAX - 1, n, W_GROUP), F32),
                   jax.ShapeDtypeStruct((n, W_GROUP), F32)],
        compiler_params=pltpu.CompilerParams(
            dimension_semantics=("arbitrary",), vmem_limit_bytes=VMEM_LIMIT),
        name=f"mix_sample_{l}",
    )(x, mod_s, mod_s, mod_s, sa_t, sc_t, sd_t, *weights)


def _modulation(refs, rows_per_seq, tm):
    if rows_per_seq == 1:
        return [r[...] for r in refs]
    b = (pl.program_id(0) * tm) // rows_per_seq
    return [r[pl.ds(b, 1), :] for r in refs]


def _ffn_dense_kernel(x_ref, shift_ref, scale_ref, gate_ref, gffn_ref, gfin_ref, w1_ref, w3_ref,
                      w2_ref, *rest, rows_per_seq, tm, final_norm, cast_groups):
    n_src = sum(cast_groups)
    o_ref = rest[n_src]
    x = x_ref[...]
    shift, scale, gate = _modulation([shift_ref, scale_ref, gate_ref], rows_per_seq, tm)
    h = (_rms(x, gffn_ref[...]) * (1.0 + scale) + shift).astype(BF16)
    act = (_silu(_dot(h, w1_ref[...])) * _dot(h, w3_ref[...])).astype(BF16)
    y = x + gate * _dot(act, w2_ref[...])
    o_ref[...] = _rms(y, gfin_ref[...]) if final_norm else y
    _run_cast(rest[:n_src], rest[n_src + 1:], cast_groups)


def _mod_specs(mod, l, ks, rows_per_seq, tm):
    d = mod.shape[-1]
    if rows_per_seq == 1:
        return [pl.BlockSpec((None, None, tm, d), lambda t, k=k: (l, k, t, 0)) for k in ks]
    n_seq = mod.shape[2]
    return [pl.BlockSpec((None, None, n_seq, d), lambda t, k=k: (l, k, 0, 0)) for k in ks]


def _ffn_dense(x2d, mod, l, wts, w_bf, rows_per_seq, final_norm, cast=()):
    m, d = x2d.shape
    tm = min(TM_FFN, m)
    assert m % tm == 0 and (rows_per_seq == 1 or rows_per_seq % tm == 0)
    steps = m // tm
    x_spec = pl.BlockSpec((tm, d), lambda t: (t, 0))
    c_args, c_in, c_out, c_shapes, c_groups = _cast_job(cast, steps, lambda t: t)
    out = pl.pallas_call(
        functools.partial(_ffn_dense_kernel, rows_per_seq=rows_per_seq, tm=tm,
                          final_norm=final_norm, cast_groups=tuple(c_groups)),
        grid=(steps,),
        in_specs=[x_spec] + _mod_specs(mod, l, (3, 4, 5), rows_per_seq, tm)
                 + [_layer_spec(wts['g_ffn'], l), _const_spec(wts['g_final'].shape)]
                 + [_const_spec(a.shape) for a in w_bf] + c_in,
        out_specs=[x_spec] + c_out,
        out_shape=[jax.ShapeDtypeStruct((m, d), F32)] + c_shapes,
        compiler_params=pltpu.CompilerParams(
            dimension_semantics=("arbitrary",), vmem_limit_bytes=VMEM_LIMIT_DENSE),
        name=f"ffn_dense_{l}",
    )(x2d, mod, mod, mod, wts['g_ffn'], wts['g_final'], *w_bf, *c_args)
    return out[0], list(out[1:])


def _moe_route_kernel(x_ref, shift_ref, scale_ref, gffn_ref, wr_ref, br_ref, h_ref, meta_ref, *,
                      rows_per_seq, tm):
    shift, scale = _modulation([shift_ref, scale_ref], rows_per_seq, tm)
    h32 = _rms(x_ref[...], gffn_ref[...]) * (1.0 + scale) + shift
    _to_token_tiles(h_ref, h32, tm)
    meta_ref[...] = _router_meta(h32, wr_ref, br_ref)


def _moe_route(x2d, mod, l, wts, rows_per_seq):
    m, d = x2d.shape
    tm = min(TM_FFN, m)
    assert m % tm == 0 and (rows_per_seq == 1 or rows_per_seq % tm == 0)
    i = l // 2
    return pl.pallas_call(
        functools.partial(_moe_route_kernel, rows_per_seq=rows_per_seq, tm=tm),
        grid=(m // tm,),
        in_specs=[pl.BlockSpec((tm, d), lambda t: (t, 0))]
                 + _mod_specs(mod, l, (3, 4), rows_per_seq, tm)
                 + [_layer_spec(wts['g_ffn'], l), _layer_spec(wts['w_router'], i),
                    _layer_spec(wts['b_router'], i)],
        out_specs=[pl.BlockSpec((tm * SUBLANES, LANES), lambda t: (t, 0)),
                   pl.BlockSpec((tm, LANES), lambda t: (t, 0))],
        out_shape=[jax.ShapeDtypeStruct((m * SUBLANES, LANES), F32),
                   jax.ShapeDtypeStruct((m, LANES), F32)],
        compiler_params=pltpu.CompilerParams(
            dimension_semantics=("arbitrary",), vmem_limit_bytes=VMEM_LIMIT),
        name=f"moe_route_{l}_{m}",
    )(x2d, mod, mod, wts['g_ffn'], wts['w_router'], wts['b_router'])


def _super_block_pieces(m_p, m_s, s_tok):
    pieces = []
    for k in range(N_SUPER):
        lo, hi = k * s_tok, (k + 1) * s_tok
        ps = []
        if lo < m_p:
            ps.append((0, lo, 0, min(hi, m_p) - lo))
        if hi > m_p:
            s0 = max(lo, m_p)
            ps.append((1, s0 - m_p, s0 - lo, hi - s0))
        pieces.append(ps)
    return pieces


def _moe_expert_kernel(cnt_ref, off_ref, idx_hbm, g_hbm, hp_hbm, hs_hbm, w13_ref, w2_ref,
                       yp_hbm, ys_hbm, h_scr, y_scr, xbuf, obuf, idx_s, g_s, sem,
                       *, pieces, s_tok, s_pad, l_pad, tm):
    sb = pl.program_id(0)
    e = pl.program_id(1)
    seg = sb * N_EXPERTS + e
    rows = s_tok * SUBLANES

    def piece_copies(k, to_vmem):
        copies = []
        for j, (grp, src_tok, dst_tok, n) in enumerate(pieces[k]):
            hbm = ((hp_hbm, hs_hbm) if to_vmem else (yp_hbm, ys_hbm))[grp]
            hbm = hbm.at[pl.ds(src_tok * SUBLANES, n * SUBLANES)]
            if to_vmem:
                copies.append(pltpu.make_async_copy(
                    hbm, h_scr.at[pl.ds(dst_tok * SUBLANES, n * SUBLANES)], sem.at[j]))
            else:
                copies.append(pltpu.make_async_copy(
                    y_scr.at[pl.ds(dst_tok * SUBLANES, n * SUBLANES)], hbm, sem.at[2 + j]))
        return copies

    def gate_copy(s):
        slot = lax.rem(s, 2)
        src = pl.ds(pl.multiple_of(s * s_pad, IDX_ALIGN), s_pad)
        dst = pl.ds(pl.multiple_of(slot * s_pad, IDX_ALIGN), s_pad)
        return pltpu.make_async_copy(g_hbm.at[src], g_s.at[dst], sem.at[6 + slot])

    def list_copy():
        src = pl.ds(pl.multiple_of(sb * l_pad, IDX_ALIGN), l_pad)
        return pltpu.make_async_copy(idx_hbm.at[src], idx_s, sem.at[4])

    @pl.when(seg == 0)
    def _first_gates():
        gate_copy(seg).start()
        obuf[...] = jnp.zeros(obuf.shape, F32)

    @pl.when(seg + 1 < N_SUPER * N_EXPERTS)
    def _next_gates():
        gate_copy(seg + 1).start()

    for k in range(N_SUPER):
        @pl.when(jnp.logical_and(sb == k, e == 0))
        def _load_super_block(k=k):
            list_copy().start()
            for c in piece_copies(k, True):
                c.start()
            if k > 0:
                for c in piece_copies(k - 1, False):
                    c.wait()
            zrows = 256
            assert rows % zrows == 0
            def zero(i, carry):
                r0 = pl.multiple_of(i * zrows, zrows)
                y_scr[pl.ds(r0, zrows), :] = jnp.zeros((zrows, LANES), F32)
                return carry
            lax.fori_loop(0, rows // zrows, zero, 0)
            for c in piece_copies(k, True):
                c.wait()
            list_copy().wait()

    gate_copy(seg).wait()
    gts = lax.rem(seg, 2) * s_pad
    lst = off_ref[seg]

    def gather(base):
        for r in range(tm):
            t8 = pl.multiple_of(idx_s[lst + base + r] * SUBLANES, SUBLANES)
            xbuf[r * SUBLANES:(r + 1) * SUBLANES, :] = h_scr[pl.ds(t8, SUBLANES), :]

    def scatter_add(base, limit):
        for r0 in range(0, tm, SUBLANES):
            upd = []
            for r in range(r0, r0 + SUBLANES):
                tok = idx_s[lst + base + r]
                t8 = pl.multiple_of(tok * SUBLANES, SUBLANES)
                g = jnp.where(base + r < limit, g_s[gts + tok], 0.0)
                o = obuf[r * SUBLANES:(r + 1) * SUBLANES, :]
                upd.append((t8, y_scr[pl.ds(t8, SUBLANES), :] + g * o))
            for t8, v in upd:
                y_scr[pl.ds(t8, SUBLANES), :] = v

    n_sel = cnt_ref[seg]
    n_tiles = (n_sel + tm - 1) // tm

    def tile(i, carry):
        x = _from_token_tiles(xbuf, tm).astype(BF16)
        gather((i + 1) * tm)
        scatter_add(jnp.maximum(i - 1, 0) * tm, jnp.where(i > 0, n_sel, 0))
        up = _dot(x, w13_ref[...])
        ff = up.shape[1] // 2
        act = (_silu(up[:, :ff]) * up[:, ff:]).astype(BF16)
        _to_token_tiles(obuf, _dot(act, w2_ref[...]), tm)
        return carry

    gather(0)
    lax.fori_loop(0, n_tiles, tile, 0)

    @pl.when(n_tiles > 0)
    def _last_scatter():
        scatter_add((n_tiles - 1) * tm, n_sel)

    for k in range(N_SUPER):
        @pl.when(jnp.logical_and(sb == k, e == N_EXPERTS - 1))
        def _store_super_block(k=k):
            for c in piece_copies(k, False):
                c.start()
            if k == N_SUPER - 1:
                for c in piece_copies(k, False):
                    c.wait()


def _moe_experts(counts, offs, idx, gates, h_p, h_s, l, w_bf, s_tok, s_pad, l_pad):
    tm = TM_EXPERT
    w13, w2 = w_bf
    w_spec = lambda a: pl.BlockSpec((None,) + a.shape[1:], lambda sb, e, cnt, off: (e, 0, 0))
    any_spec = pl.BlockSpec(memory_space=pl.ANY)
    rows = s_tok * SUBLANES
    pieces = _super_block_pieces(h_p.shape[0] // SUBLANES, h_s.shape[0] // SUBLANES, s_tok)
    return pl.pallas_call(
        functools.partial(_moe_expert_kernel, pieces=pieces, s_tok=s_tok, s_pad=s_pad,
                          l_pad=l_pad, tm=tm),
        grid_spec=pltpu.PrefetchScalarGridSpec(
            num_scalar_prefetch=2,
            grid=(N_SUPER, N_EXPERTS),
            in_specs=[any_spec] * 4 + [w_spec(w13), w_spec(w2)],
            out_specs=[any_spec, any_spec],
            scratch_shapes=[pltpu.VMEM((rows, LANES), F32), pltpu.VMEM((rows, LANES), F32),
                            pltpu.VMEM((tm * SUBLANES, LANES), F32),
                            pltpu.VMEM((tm * SUBLANES, LANES), F32),
                            pltpu.SMEM((l_pad,), jnp.int32), pltpu.SMEM((2 * s_pad,), F32),
                            pltpu.SemaphoreType.DMA((8,))]),
        out_shape=[jax.ShapeDtypeStruct(h_p.shape, F32), jax.ShapeDtypeStruct(h_s.shape, F32)],
        compiler_params=pltpu.CompilerParams(
            dimension_semantics=("arbitrary", "arbitrary"), vmem_limit_bytes=VMEM_LIMIT_EXPERT),
        name=f"moe_experts_{l}",
    )(counts, offs, idx, gates, h_p, h_s, w13, w2)


def _moe_residual_kernel(x_ref, gate_ref, gfin_ref, y_ref, o_ref, *, rows_per_seq, tm, final_norm):
    (gate,) = _modulation([gate_ref], rows_per_seq, tm)
    y = x_ref[...] + gate * _from_token_tiles(y_ref, tm)
    o_ref[...] = _rms(y, gfin_ref[...]) if final_norm else y


def _moe_residual(x2d, mod, y, l, wts, rows_per_seq, final_norm):
    m, d = x2d.shape
    tm = min(TM_FFN, m)
    assert m % tm == 0
    x_spec = pl.BlockSpec((tm, d), lambda t: (t, 0))
    return pl.pallas_call(
        functools.partial(_moe_residual_kernel, rows_per_seq=rows_per_seq, tm=tm,
                          final_norm=final_norm),
        grid=(m // tm,),
        in_specs=[x_spec] + _mod_specs(mod, l, (5,), rows_per_seq, tm)
                 + [_const_spec(wts['g_final'].shape),
                    pl.BlockSpec((tm * SUBLANES, LANES), lambda t: (t, 0))],
        out_specs=x_spec,
        out_shape=jax.ShapeDtypeStruct((m, d), F32),
        compiler_params=pltpu.CompilerParams(
            dimension_semantics=("arbitrary",), vmem_limit_bytes=VMEM_LIMIT),
        name=f"moe_residual_{l}_{m}",
    )(x2d, mod, wts['g_final'], y)


def _ffn_moe(routed_p, xs, mod_s, l, wts, w_bf):
    h_p, meta_p = routed_p
    m_p, m_s = meta_p.shape[0], xs.shape[0]
    n_tok = m_p + m_s
    s_tok = n_tok // N_SUPER
    assert s_tok * N_SUPER == n_tok and s_tok % SUBLANES == 0
    s_pad = -(-s_tok // IDX_ALIGN) * IDX_ALIGN

    h_s, meta_s = _moe_route(xs, mod_s, l, wts, 1)

    n_meta = 2 * N_EXPERTS + 2
    meta = jnp.concatenate([meta_p[:, :n_meta], meta_s[:, :n_meta]], axis=0)
    per_seg = lambda a: a.reshape(N_SUPER, s_tok, N_EXPERTS).transpose(0, 2, 1)
    gate = per_seg(meta[:, :N_EXPERTS])
    counts = jnp.sum(per_seg(meta[:, N_EXPERTS:2 * N_EXPERTS]).astype(jnp.int32), axis=2)
    offs = jnp.cumsum(counts, axis=1) - counts
    tok = lax.broadcasted_iota(jnp.int32, (N_SUPER, s_tok), 1)
    tok_bits = (s_tok - 1).bit_length()
    keys = jnp.concatenate(
        [meta[:, 2 * N_EXPERTS + k].astype(jnp.int32).reshape(N_SUPER, s_tok) * (1 << tok_bits) + tok
         for k in range(2)], axis=1)
    idx = lax.sort(keys, dimension=1) & ((1 << tok_bits) - 1)
    l_pad = -(-(2 * s_tok + 2 * TM_EXPERT) // IDX_ALIGN) * IDX_ALIGN
    idx = jnp.pad(idx, ((0, 0), (0, l_pad - 2 * s_tok))).reshape(-1)
    gate = jnp.pad(gate, ((0, 0), (0, 0), (0, s_pad - s_tok))).reshape(-1)

    return _moe_experts(counts.reshape(-1), offs.reshape(-1), idx, gate, h_p, h_s, l, w_bf,
                        s_tok, s_pad, l_pad)


def kernel(x_prompt, x_sample, state_conv_a, state_conv_c, state_pool_d, c_prompt, c_sample,
           w_ada, b_ada, g_mix, w_in, conv_a_w, ln_v_g, ln_v_b, w_s, b_s, conv_c_w, conv_c_b,
           ln_c_g, ln_c_b, pool_w, pool_scale, g_out, w_out, g_ffn, w1_dense, w3_dense, w2_dense,
           w_router, b_router, w1_moe, w3_moe, w2_moe, g_final):
    p = dict(g_mix=g_mix, w_in=w_in, conv_a_w=conv_a_w, ln_v_g=ln_v_g, ln_v_b=ln_v_b, w_s=w_s,
             b_s=b_s, conv_c_w=conv_c_w, conv_c_b=conv_c_b, ln_c_g=ln_c_g, ln_c_b=ln_c_b,
             pool_w=pool_w, pool_scale=pool_scale, g_out=g_out, w_out=w_out, g_ffn=g_ffn,
             w1_dense=w1_dense, w3_dense=w3_dense, w2_dense=w2_dense, w_router=w_router,
             b_router=b_router, w1_moe=w1_moe, w3_moe=w3_moe, w2_moe=w2_moe, g_final=g_final)
    n_p, seq, d = x_prompt.shape
    n_s, dec_seq, _ = x_sample.shape
    assert dec_seq == 1 and d == D_MODEL

    mod_p, mod_s = _ada(c_prompt, c_sample, w_ada, b_ada)
    kmajor = lambda s: jnp.transpose(s, (0, 2, 1, 3))
    sa_t, sc_t, sd_t = kmajor(state_conv_a), kmajor(state_conv_c), kmajor(state_pool_d)
    mixw = _mix_params(p)
    pad_e = LANES - N_EXPERTS
    wts = dict(
        g_ffn=g_ffn.reshape(DEPTH, 1, d), g_final=g_final.reshape(1, d),
        w1_dense=w1_dense, w3_dense=w3_dense, w2_dense=w2_dense,
        w_router=jnp.pad(w_router, ((0, 0), (0, 0), (0, pad_e))),
        b_router=jnp.pad(b_router, ((0, 0), (0, pad_e))).reshape(-1, 1, LANES),
        w1_moe=w1_moe, w3_moe=w3_moe, w2_moe=w2_moe)

    xp = x_prompt
    xs = x_sample.reshape(n_s, d)
    states_p = [[], [], [], []]
    states_s = [[], [], [], []]
    pending = None
    moe_bf = None
    for l in range(DEPTH):
        last = l == DEPTH - 1
        moe = l % 2 == 1
        cast = [] if moe else [((wts[k],), l // 2) for k in ('w1_dense', 'w3_dense', 'w2_dense')]
        st_p, dense_bf = _mix_prompt(xp, mod_p, l, mixw, 0, pending, wts if moe else None, cast)
        xp, *st_p = st_p
        pending = None
        xs, *st_s = _mix_sample(xs, mod_s, sa_t, sc_t, sd_t, l, mixw, PAST_LEN)
        for acc, s in zip(states_p, st_p[:4]):
            acc.append(s)
        for acc, s in zip(states_s, st_s):
            acc.append(s)
        if moe:
            if moe_bf is None:
                w1, w3, w2 = (wts[k][l // 2].astype(BF16) for k in ('w1_moe', 'w3_moe', 'w2_moe'))
                moe_bf = [jnp.concatenate([w1, w3], axis=-1), w2]
            y_p, y_s = _ffn_moe(st_p[4:], xs, mod_s, l, wts, moe_bf)
            moe_bf = None
            xs = _moe_residual(xs, mod_s, y_s, l, wts, 1, last)
            if last:
                xp = _moe_residual(xp.reshape(n_p * seq, d), mod_p, y_p, l, wts, seq,
                                   True).reshape(n_p, seq, d)
            else:
                pending = (y_p, l)
        else:
            nxt = (l + 1) // 2
            cast = ([((wts['w1_moe'], wts['w3_moe']), nxt), ((wts['w2_moe'],), nxt)]
                    if l + 1 < DEPTH else [])
            xp2d, moe_bf = _ffn_dense(xp.reshape(n_p * seq, d), mod_p, l, wts, dense_bf, seq, last,
                                      cast)
            xp = xp2d.reshape(n_p, seq, d)
            xs, _ = _ffn_dense(xs, mod_s, l, wts, dense_bf, 1, last)
            moe_bf = moe_bf or None

    a_p, c_p, d_p, v_p = (jnp.stack(s) for s in states_p)
    a_s, c_s, d_s = (kmajor(jnp.stack(s)) for s in states_s[:3])
    v_s = jnp.stack(states_s[3]).reshape(DEPTH, n_s, 1, W_GROUP)
    return (xp, xs.reshape(n_s, 1, d), a_p, c_p, d_p, v_p, a_s, c_s, d_s, v_s)
```

```python
import functools

import jax
import jax.numpy as jnp
from jax import lax
from jax.experimental import pallas as pl
from jax.experimental.pallas import tpu as pltpu

D_MODEL = 1024
DEPTH = 4
W_GROUP = 256
CONV_A = 3
CHUNK = 128
N_HEADS_B = 4
HEAD_B = 64
CONV_C = 31
POOL_WINDOWS = (2, 4, 8, 16)
POOL_MAX = 16
GROUP_D = 64
IN_COLS = 2048
N_EXPERTS = 8
PAST_LEN = 16384
EPS = 1e-6

LANES = 128
LANE_TILES = W_GROUP // LANES
HIST_A = 8
HIST_C = 32
HIST_D = 32
TL_MIX = 512
ROW_CHUNK = 64
TM_FFN = 512
SUBLANES = 8
N_SUPER = 4
TM_EXPERT = 256
IDX_ALIGN = 1024
VMEM_LIMIT = 56 * 1024 * 1024
VMEM_LIMIT_DENSE = 60 * 1024 * 1024
VMEM_LIMIT_EXPERT = 62 * 1024 * 1024

F32 = jnp.float32
BF16 = jnp.bfloat16


def _dot(a, b):
    return jnp.dot(a, b, preferred_element_type=F32)


def _rms(x, g):
    return x * lax.rsqrt(jnp.mean(x * x, axis=-1, keepdims=True) + EPS) * g


def _ln(x, g, b):
    xc = x - jnp.mean(x, axis=-1, keepdims=True)
    var = jnp.mean(xc * xc, axis=-1, keepdims=True)
    return xc * lax.rsqrt(var + EPS) * g + b


def _silu(x):
    return x * jax.nn.sigmoid(x)


def _lane_group_select(vals, shape):
    lane = lax.broadcasted_iota(jnp.int32, shape, 1)
    out = vals[3]
    for g in (2, 1, 0):
        out = jnp.where(lane < (g + 1) * GROUP_D, vals[g], out)
    return out


ADA_CHUNKS = 2


def _ada_kernel(cp_ref, cs_ref, w_ref, b_ref, op_ref, os_ref):
    cp = _silu(cp_ref[...]).astype(BF16)
    cs = _silu(cs_ref[...]).astype(BF16)
    for k in range(ADA_CHUNKS):
        cols = slice(k * D_MODEL, (k + 1) * D_MODEL)
        w = w_ref[:, cols].astype(BF16)
        op_ref[k] = _dot(cp, w) + b_ref[:, cols]
        os_ref[k] = _dot(cs, w) + b_ref[:, cols]


def _ada(c_prompt, c_sample, w_ada, b_ada):
    n_p, n_s = c_prompt.shape[0], c_sample.shape[0]
    d = D_MODEL
    wide = ADA_CHUNKS * d
    return pl.pallas_call(
        _ada_kernel,
        grid=(DEPTH, 6 // ADA_CHUNKS),
        in_specs=[
            pl.BlockSpec((n_p, d), lambda l, k: (0, 0)),
            pl.BlockSpec((n_s, d), lambda l, k: (0, 0)),
            pl.BlockSpec((None, d, wide), lambda l, k: (l, 0, k)),
            pl.BlockSpec((None, 1, wide), lambda l, k: (l, 0, k)),
        ],
        out_specs=[
            pl.BlockSpec((None, ADA_CHUNKS, n_p, d), lambda l, k: (l, k, 0, 0)),
            pl.BlockSpec((None, ADA_CHUNKS, n_s, d), lambda l, k: (l, k, 0, 0)),
        ],
        out_shape=[
            jax.ShapeDtypeStruct((DEPTH, 6, n_p, d), F32),
            jax.ShapeDtypeStruct((DEPTH, 6, n_s, d), F32),
        ],
        compiler_params=pltpu.CompilerParams(
            dimension_semantics=("arbitrary", "arbitrary"), vmem_limit_bytes=VMEM_LIMIT),
        name="ada",
    )(c_prompt, c_sample, w_ada, b_ada.reshape(DEPTH, 1, 6 * d))


def _masked_ws(ws_ref):
    r = lax.broadcasted_iota(jnp.int32, (CHUNK, CHUNK), 0)
    c = lax.broadcasted_iota(jnp.int32, (CHUNK, CHUNK), 1)
    return [jnp.where(c <= r, ws_ref[h], 0.0).astype(BF16) for h in range(N_HEADS_B)]


def _ext_rows(ref, j, row0, n):
    return ref[pl.ds(row0 * LANE_TILES + j, n, stride=LANE_TILES), :]


def _ext_store(ref, row0, val):
    for j in range(LANE_TILES):
        ref[pl.ds(row0 * LANE_TILES + j, val.shape[0], stride=LANE_TILES), :] = (
            val[:, j * LANES:(j + 1) * LANES])


def _ext_load(ref, row0, n):
    return jnp.concatenate([_ext_rows(ref, j, row0, n) for j in range(LANE_TILES)], axis=1)


def _merge_and_project(x, gate, outs, gout_ref, wout_bf):
    merged = jnp.concatenate(
        [_rms(o, gout_ref[i:i + 1, :]) for i, o in enumerate(outs)], axis=1).astype(BF16)
    return x + gate * _dot(merged, wout_bf)


def _to_token_tiles(ref, val, tm):
    for k in range(SUBLANES):
        ref[pl.ds(k, tm, stride=SUBLANES), :] = val[:, k * LANES:(k + 1) * LANES]


def _from_token_tiles(ref, tm):
    return jnp.concatenate(
        [ref[pl.ds(k, tm, stride=SUBLANES), :] for k in range(SUBLANES)], axis=1)


def _router_meta(h32, wr_ref, br_ref):
    wr = wr_ref[...]
    h_hi = h32.astype(BF16)
    h_lo = (h32 - h_hi.astype(F32)).astype(BF16)
    w_hi = wr.astype(BF16)
    w_lo = (wr - w_hi.astype(F32)).astype(BF16)
    logits = _dot(h_hi, w_hi) + (_dot(h_lo, w_hi) + _dot(h_hi, w_lo)) + br_ref[...]
    lane = lax.broadcasted_iota(jnp.int32, logits.shape, 1)
    lane_f = lane.astype(F32)
    neg = jnp.float32(-jnp.inf)
    logits = jnp.where(lane < N_EXPERTS, logits, neg)
    m1 = jnp.max(logits, axis=-1, keepdims=True)
    i1 = jnp.min(jnp.where(logits == m1, lane_f, float(LANES)), axis=-1, keepdims=True)
    rest = jnp.where(lane_f == i1, neg, logits)
    m2 = jnp.max(rest, axis=-1, keepdims=True)
    i2 = jnp.min(jnp.where(rest == m2, lane_f, float(LANES)), axis=-1, keepdims=True)
    e = jnp.exp(m2 - m1)
    g1 = 1.0 / (1.0 + e)
    g2 = e / (1.0 + e)
    comb = jnp.where(lane_f == i1, g1, 0.0) + jnp.where(lane_f == i2, g2, 0.0)
    flags = (jnp.where(lane_f == i1 + N_EXPERTS, 1.0, 0.0)
             + jnp.where(lane_f == i2 + N_EXPERTS, 1.0, 0.0))
    chosen = (jnp.where(lane == 2 * N_EXPERTS, i1, 0.0)
              + jnp.where(lane == 2 * N_EXPERTS + 1, i2, 0.0))
    return comb + flags + chosen


def _mix_prompt_kernel(*refs, tl, start_pos, pre, post, cast_groups):
    refs = list(refs)
    take = lambda n: [refs.pop(0) for _ in range(n)]
    (x_ref,) = take(1)
    y_ref, gprev_ref = take(2) if pre else (None, None)
    shift_ref, scale_ref, gate_ref = take(3)
    shift2_ref, scale2_ref = take(2) if post else (None, None)
    (gmix_ref, win_ref, wout_ref, convaw_ref, lnvg_ref, lnvb_ref, ws_ref, bsfull_ref, convcw_ref,
     convcb_ref, lncg_ref, lncb_ref, poolw_ref, poolscale_ref, gout_ref) = take(15)
    gffn_ref, wr_ref, br_ref = take(3) if post else (None, None, None)
    cast_src = take(sum(cast_groups))
    xo_ref, na_ref, nc_ref, nd_ref, nv_ref = take(5)
    h_ref, meta_ref = take(2) if post else (None, None)
    cast_dst = take(len(cast_groups))
    win_bf, wout_bf, exta, extc, extd, ext2, ext4, ext8 = take(8)
    assert not refs
    _run_cast(cast_src, cast_dst, cast_groups)
    b = pl.program_id(0)
    t = pl.program_id(1)
    last_t = pl.num_programs(1) - 1

    @pl.when(jnp.logical_and(b == 0, t == 0))
    def _cast_weights():
        rows = 128
        def body(i, carry):
            r0 = pl.multiple_of(i * rows, rows)
            win_bf[pl.ds(r0, rows), :] = win_ref[pl.ds(r0, rows), :].astype(BF16)
            wout_bf[pl.ds(r0, rows), :] = wout_ref[pl.ds(r0, rows), :].astype(BF16)
            return carry
        lax.fori_loop(0, D_MODEL // rows, body, 0)

    @pl.when(t == 0)
    def _zero_history():
        for ref, hist in ((exta, HIST_A), (extc, HIST_C), (extd, HIST_D)):
            ref[0:hist * LANE_TILES, :] = jnp.zeros((hist * LANE_TILES, LANES), F32)

    x = x_ref[...]
    if pre:
        x = x + gprev_ref[pl.ds(b, 1), :] * _from_token_tiles(y_ref, tl)
    shift = shift_ref[pl.ds(b, 1), :]
    scale = scale_ref[pl.ds(b, 1), :]
    gate = gate_ref[pl.ds(b, 1), :]
    h = _rms(x, gmix_ref[...] * (1.0 + scale)) + shift
    proj = _dot(h.astype(BF16), win_bf[...])
    a_b, a_c, a_h, b_u, b_v, c_a, c_g, d_p = [
        proj[:, i * W_GROUP:(i + 1) * W_GROUP] for i in range(8)]

    def dwconv(ext_ref, w_ref, hist, width):
        off = hist - (width - 1)
        halves = []
        for j in range(LANE_TILES):
            chunks = []
            for c0 in range(0, tl, ROW_CHUNK):
                acc = None
                for k in range(width):
                    term = (_ext_rows(ext_ref, j, off + c0 + k, ROW_CHUNK)
                            * w_ref[k:k + 1, j * LANES:(j + 1) * LANES])
                    acc = term if acc is None else acc + term
                chunks.append(acc)
            halves.append(jnp.concatenate(chunks, axis=0))
        return jnp.concatenate(halves, axis=1)

    _ext_store(exta, HIST_A, a_c * a_h)
    out_a = a_b * dwconv(exta, convaw_ref, HIST_A, CONV_A)

    v_n = _ln(b_v, lnvg_ref[...], lnvb_ref[...])
    v_bf = v_n.astype(BF16)
    wm = _masked_ws(ws_ref)
    lane = lax.broadcasted_iota(jnp.int32, (CHUNK, W_GROUP), 1)
    mixed_chunks = []
    for j in range(tl // CHUNK):
        vc = v_bf[j * CHUNK:(j + 1) * CHUNK, :]
        mixed = _dot(wm[3], vc)
        for hd in (2, 1, 0):
            mixed = jnp.where(lane < (hd + 1) * HEAD_B, _dot(wm[hd], vc), mixed)
        mixed_chunks.append(mixed + bsfull_ref[...])
    out_b = b_u * jnp.concatenate(mixed_chunks, axis=0)

    _ext_store(extc, HIST_C, c_a * jax.nn.sigmoid(c_g))
    y_c = dwconv(extc, convcw_ref, HIST_C, CONV_C) + convcb_ref[...]
    out_c = _silu(_ln(y_c, lncg_ref[...], lncb_ref[...]))

    n = HIST_D + tl
    _ext_store(extd, HIST_D, d_p)
    bufs = (extd, ext2, ext4, ext8)
    pos1 = start_pos + 1 + t * tl + lax.broadcasted_iota(jnp.int32, (tl, LANES), 0)
    low_group = lax.broadcasted_iota(jnp.int32, (tl, LANES), 1) < GROUP_D
    means = []
    for j in range(LANE_TILES):
        levels = 2 * (j + 1)
        for lv in range(levels - 1):
            first = 8 * (lv + 1)
            bufs[lv + 1][pl.ds(first * LANE_TILES + j, n - first, stride=LANE_TILES), :] = (
                _ext_rows(bufs[lv], j, first, n - first)
                + _ext_rows(bufs[lv], j, first - (1 << lv), n - first))
        prev = bufs[levels - 1]
        s_lo = _ext_rows(prev, j, HIST_D, tl)
        s_hi = s_lo + _ext_rows(prev, j, HIST_D - (1 << (levels - 1)), tl)
        w_lo, w_hi = POOL_WINDOWS[2 * j], POOL_WINDOWS[2 * j + 1]
        cnt = jnp.where(low_group, jnp.minimum(pos1, w_lo), jnp.minimum(pos1, w_hi)).astype(F32)
        means.append(jnp.where(low_group, s_lo, s_hi) / cnt)
    pooled = jnp.concatenate(means, axis=1) - d_p
    out_d = _dot(pooled.astype(BF16), poolw_ref[...].astype(BF16)) * poolscale_ref[...]

    x_new = _merge_and_project(x, gate, [out_a, out_b, out_c, out_d], gout_ref, wout_bf[...])
    xo_ref[...] = x_new
    if post:
        h2 = (_rms(x_new, gffn_ref[...]) * (1.0 + scale2_ref[pl.ds(b, 1), :])
              + shift2_ref[pl.ds(b, 1), :])
        _to_token_tiles(h_ref, h2, tl)
        meta_ref[...] = _router_meta(h2, wr_ref, br_ref)

    @pl.when(t == last_t)
    def _emit_state():
        na_ref[...] = _ext_load(exta, HIST_A + tl - (CONV_A - 1), CONV_A - 1)
        nc_ref[...] = _ext_load(extc, HIST_C + tl - (CONV_C - 1), CONV_C - 1)
        nd_ref[...] = _ext_load(extd, HIST_D + tl - (POOL_MAX - 1), POOL_MAX - 1)
        nv_ref[...] = v_n[tl - CHUNK:tl, :]

    for ref, hist in ((exta, HIST_A), (extc, HIST_C), (extd, HIST_D)):
        ref[0:hist * LANE_TILES, :] = ref[tl * LANE_TILES:(tl + hist) * LANE_TILES, :]


def _mix_params(p):
    r3 = lambda a: a.reshape(DEPTH, 1, -1)
    eye = jnp.eye(4, dtype=F32)
    pool_bd = (eye[None, :, None, :, None] * p['pool_w'][:, :, :, None, :]).reshape(
        DEPTH, W_GROUP, W_GROUP)
    return dict(
        gmix=r3(p['g_mix']), w_in=p['w_in'], w_out=p['w_out'], conva=p['conv_a_w'],
        lnvg=r3(p['ln_v_g']), lnvb=r3(p['ln_v_b']), ws=p['w_s'],
        bsfull=jnp.repeat(jnp.swapaxes(p['b_s'], 1, 2), HEAD_B, axis=2),
        ws0=r3(jnp.repeat(p['w_s'][:, :, 0, 0], HEAD_B, axis=1)),
        bs0=r3(jnp.repeat(p['b_s'][:, :, 0], HEAD_B, axis=1)),
        convc=p['conv_c_w'], convcb=r3(p['conv_c_b']), lncg=r3(p['ln_c_g']), lncb=r3(p['ln_c_b']),
        poolw=pool_bd, poolscale=r3(p['pool_scale']), gout=p['g_out'])


def _layer_spec(a, l):
    nd = a.ndim - 1
    return pl.BlockSpec((None,) + a.shape[1:], lambda *_: (l,) + (0,) * nd,
                        pipeline_mode=pl.Buffered(1))


def _const_spec(shape):
    nd = len(shape)
    return pl.BlockSpec(shape, lambda *_: (0,) * nd, pipeline_mode=pl.Buffered(1))


BF16_ROWS = 16


def _cast_job(cast, steps, step_of):
    args, in_specs, out_specs, out_shapes = [], [], [], []
    for arrays, j in cast:
        a = arrays[0]
        assert all(o.shape == a.shape for o in arrays) and a.shape[-1] % LANES == 0
        groups = a.shape[1] if a.ndim == 4 else 1
        rows, cols = a.shape[-2:]
        assert steps % groups == 0
        blocks = steps // groups
        while rows % (blocks * BF16_ROWS):
            assert blocks % 2 == 0
            blocks //= 2
        rep = steps // (groups * blocks)

        def block_of(*g, rep=rep, blocks=blocks, grouped=a.ndim == 4):
            q = step_of(*g) // rep
            return (q // blocks, q % blocks, 0) if grouped else (q, 0)

        lead = (None,) * (a.ndim - 2)
        wide = cols * len(arrays)
        for o in arrays:
            args.append(o)
            in_specs.append(pl.BlockSpec(lead + (rows // blocks, cols),
                                         lambda *g, j=j, f=block_of: (j,) + f(*g)))
        out_specs.append(pl.BlockSpec(lead[1:] + (rows // blocks, wide), block_of))
        out_shapes.append(jax.ShapeDtypeStruct(a.shape[1:-1] + (wide,), BF16))
    return args, in_specs, out_specs, out_shapes, [len(arrays) for arrays, _ in cast]


def _run_cast(srcs, dsts, group_sizes):
    srcs = list(srcs)
    for dst, n in zip(dsts, group_sizes):
        parts = [srcs.pop(0)[...].astype(BF16) for _ in range(n)]
        dst[...] = parts[0] if n == 1 else jnp.concatenate(parts, axis=-1)


def _mix_prompt(x, mod_p, l, w, start_pos, prev_moe=None, wts=None, cast=()):
    n_b, seq, d = x.shape
    tl = TL_MIX
    nt = seq // tl
    assert seq % tl == 0 and tl % CHUNK == 0 and seq >= CHUNK
    pre, post = prev_moe is not None, wts is not None
    mod_spec = lambda lay, k: pl.BlockSpec((None, None, n_b, d), lambda b, t: (lay, k, 0, 0))
    tiles_spec = lambda rows: pl.BlockSpec((tl * rows, LANES), lambda b, t: (b * nt + t, 0))
    weights = [w[k] for k in ('gmix', 'w_in', 'w_out', 'conva', 'lnvg', 'lnvb', 'ws', 'bsfull',
                              'convc', 'convcb', 'lncg', 'lncb', 'poolw', 'poolscale', 'gout')]
    state_spec = lambda r: pl.BlockSpec((None, r, W_GROUP), lambda b, t: (b, 0, 0))
    ext = lambda hist: pltpu.VMEM(((hist + tl) * LANE_TILES, LANES), F32)

    args, in_specs = [x], [pl.BlockSpec((None, tl, d), lambda b, t: (b, t, 0))]
    if pre:
        y_prev, l_prev = prev_moe
        args += [y_prev, mod_p]
        in_specs += [tiles_spec(SUBLANES), mod_spec(l_prev, 5)]
    args += [mod_p] * 3
    in_specs += [mod_spec(l, 0), mod_spec(l, 1), mod_spec(l, 2)]
    if post:
        args += [mod_p] * 2
        in_specs += [mod_spec(l, 3), mod_spec(l, 4)]
    args += weights
    in_specs += [_layer_spec(a, l) for a in weights]
    out_specs = [pl.BlockSpec((None, tl, d), lambda b, t: (b, t, 0)),
                 state_spec(CONV_A - 1), state_spec(CONV_C - 1), state_spec(POOL_MAX - 1),
                 state_spec(CHUNK)]
    out_shape = [jax.ShapeDtypeStruct(x.shape, F32),
                 jax.ShapeDtypeStruct((n_b, CONV_A - 1, W_GROUP), F32),
                 jax.ShapeDtypeStruct((n_b, CONV_C - 1, W_GROUP), F32),
                 jax.ShapeDtypeStruct((n_b, POOL_MAX - 1, W_GROUP), F32),
                 jax.ShapeDtypeStruct((n_b, CHUNK, W_GROUP), F32)]
    if post:
        i = l // 2
        args += [wts['g_ffn'], wts['w_router'], wts['b_router']]
        in_specs += [_layer_spec(wts['g_ffn'], l), _layer_spec(wts['w_router'], i),
                     _layer_spec(wts['b_router'], i)]
        out_specs += [tiles_spec(SUBLANES), tiles_spec(1)]
        out_shape += [jax.ShapeDtypeStruct((n_b * seq * SUBLANES, LANES), F32),
                      jax.ShapeDtypeStruct((n_b * seq, LANES), F32)]
    c_args, c_in, c_out, c_shapes, c_groups = _cast_job(cast, n_b * nt,
                                                        lambda b, t: b * nt + t)
    out = pl.pallas_call(
        functools.partial(_mix_prompt_kernel, tl=tl, start_pos=start_pos, pre=pre, post=post,
                          cast_groups=tuple(c_groups)),
        grid=(n_b, nt),
        in_specs=in_specs + c_in,
        out_specs=out_specs + c_out,
        out_shape=out_shape + c_shapes,
        scratch_shapes=[pltpu.VMEM((d, IN_COLS), BF16), pltpu.VMEM((d, d), BF16),
                        ext(HIST_A), ext(HIST_C), ext(HIST_D), ext(HIST_D), ext(HIST_D),
                        ext(HIST_D)],
        compiler_params=pltpu.CompilerParams(
            dimension_semantics=("arbitrary", "arbitrary"), vmem_limit_bytes=VMEM_LIMIT),
        name=f"mix_prompt_{l}",
    )(*args, *c_args)
    n_out = len(out) - len(cast)
    return out[:n_out], list(out[n_out:])


def _mix_sample_kernel(x_ref, shift_ref, scale_ref, gate_ref, sa_ref, sc_ref, sd_ref, gmix_ref,
                       win_ref, wout_ref, convaw_ref, lnvg_ref, lnvb_ref, ws0_ref, bs0_ref,
                       convcw_ref, convcb_ref, lncg_ref, lncb_ref, poolw_ref, poolscale_ref,
                       gout_ref, xo_ref, na_ref, nc_ref, nd_ref, nv_ref, *, start_pos):
    x = x_ref[...]
    h = _rms(x, gmix_ref[...]) * (1.0 + scale_ref[...]) + shift_ref[...]
    proj = _dot(h.astype(BF16), win_ref[...].astype(BF16))
    a_b, a_c, a_h, b_u, b_v, c_a, c_g, d_p = [
        proj[:, i * W_GROUP:(i + 1) * W_GROUP] for i in range(8)]

    ch = a_c * a_h
    y_a = convaw_ref[CONV_A - 1:CONV_A, :] * ch
    for k in range(CONV_A - 1):
        y_a = y_a + convaw_ref[k:k + 1, :] * sa_ref[k]
    out_a = a_b * y_a
    for k in range(CONV_A - 2):
        na_ref[k] = sa_ref[k + 1]
    na_ref[CONV_A - 2] = ch

    v_n = _ln(b_v, lnvg_ref[...], lnvb_ref[...])
    out_b = b_u * (ws0_ref[...] * v_n + bs0_ref[...])
    nv_ref[...] = v_n

    glu = c_a * jax.nn.sigmoid(c_g)
    y_c = convcw_ref[CONV_C - 1:CONV_C, :] * glu + convcb_ref[...]
    for k in range(CONV_C - 1):
        y_c = y_c + convcw_ref[k:k + 1, :] * sc_ref[k]
    out_c = _silu(_ln(y_c, lncg_ref[...], lncb_ref[...]))
    for k in range(CONV_C - 2):
        nc_ref[k] = sc_ref[k + 1]
    nc_ref[CONV_C - 2] = glu

    hist = POOL_MAX - 1
    run = d_p
    taken = 0
    sums = []
    for w in POOL_WINDOWS:
        while taken < w - 1:
            run = run + sd_ref[hist - 1 - taken]
            taken += 1
        sums.append(run / float(min(start_pos + 1, w)))
    pooled = _lane_group_select(sums, d_p.shape) - d_p
    out_d = _dot(pooled.astype(BF16), poolw_ref[...].astype(BF16)) * poolscale_ref[...]
    for k in range(hist - 1):
        nd_ref[k] = sd_ref[k + 1]
    nd_ref[hist - 1] = d_p

    xo_ref[...] = _merge_and_project(x, gate_ref[...], [out_a, out_b, out_c, out_d], gout_ref,
                                     wout_ref[...].astype(BF16))


def _mix_sample(x, mod_s, sa_t, sc_t, sd_t, l, w, start_pos):
    n, d = x.shape
    assert start_pos + 1 >= POOL_MAX
    mod_spec = lambda k: pl.BlockSpec((None, None, n, d), lambda i: (l, k, 0, 0))
    st_spec = lambda r: pl.BlockSpec((None, r, n, W_GROUP), lambda i: (l, 0, 0, 0))
    weights = [w[k] for k in ('gmix', 'w_in', 'w_out', 'conva', 'lnvg', 'lnvb', 'ws0', 'bs0',
                              'convc', 'convcb', 'lncg', 'lncb', 'poolw', 'poolscale', 'gout')]
    full = lambda shape: pl.BlockSpec(shape, lambda i: (0,) * len(shape))
    return pl.pallas_call(
        functools.partial(_mix_sample_kernel, start_pos=start_pos),
        grid=(1,),
        in_specs=[full((n, d)), mod_spec(0), mod_spec(1), mod_spec(2),
                  st_spec(CONV_A - 1), st_spec(CONV_C - 1), st_spec(POOL_MAX - 1)]
                 + [_layer_spec(a, l) for a in weights],
        out_specs=[full((n, d)), full((CONV_A - 1, n, W_GROUP)), full((CONV_C - 1, n, W_GROUP)),
                   full((POOL_MAX - 1, n, W_GROUP)), full((n, W_GROUP))],
        out_shape=[jax.ShapeDtypeStruct((n, d), F32),
                   jax.ShapeDtypeStruct((CONV_A - 1, n, W_GROUP), F32),
                   jax.ShapeDtypeStruct((CONV_C - 1, n, W_GROUP), F32),
                   jax.ShapeDtypeStruct((POOL_MAX - 1, n, W_GROUP), F32),
                   jax.ShapeDtypeStruct((n, W_GROUP), F32)],
        compiler_params=pltpu.CompilerParams(
            dimension_semantics=("arbitrary",), vmem_limit_bytes=VMEM_LIMIT),
        name=f"mix_sample_{l}",
    )(x, mod_s, mod_s, mod_s, sa_t, sc_t, sd_t, *weights)


def _modulation(refs, rows_per_seq, tm):
    if rows_per_seq == 1:
        return [r[...] for r in refs]
    b = (pl.program_id(0) * tm) // rows_per_seq
    return [r[pl.ds(b, 1), :] for r in refs]


def _ffn_dense_kernel(x_ref, shift_ref, scale_ref, gate_ref, gffn_ref, gfin_ref, w1_ref, w3_ref,
                      w2_ref, *rest, rows_per_seq, tm, final_norm, cast_groups):
    n_src = sum(cast_groups)
    o_ref = rest[n_src]
    x = x_ref[...]
    shift, scale, gate = _modulation([shift_ref, scale_ref, gate_ref], rows_per_seq, tm)
    h = (_rms(x, gffn_ref[...]) * (1.0 + scale) + shift).astype(BF16)
    act = (_silu(_dot(h, w1_ref[...])) * _dot(h, w3_ref[...])).astype(BF16)
    y = x + gate * _dot(act, w2_ref[...])
    o_ref[...] = _rms(y, gfin_ref[...]) if final_norm else y
    _run_cast(rest[:n_src], rest[n_src + 1:], cast_groups)


def _mod_specs(mod, l, ks, rows_per_seq, tm):
    d = mod.shape[-1]
    if rows_per_seq == 1:
        return [pl.BlockSpec((None, None, tm, d), lambda t, k=k: (l, k, t, 0)) for k in ks]
    n_seq = mod.shape[2]
    return [pl.BlockSpec((None, None, n_seq, d), lambda t, k=k: (l, k, 0, 0)) for k in ks]


def _ffn_dense(x2d, mod, l, wts, w_bf, rows_per_seq, final_norm, cast=()):
    m, d = x2d.shape
    tm = min(TM_FFN, m)
    assert m % tm == 0 and (rows_per_seq == 1 or rows_per_seq % tm == 0)
    steps = m // tm
    x_spec = pl.BlockSpec((tm, d), lambda t: (t, 0))
    c_args, c_in, c_out, c_shapes, c_groups = _cast_job(cast, steps, lambda t: t)
    out = pl.pallas_call(
        functools.partial(_ffn_dense_kernel, rows_per_seq=rows_per_seq, tm=tm,
                          final_norm=final_norm, cast_groups=tuple(c_groups)),
        grid=(steps,),
        in_specs=[x_spec] + _mod_specs(mod, l, (3, 4, 5), rows_per_seq, tm)
                 + [_layer_spec(wts['g_ffn'], l), _const_spec(wts['g_final'].shape)]
                 + [_const_spec(a.shape) for a in w_bf] + c_in,
        out_specs=[x_spec] + c_out,
        out_shape=[jax.ShapeDtypeStruct((m, d), F32)] + c_shapes,
        compiler_params=pltpu.CompilerParams(
            dimension_semantics=("arbitrary",), vmem_limit_bytes=VMEM_LIMIT_DENSE),
        name=f"ffn_dense_{l}",
    )(x2d, mod, mod, mod, wts['g_ffn'], wts['g_final'], *w_bf, *c_args)
    return out[0], list(out[1:])


def _moe_route_kernel(x_ref, shift_ref, scale_ref, gffn_ref, wr_ref, br_ref, h_ref, meta_ref, *,
                      rows_per_seq, tm):
    shift, scale = _modulation([shift_ref, scale_ref], rows_per_seq, tm)
    h32 = _rms(x_ref[...], gffn_ref[...]) * (1.0 + scale) + shift
    _to_token_tiles(h_ref, h32, tm)
    meta_ref[...] = _router_meta(h32, wr_ref, br_ref)


def _moe_route(x2d, mod, l, wts, rows_per_seq):
    m, d = x2d.shape
    tm = min(TM_FFN, m)
    assert m % tm == 0 and (rows_per_seq == 1 or rows_per_seq % tm == 0)
    i = l // 2
    return pl.pallas_call(
        functools.partial(_moe_route_kernel, rows_per_seq=rows_per_seq, tm=tm),
        grid=(m // tm,),
        in_specs=[pl.BlockSpec((tm, d), lambda t: (t, 0))]
                 + _mod_specs(mod, l, (3, 4), rows_per_seq, tm)
                 + [_layer_spec(wts['g_ffn'], l), _layer_spec(wts['w_router'], i),
                    _layer_spec(wts['b_router'], i)],
        out_specs=[pl.BlockSpec((tm * SUBLANES, LANES), lambda t: (t, 0)),
                   pl.BlockSpec((tm, LANES), lambda t: (t, 0))],
        out_shape=[jax.ShapeDtypeStruct((m * SUBLANES, LANES), F32),
                   jax.ShapeDtypeStruct((m, LANES), F32)],
        compiler_params=pltpu.CompilerParams(
            dimension_semantics=("arbitrary",), vmem_limit_bytes=VMEM_LIMIT),
        name=f"moe_route_{l}_{m}",
    )(x2d, mod, mod, wts['g_ffn'], wts['w_router'], wts['b_router'])


def _super_block_pieces(m_p, m_s, s_tok):
    pieces = []
    for k in range(N_SUPER):
        lo, hi = k * s_tok, (k + 1) * s_tok
        ps = []
        if lo < m_p:
            ps.append((0, lo, 0, min(hi, m_p) - lo))
        if hi > m_p:
            s0 = max(lo, m_p)
            ps.append((1, s0 - m_p, s0 - lo, hi - s0))
        pieces.append(ps)
    return pieces


def _moe_expert_kernel(cnt_ref, off_ref, idx_hbm, g_hbm, hp_hbm, hs_hbm, w13_ref, w2_ref,
                       yp_hbm, ys_hbm, h_scr, y_scr, xbuf, obuf, idx_s, g_s, sem,
                       *, pieces, s_tok, s_pad, l_pad, tm):
    sb = pl.program_id(0)
    e = pl.program_id(1)
    seg = sb * N_EXPERTS + e
    rows = s_tok * SUBLANES

    def piece_copies(k, to_vmem):
        copies = []
        for j, (grp, src_tok, dst_tok, n) in enumerate(pieces[k]):
            hbm = ((hp_hbm, hs_hbm) if to_vmem else (yp_hbm, ys_hbm))[grp]
            hbm = hbm.at[pl.ds(src_tok * SUBLANES, n * SUBLANES)]
            if to_vmem:
                copies.append(pltpu.make_async_copy(
                    hbm, h_scr.at[pl.ds(dst_tok * SUBLANES, n * SUBLANES)], sem.at[j]))
            else:
                copies.append(pltpu.make_async_copy(
                    y_scr.at[pl.ds(dst_tok * SUBLANES, n * SUBLANES)], hbm, sem.at[2 + j]))
        return copies

    def gate_copy(s):
        slot = lax.rem(s, 2)
        src = pl.ds(pl.multiple_of(s * s_pad, IDX_ALIGN), s_pad)
        dst = pl.ds(pl.multiple_of(slot * s_pad, IDX_ALIGN), s_pad)
        return pltpu.make_async_copy(g_hbm.at[src], g_s.at[dst], sem.at[6 + slot])

    def list_copy():
        src = pl.ds(pl.multiple_of(sb * l_pad, IDX_ALIGN), l_pad)
        return pltpu.make_async_copy(idx_hbm.at[src], idx_s, sem.at[4])

    @pl.when(seg == 0)
    def _first_gates():
        gate_copy(seg).start()
        obuf[...] = jnp.zeros(obuf.shape, F32)

    @pl.when(seg + 1 < N_SUPER * N_EXPERTS)
    def _next_gates():
        gate_copy(seg + 1).start()

    for k in range(N_SUPER):
        @pl.when(jnp.logical_and(sb == k, e == 0))
        def _load_super_block(k=k):
            list_copy().start()
            for c in piece_copies(k, True):
                c.start()
            if k > 0:
                for c in piece_copies(k - 1, False):
                    c.wait()
            zrows = 256
            assert rows % zrows == 0
            def zero(i, carry):
                r0 = pl.multiple_of(i * zrows, zrows)
                y_scr[pl.ds(r0, zrows), :] = jnp.zeros((zrows, LANES), F32)
                return carry
            lax.fori_loop(0, rows // zrows, zero, 0)
            for c in piece_copies(k, True):
                c.wait()
            list_copy().wait()

    gate_copy(seg).wait()
    gts = lax.rem(seg, 2) * s_pad
    lst = off_ref[seg]

    def gather(base):
        for r in range(tm):
            t8 = pl.multiple_of(idx_s[lst + base + r] * SUBLANES, SUBLANES)
            xbuf[r * SUBLANES:(r + 1) * SUBLANES, :] = h_scr[pl.ds(t8, SUBLANES), :]

    def scatter_add(base, limit):
        for r0 in range(0, tm, SUBLANES):
            upd = []
            for r in range(r0, r0 + SUBLANES):
                tok = idx_s[lst + base + r]
                t8 = pl.multiple_of(tok * SUBLANES, SUBLANES)
                g = jnp.where(base + r < limit, g_s[gts + tok], 0.0)
                o = obuf[r * SUBLANES:(r + 1) * SUBLANES, :]
                upd.append((t8, y_scr[pl.ds(t8, SUBLANES), :] + g * o))
            for t8, v in upd:
                y_scr[pl.ds(t8, SUBLANES), :] = v

    n_sel = cnt_ref[seg]
    n_tiles = (n_sel + tm - 1) // tm

    def tile(i, carry):
        x = _from_token_tiles(xbuf, tm).astype(BF16)
        gather((i + 1) * tm)
        scatter_add(jnp.maximum(i - 1, 0) * tm, jnp.where(i > 0, n_sel, 0))
        up = _dot(x, w13_ref[...])
        ff = up.shape[1] // 2
        act = (_silu(up[:, :ff]) * up[:, ff:]).astype(BF16)
        _to_token_tiles(obuf, _dot(act, w2_ref[...]), tm)
        return carry

    gather(0)
    lax.fori_loop(0, n_tiles, tile, 0)

    @pl.when(n_tiles > 0)
    def _last_scatter():
        scatter_add((n_tiles - 1) * tm, n_sel)

    for k in range(N_SUPER):
        @pl.when(jnp.logical_and(sb == k, e == N_EXPERTS - 1))
        def _store_super_block(k=k):
            for c in piece_copies(k, False):
                c.start()
            if k == N_SUPER - 1:
                for c in piece_copies(k, False):
                    c.wait()


def _moe_experts(counts, offs, idx, gates, h_p, h_s, l, w_bf, s_tok, s_pad, l_pad):
    tm = TM_EXPERT
    w13, w2 = w_bf
    w_spec = lambda a: pl.BlockSpec((None,) + a.shape[1:], lambda sb, e, cnt, off: (e, 0, 0))
    any_spec = pl.BlockSpec(memory_space=pl.ANY)
    rows = s_tok * SUBLANES
    pieces = _super_block_pieces(h_p.shape[0] // SUBLANES, h_s.shape[0] // SUBLANES, s_tok)
    return pl.pallas_call(
        functools.partial(_moe_expert_kernel, pieces=pieces, s_tok=s_tok, s_pad=s_pad,
                          l_pad=l_pad, tm=tm),
        grid_spec=pltpu.PrefetchScalarGridSpec(
            num_scalar_prefetch=2,
            grid=(N_SUPER, N_EXPERTS),
            in_specs=[any_spec] * 4 + [w_spec(w13), w_spec(w2)],
            out_specs=[any_spec, any_spec],
            scratch_shapes=[pltpu.VMEM((rows, LANES), F32), pltpu.VMEM((rows, LANES), F32),
                            pltpu.VMEM((tm * SUBLANES, LANES), F32),
                            pltpu.VMEM((tm * SUBLANES, LANES), F32),
                            pltpu.SMEM((l_pad,), jnp.int32), pltpu.SMEM((2 * s_pad,), F32),
                            pltpu.SemaphoreType.DMA((8,))]),
        out_shape=[jax.ShapeDtypeStruct(h_p.shape, F32), jax.ShapeDtypeStruct(h_s.shape, F32)],
        compiler_params=pltpu.CompilerParams(
            dimension_semantics=("arbitrary", "arbitrary"), vmem_limit_bytes=VMEM_LIMIT_EXPERT),
        name=f"moe_experts_{l}",
    )(counts, offs, idx, gates, h_p, h_s, w13, w2)


def _moe_residual_kernel(x_ref, gate_ref, gfin_ref, y_ref, o_ref, *, rows_per_seq, tm, final_norm):
    (gate,) = _modulation([gate_ref], rows_per_seq, tm)
    y = x_ref[...] + gate * _from_token_tiles(y_ref, tm)
    o_ref[...] = _rms(y, gfin_ref[...]) if final_norm else y


def _moe_residual(x2d, mod, y, l, wts, rows_per_seq, final_norm):
    m, d = x2d.shape
    tm = min(TM_FFN, m)
    assert m % tm == 0
    x_spec = pl.BlockSpec((tm, d), lambda t: (t, 0))
    return pl.pallas_call(
        functools.partial(_moe_residual_kernel, rows_per_seq=rows_per_seq, tm=tm,
                          final_norm=final_norm),
        grid=(m // tm,),
        in_specs=[x_spec] + _mod_specs(mod, l, (5,), rows_per_seq, tm)
                 + [_const_spec(wts['g_final'].shape),
                    pl.BlockSpec((tm * SUBLANES, LANES), lambda t: (t, 0))],
        out_specs=x_spec,
        out_shape=jax.ShapeDtypeStruct((m, d), F32),
        compiler_params=pltpu.CompilerParams(
            dimension_semantics=("arbitrary",), vmem_limit_bytes=VMEM_LIMIT),
        name=f"moe_residual_{l}_{m}",
    )(x2d, mod, wts['g_final'], y)


def _ffn_moe(routed_p, xs, mod_s, l, wts, w_bf):
    h_p, meta_p = routed_p
    m_p, m_s = meta_p.shape[0], xs.shape[0]
    n_tok = m_p + m_s
    s_tok = n_tok // N_SUPER
    assert s_tok * N_SUPER == n_tok and s_tok % SUBLANES == 0
    s_pad = -(-s_tok // IDX_ALIGN) * IDX_ALIGN

    h_s, meta_s = _moe_route(xs, mod_s, l, wts, 1)

    n_meta = 2 * N_EXPERTS + 2
    meta = jnp.concatenate([meta_p[:, :n_meta], meta_s[:, :n_meta]], axis=0)
    per_seg = lambda a: a.reshape(N_SUPER, s_tok, N_EXPERTS).transpose(0, 2, 1)
    gate = per_seg(meta[:, :N_EXPERTS])
    counts = jnp.sum(per_seg(meta[:, N_EXPERTS:2 * N_EXPERTS]).astype(jnp.int32), axis=2)
    offs = jnp.cumsum(counts, axis=1) - counts
    tok = lax.broadcasted_iota(jnp.int32, (N_SUPER, s_tok), 1)
    tok_bits = (s_tok - 1).bit_length()
    keys = jnp.concatenate(
        [meta[:, 2 * N_EXPERTS + k].astype(jnp.int32).reshape(N_SUPER, s_tok) * (1 << tok_bits) + tok
         for k in range(2)], axis=1)
    assert N_EXPERTS << tok_bits <= 1 << 16
    idx = lax.sort(keys.astype(jnp.uint16), dimension=1).astype(jnp.int32) & ((1 << tok_bits) - 1)
    l_pad = -(-(2 * s_tok + 2 * TM_EXPERT) // IDX_ALIGN) * IDX_ALIGN
    idx = jnp.pad(idx, ((0, 0), (0, l_pad - 2 * s_tok))).reshape(-1)
    gate = jnp.pad(gate, ((0, 0), (0, 0), (0, s_pad - s_tok))).reshape(-1)

    return _moe_experts(counts.reshape(-1), offs.reshape(-1), idx, gate, h_p, h_s, l, w_bf,
                        s_tok, s_pad, l_pad)


def kernel(x_prompt, x_sample, state_conv_a, state_conv_c, state_pool_d, c_prompt, c_sample,
           w_ada, b_ada, g_mix, w_in, conv_a_w, ln_v_g, ln_v_b, w_s, b_s, conv_c_w, conv_c_b,
           ln_c_g, ln_c_b, pool_w, pool_scale, g_out, w_out, g_ffn, w1_dense, w3_dense, w2_dense,
           w_router, b_router, w1_moe, w3_moe, w2_moe, g_final):
    p = dict(g_mix=g_mix, w_in=w_in, conv_a_w=conv_a_w, ln_v_g=ln_v_g, ln_v_b=ln_v_b, w_s=w_s,
             b_s=b_s, conv_c_w=conv_c_w, conv_c_b=conv_c_b, ln_c_g=ln_c_g, ln_c_b=ln_c_b,
             pool_w=pool_w, pool_scale=pool_scale, g_out=g_out, w_out=w_out, g_ffn=g_ffn,
             w1_dense=w1_dense, w3_dense=w3_dense, w2_dense=w2_dense, w_router=w_router,
             b_router=b_router, w1_moe=w1_moe, w3_moe=w3_moe, w2_moe=w2_moe, g_final=g_final)
    n_p, seq, d = x_prompt.shape
    n_s, dec_seq, _ = x_sample.shape
    assert dec_seq == 1 and d == D_MODEL

    mod_p, mod_s = _ada(c_prompt, c_sample, w_ada, b_ada)
    kmajor = lambda s: jnp.transpose(s, (0, 2, 1, 3))
    sa_t, sc_t, sd_t = kmajor(state_conv_a), kmajor(state_conv_c), kmajor(state_pool_d)
    mixw = _mix_params(p)
    pad_e = LANES - N_EXPERTS
    wts = dict(
        g_ffn=g_ffn.reshape(DEPTH, 1, d), g_final=g_final.reshape(1, d),
        w1_dense=w1_dense, w3_dense=w3_dense, w2_dense=w2_dense,
        w_router=jnp.pad(w_router, ((0, 0), (0, 0), (0, pad_e))),
        b_router=jnp.pad(b_router, ((0, 0), (0, pad_e))).reshape(-1, 1, LANES),
        w1_moe=w1_moe, w3_moe=w3_moe, w2_moe=w2_moe)

    xp = x_prompt
    xs = x_sample.reshape(n_s, d)
    states_p = [[], [], [], []]
    states_s = [[], [], [], []]
    pending = None
    moe_bf = None
    for l in range(DEPTH):
        last = l == DEPTH - 1
        moe = l % 2 == 1
        cast = [] if moe else [((wts[k],), l // 2) for k in ('w1_dense', 'w3_dense', 'w2_dense')]
        st_p, dense_bf = _mix_prompt(xp, mod_p, l, mixw, 0, pending, wts if moe else None, cast)
        xp, *st_p = st_p
        pending = None
        xs, *st_s = _mix_sample(xs, mod_s, sa_t, sc_t, sd_t, l, mixw, PAST_LEN)
        for acc, s in zip(states_p, st_p[:4]):
            acc.append(s)
        for acc, s in zip(states_s, st_s):
            acc.append(s)
        if moe:
            if moe_bf is None:
                w1, w3, w2 = (wts[k][l // 2].astype(BF16) for k in ('w1_moe', 'w3_moe', 'w2_moe'))
                moe_bf = [jnp.concatenate([w1, w3], axis=-1), w2]
            y_p, y_s = _ffn_moe(st_p[4:], xs, mod_s, l, wts, moe_bf)
            moe_bf = None
            xs = _moe_residual(xs, mod_s, y_s, l, wts, 1, last)
            if last:
                xp = _moe_residual(xp.reshape(n_p * seq, d), mod_p, y_p, l, wts, seq,
                                   True).reshape(n_p, seq, d)
            else:
                pending = (y_p, l)
        else:
            nxt = (l + 1) // 2
            cast = ([((wts['w1_moe'], wts['w3_moe']), nxt), ((wts['w2_moe'],), nxt)]
                    if l + 1 < DEPTH else [])
            xp2d, moe_bf = _ffn_dense(xp.reshape(n_p * seq, d), mod_p, l, wts, dense_bf, seq, last,
                                      cast)
            xp = xp2d.reshape(n_p, seq, d)
            xs, _ = _ffn_dense(xs, mod_s, l, wts, dense_bf, 1, last)
            moe_bf = moe_bf or None

    a_p, c_p, d_p, v_p = (jnp.stack(s) for s in states_p)
    a_s, c_s, d_s = (kmajor(jnp.stack(s)) for s in states_s[:3])
    v_s = jnp.stack(states_s[3]).reshape(DEPTH, n_s, 1, W_GROUP)
    return (xp, xs.reshape(n_s, 1, d), a_p, c_p, d_p, v_p, a_s, c_s, d_s, v_s)
```

```python
import functools

import jax
import jax.numpy as jnp
from jax import lax
from jax.experimental import pallas as pl
from jax.experimental.pallas import tpu as pltpu

D_MODEL = 1024
DEPTH = 4
W_GROUP = 256
CONV_A = 3
CHUNK = 128
N_HEADS_B = 4
HEAD_B = 64
CONV_C = 31
POOL_WINDOWS = (2, 4, 8, 16)
POOL_MAX = 16
GROUP_D = 64
IN_COLS = 2048
N_EXPERTS = 8
PAST_LEN = 16384
EPS = 1e-6

LANES = 128
LANE_TILES = W_GROUP // LANES
HIST_A = 8
HIST_C = 32
HIST_D = 32
TL_MIX = 512
ROW_CHUNK = 64
TM_FFN = 512
SUBLANES = 8
N_SUPER = 4
TM_EXPERT = 256
IDX_ALIGN = 1024
VMEM_LIMIT = 56 * 1024 * 1024
VMEM_LIMIT_DENSE = 60 * 1024 * 1024
VMEM_LIMIT_EXPERT = 62 * 1024 * 1024

F32 = jnp.float32
BF16 = jnp.bfloat16


def _dot(a, b):
    return jnp.dot(a, b, preferred_element_type=F32)


def _rms(x, g):
    return x * lax.rsqrt(jnp.mean(x * x, axis=-1, keepdims=True) + EPS) * g


def _ln(x, g, b):
    xc = x - jnp.mean(x, axis=-1, keepdims=True)
    var = jnp.mean(xc * xc, axis=-1, keepdims=True)
    return xc * lax.rsqrt(var + EPS) * g + b


def _silu(x):
    return x * jax.nn.sigmoid(x)


def _lane_group_select(vals, shape):
    lane = lax.broadcasted_iota(jnp.int32, shape, 1)
    out = vals[3]
    for g in (2, 1, 0):
        out = jnp.where(lane < (g + 1) * GROUP_D, vals[g], out)
    return out


ADA_CHUNKS = 2


def _ada_kernel(cp_ref, cs_ref, w_ref, b_ref, op_ref, os_ref):
    cp = _silu(cp_ref[...]).astype(BF16)
    cs = _silu(cs_ref[...]).astype(BF16)
    for k in range(ADA_CHUNKS):
        cols = slice(k * D_MODEL, (k + 1) * D_MODEL)
        w = w_ref[:, cols].astype(BF16)
        op_ref[k] = _dot(cp, w) + b_ref[:, cols]
        os_ref[k] = _dot(cs, w) + b_ref[:, cols]


def _ada(c_prompt, c_sample, w_ada, b_ada):
    n_p, n_s = c_prompt.shape[0], c_sample.shape[0]
    d = D_MODEL
    wide = ADA_CHUNKS * d
    return pl.pallas_call(
        _ada_kernel,
        grid=(DEPTH, 6 // ADA_CHUNKS),
        in_specs=[
            pl.BlockSpec((n_p, d), lambda l, k: (0, 0)),
            pl.BlockSpec((n_s, d), lambda l, k: (0, 0)),
            pl.BlockSpec((None, d, wide), lambda l, k: (l, 0, k)),
            pl.BlockSpec((None, 1, wide), lambda l, k: (l, 0, k)),
        ],
        out_specs=[
            pl.BlockSpec((None, ADA_CHUNKS, n_p, d), lambda l, k: (l, k, 0, 0)),
            pl.BlockSpec((None, ADA_CHUNKS, n_s, d), lambda l, k: (l, k, 0, 0)),
        ],
        out_shape=[
            jax.ShapeDtypeStruct((DEPTH, 6, n_p, d), F32),
            jax.ShapeDtypeStruct((DEPTH, 6, n_s, d), F32),
        ],
        compiler_params=pltpu.CompilerParams(
            dimension_semantics=("arbitrary", "arbitrary"), vmem_limit_bytes=VMEM_LIMIT),
        name="ada",
    )(c_prompt, c_sample, w_ada, b_ada.reshape(DEPTH, 1, 6 * d))


def _masked_ws(ws_ref):
    r = lax.broadcasted_iota(jnp.int32, (CHUNK, CHUNK), 0)
    c = lax.broadcasted_iota(jnp.int32, (CHUNK, CHUNK), 1)
    return [jnp.where(c <= r, ws_ref[h], 0.0).astype(BF16) for h in range(N_HEADS_B)]


def _ext_rows(ref, j, row0, n):
    return ref[pl.ds(row0 * LANE_TILES + j, n, stride=LANE_TILES), :]


def _ext_store(ref, row0, val):
    for j in range(LANE_TILES):
        ref[pl.ds(row0 * LANE_TILES + j, val.shape[0], stride=LANE_TILES), :] = (
            val[:, j * LANES:(j + 1) * LANES])


def _ext_load(ref, row0, n):
    return jnp.concatenate([_ext_rows(ref, j, row0, n) for j in range(LANE_TILES)], axis=1)


def _merge_and_project(x, gate, outs, gout_ref, wout_bf):
    merged = jnp.concatenate(
        [_rms(o, gout_ref[i:i + 1, :]) for i, o in enumerate(outs)], axis=1).astype(BF16)
    return x + gate * _dot(merged, wout_bf)


def _to_token_tiles(ref, val, tm):
    for k in range(SUBLANES):
        ref[pl.ds(k, tm, stride=SUBLANES), :] = val[:, k * LANES:(k + 1) * LANES]


def _from_token_tiles(ref, tm):
    return jnp.concatenate(
        [ref[pl.ds(k, tm, stride=SUBLANES), :] for k in range(SUBLANES)], axis=1)


def _router_meta(h32, wr_ref, br_ref):
    wr = wr_ref[...]
    h_hi = h32.astype(BF16)
    h_lo = (h32 - h_hi.astype(F32)).astype(BF16)
    w_hi = wr.astype(BF16)
    w_lo = (wr - w_hi.astype(F32)).astype(BF16)
    logits = _dot(h_hi, w_hi) + (_dot(h_lo, w_hi) + _dot(h_hi, w_lo)) + br_ref[...]
    lane = lax.broadcasted_iota(jnp.int32, logits.shape, 1)
    lane_f = lane.astype(F32)
    neg = jnp.float32(-jnp.inf)
    logits = jnp.where(lane < N_EXPERTS, logits, neg)
    m1 = jnp.max(logits, axis=-1, keepdims=True)
    i1 = jnp.min(jnp.where(logits == m1, lane_f, float(LANES)), axis=-1, keepdims=True)
    rest = jnp.where(lane_f == i1, neg, logits)
    m2 = jnp.max(rest, axis=-1, keepdims=True)
    i2 = jnp.min(jnp.where(rest == m2, lane_f, float(LANES)), axis=-1, keepdims=True)
    e = jnp.exp(m2 - m1)
    g1 = 1.0 / (1.0 + e)
    g2 = e / (1.0 + e)
    comb = jnp.where(lane_f == i1, g1, 0.0) + jnp.where(lane_f == i2, g2, 0.0)
    flags = (jnp.where(lane_f == i1 + N_EXPERTS, 1.0, 0.0)
             + jnp.where(lane_f == i2 + N_EXPERTS, 1.0, 0.0))
    chosen = (jnp.where(lane == 2 * N_EXPERTS, i1, 0.0)
              + jnp.where(lane == 2 * N_EXPERTS + 1, i2, 0.0))
    return comb + flags + chosen


def _mix_prompt_kernel(*refs, tl, start_pos, pre, post, cast_groups):
    refs = list(refs)
    take = lambda n: [refs.pop(0) for _ in range(n)]
    (x_ref,) = take(1)
    y_ref, gprev_ref = take(2) if pre else (None, None)
    shift_ref, scale_ref, gate_ref = take(3)
    shift2_ref, scale2_ref = take(2) if post else (None, None)
    (gmix_ref, win_ref, wout_ref, convaw_ref, lnvg_ref, lnvb_ref, ws_ref, bsfull_ref, convcw_ref,
     convcb_ref, lncg_ref, lncb_ref, poolw_ref, poolscale_ref, gout_ref) = take(15)
    gffn_ref, wr_ref, br_ref = take(3) if post else (None, None, None)
    cast_src = take(sum(cast_groups))
    xo_ref, na_ref, nc_ref, nd_ref, nv_ref = take(5)
    h_ref, meta_ref = take(2) if post else (None, None)
    cast_dst = take(len(cast_groups))
    win_bf, wout_bf, exta, extc, extd, ext2, ext4, ext8 = take(8)
    assert not refs
    _run_cast(cast_src, cast_dst, cast_groups)
    b = pl.program_id(0)
    t = pl.program_id(1)
    last_t = pl.num_programs(1) - 1

    @pl.when(jnp.logical_and(b == 0, t == 0))
    def _cast_weights():
        rows = 128
        def body(i, carry):
            r0 = pl.multiple_of(i * rows, rows)
            win_bf[pl.ds(r0, rows), :] = win_ref[pl.ds(r0, rows), :].astype(BF16)
            wout_bf[pl.ds(r0, rows), :] = wout_ref[pl.ds(r0, rows), :].astype(BF16)
            return carry
        lax.fori_loop(0, D_MODEL // rows, body, 0)

    @pl.when(t == 0)
    def _zero_history():
        for ref, hist in ((exta, HIST_A), (extc, HIST_C), (extd, HIST_D)):
            ref[0:hist * LANE_TILES, :] = jnp.zeros((hist * LANE_TILES, LANES), F32)

    x = x_ref[...]
    if pre:
        x = x + gprev_ref[pl.ds(b, 1), :] * _from_token_tiles(y_ref, tl)
    shift = shift_ref[pl.ds(b, 1), :]
    scale = scale_ref[pl.ds(b, 1), :]
    gate = gate_ref[pl.ds(b, 1), :]
    h = _rms(x, gmix_ref[...] * (1.0 + scale)) + shift
    proj = _dot(h.astype(BF16), win_bf[...])
    a_b, a_c, a_h, b_u, b_v, c_a, c_g, d_p = [
        proj[:, i * W_GROUP:(i + 1) * W_GROUP] for i in range(8)]

    def dwconv(ext_ref, w_ref, hist, width):
        off = hist - (width - 1)
        halves = []
        for j in range(LANE_TILES):
            chunks = []
            for c0 in range(0, tl, ROW_CHUNK):
                acc = None
                for k in range(width):
                    term = (_ext_rows(ext_ref, j, off + c0 + k, ROW_CHUNK)
                            * w_ref[k:k + 1, j * LANES:(j + 1) * LANES])
                    acc = term if acc is None else acc + term
                chunks.append(acc)
            halves.append(jnp.concatenate(chunks, axis=0))
        return jnp.concatenate(halves, axis=1)

    _ext_store(exta, HIST_A, a_c * a_h)
    out_a = a_b * dwconv(exta, convaw_ref, HIST_A, CONV_A)

    v_n = _ln(b_v, lnvg_ref[...], lnvb_ref[...])
    v_bf = v_n.astype(BF16)
    wm = _masked_ws(ws_ref)
    lane = lax.broadcasted_iota(jnp.int32, (CHUNK, W_GROUP), 1)
    mixed_chunks = []
    for j in range(tl // CHUNK):
        vc = v_bf[j * CHUNK:(j + 1) * CHUNK, :]
        mixed = _dot(wm[3], vc)
        for hd in (2, 1, 0):
            mixed = jnp.where(lane < (hd + 1) * HEAD_B, _dot(wm[hd], vc), mixed)
        mixed_chunks.append(mixed + bsfull_ref[...])
    out_b = b_u * jnp.concatenate(mixed_chunks, axis=0)

    _ext_store(extc, HIST_C, c_a * jax.nn.sigmoid(c_g))
    y_c = dwconv(extc, convcw_ref, HIST_C, CONV_C) + convcb_ref[...]
    out_c = _silu(_ln(y_c, lncg_ref[...], lncb_ref[...]))

    n = HIST_D + tl
    _ext_store(extd, HIST_D, d_p)
    bufs = (extd, ext2, ext4, ext8)
    pos1 = start_pos + 1 + t * tl + lax.broadcasted_iota(jnp.int32, (tl, LANES), 0)
    low_group = lax.broadcasted_iota(jnp.int32, (tl, LANES), 1) < GROUP_D
    means = []
    for j in range(LANE_TILES):
        levels = 2 * (j + 1)
        for lv in range(levels - 1):
            first = 8 * (lv + 1)
            bufs[lv + 1][pl.ds(first * LANE_TILES + j, n - first, stride=LANE_TILES), :] = (
                _ext_rows(bufs[lv], j, first, n - first)
                + _ext_rows(bufs[lv], j, first - (1 << lv), n - first))
        prev = bufs[levels - 1]
        s_lo = _ext_rows(prev, j, HIST_D, tl)
        s_hi = s_lo + _ext_rows(prev, j, HIST_D - (1 << (levels - 1)), tl)
        w_lo, w_hi = POOL_WINDOWS[2 * j], POOL_WINDOWS[2 * j + 1]
        cnt = jnp.where(low_group, jnp.minimum(pos1, w_lo), jnp.minimum(pos1, w_hi)).astype(F32)
        means.append(jnp.where(low_group, s_lo, s_hi) / cnt)
    pooled = jnp.concatenate(means, axis=1) - d_p
    out_d = _dot(pooled.astype(BF16), poolw_ref[...].astype(BF16)) * poolscale_ref[...]

    x_new = _merge_and_project(x, gate, [out_a, out_b, out_c, out_d], gout_ref, wout_bf[...])
    xo_ref[...] = x_new
    if post:
        h2 = (_rms(x_new, gffn_ref[...]) * (1.0 + scale2_ref[pl.ds(b, 1), :])
              + shift2_ref[pl.ds(b, 1), :])
        _to_token_tiles(h_ref, h2, tl)
        meta_ref[...] = _router_meta(h2, wr_ref, br_ref)

    @pl.when(t == last_t)
    def _emit_state():
        na_ref[...] = _ext_load(exta, HIST_A + tl - (CONV_A - 1), CONV_A - 1)
        nc_ref[...] = _ext_load(extc, HIST_C + tl - (CONV_C - 1), CONV_C - 1)
        nd_ref[...] = _ext_load(extd, HIST_D + tl - (POOL_MAX - 1), POOL_MAX - 1)
        nv_ref[...] = v_n[tl - CHUNK:tl, :]

    for ref, hist in ((exta, HIST_A), (extc, HIST_C), (extd, HIST_D)):
        ref[0:hist * LANE_TILES, :] = ref[tl * LANE_TILES:(tl + hist) * LANE_TILES, :]


def _mix_params(p):
    r3 = lambda a: a.reshape(DEPTH, 1, -1)
    eye = jnp.eye(4, dtype=F32)
    pool_bd = (eye[None, :, None, :, None] * p['pool_w'][:, :, :, None, :]).reshape(
        DEPTH, W_GROUP, W_GROUP)
    return dict(
        gmix=r3(p['g_mix']), w_in=p['w_in'], w_out=p['w_out'], conva=p['conv_a_w'],
        lnvg=r3(p['ln_v_g']), lnvb=r3(p['ln_v_b']), ws=p['w_s'],
        bsfull=jnp.repeat(jnp.swapaxes(p['b_s'], 1, 2), HEAD_B, axis=2),
        ws0=r3(jnp.repeat(p['w_s'][:, :, 0, 0], HEAD_B, axis=1)),
        bs0=r3(jnp.repeat(p['b_s'][:, :, 0], HEAD_B, axis=1)),
        convc=p['conv_c_w'], convcb=r3(p['conv_c_b']), lncg=r3(p['ln_c_g']), lncb=r3(p['ln_c_b']),
        poolw=pool_bd, poolscale=r3(p['pool_scale']), gout=p['g_out'])


def _layer_spec(a, l):
    nd = a.ndim - 1
    return pl.BlockSpec((None,) + a.shape[1:], lambda *_: (l,) + (0,) * nd,
                        pipeline_mode=pl.Buffered(1))


def _const_spec(shape):
    nd = len(shape)
    return pl.BlockSpec(shape, lambda *_: (0,) * nd, pipeline_mode=pl.Buffered(1))


BF16_ROWS = 16


def _cast_job(cast, steps, step_of):
    args, in_specs, out_specs, out_shapes = [], [], [], []
    for arrays, j in cast:
        a = arrays[0]
        assert all(o.shape == a.shape for o in arrays) and a.shape[-1] % LANES == 0
        groups = a.shape[1] if a.ndim == 4 else 1
        rows, cols = a.shape[-2:]
        assert steps % groups == 0
        blocks = steps // groups
        while rows % (blocks * BF16_ROWS):
            assert blocks % 2 == 0
            blocks //= 2
        rep = steps // (groups * blocks)

        def block_of(*g, rep=rep, blocks=blocks, grouped=a.ndim == 4):
            q = step_of(*g) // rep
            return (q // blocks, q % blocks, 0) if grouped else (q, 0)

        lead = (None,) * (a.ndim - 2)
        wide = cols * len(arrays)
        for o in arrays:
            args.append(o)
            in_specs.append(pl.BlockSpec(lead + (rows // blocks, cols),
                                         lambda *g, j=j, f=block_of: (j,) + f(*g)))
        out_specs.append(pl.BlockSpec(lead[1:] + (rows // blocks, wide), block_of))
        out_shapes.append(jax.ShapeDtypeStruct(a.shape[1:-1] + (wide,), BF16))
    return args, in_specs, out_specs, out_shapes, [len(arrays) for arrays, _ in cast]


def _run_cast(srcs, dsts, group_sizes):
    srcs = list(srcs)
    for dst, n in zip(dsts, group_sizes):
        parts = [srcs.pop(0)[...].astype(BF16) for _ in range(n)]
        dst[...] = parts[0] if n == 1 else jnp.concatenate(parts, axis=-1)


def _mix_prompt(x, mod_p, l, w, start_pos, prev_moe=None, wts=None, cast=()):
    n_b, seq, d = x.shape
    tl = TL_MIX
    nt = seq // tl
    assert seq % tl == 0 and tl % CHUNK == 0 and seq >= CHUNK
    pre, post = prev_moe is not None, wts is not None
    mod_spec = lambda lay, k: pl.BlockSpec((None, None, n_b, d), lambda b, t: (lay, k, 0, 0))
    tiles_spec = lambda rows: pl.BlockSpec((tl * rows, LANES), lambda b, t: (b * nt + t, 0))
    weights = [w[k] for k in ('gmix', 'w_in', 'w_out', 'conva', 'lnvg', 'lnvb', 'ws', 'bsfull',
                              'convc', 'convcb', 'lncg', 'lncb', 'poolw', 'poolscale', 'gout')]
    state_spec = lambda r: pl.BlockSpec((None, r, W_GROUP), lambda b, t: (b, 0, 0))
    ext = lambda hist: pltpu.VMEM(((hist + tl) * LANE_TILES, LANES), F32)

    args, in_specs = [x], [pl.BlockSpec((None, tl, d), lambda b, t: (b, t, 0))]
    if pre:
        y_prev, l_prev = prev_moe
        args += [y_prev, mod_p]
        in_specs += [tiles_spec(SUBLANES), mod_spec(l_prev, 5)]
    args += [mod_p] * 3
    in_specs += [mod_spec(l, 0), mod_spec(l, 1), mod_spec(l, 2)]
    if post:
        args += [mod_p] * 2
        in_specs += [mod_spec(l, 3), mod_spec(l, 4)]
    args += weights
    in_specs += [_layer_spec(a, l) for a in weights]
    out_specs = [pl.BlockSpec((None, tl, d), lambda b, t: (b, t, 0)),
                 state_spec(CONV_A - 1), state_spec(CONV_C - 1), state_spec(POOL_MAX - 1),
                 state_spec(CHUNK)]
    out_shape = [jax.ShapeDtypeStruct(x.shape, F32),
                 jax.ShapeDtypeStruct((n_b, CONV_A - 1, W_GROUP), F32),
                 jax.ShapeDtypeStruct((n_b, CONV_C - 1, W_GROUP), F32),
                 jax.ShapeDtypeStruct((n_b, POOL_MAX - 1, W_GROUP), F32),
                 jax.ShapeDtypeStruct((n_b, CHUNK, W_GROUP), F32)]
    if post:
        i = l // 2
        args += [wts['g_ffn'], wts['w_router'], wts['b_router']]
        in_specs += [_layer_spec(wts['g_ffn'], l), _layer_spec(wts['w_router'], i),
                     _layer_spec(wts['b_router'], i)]
        out_specs += [tiles_spec(SUBLANES), tiles_spec(1)]
        out_shape += [jax.ShapeDtypeStruct((n_b * seq * SUBLANES, LANES), F32),
                      jax.ShapeDtypeStruct((n_b * seq, LANES), F32)]
    c_args, c_in, c_out, c_shapes, c_groups = _cast_job(cast, n_b * nt,
                                                        lambda b, t: b * nt + t)
    out = pl.pallas_call(
        functools.partial(_mix_prompt_kernel, tl=tl, start_pos=start_pos, pre=pre, post=post,
                          cast_groups=tuple(c_groups)),
        grid=(n_b, nt),
        in_specs=in_specs + c_in,
        out_specs=out_specs + c_out,
        out_shape=out_shape + c_shapes,
        scratch_shapes=[pltpu.VMEM((d, IN_COLS), BF16), pltpu.VMEM((d, d), BF16),
                        ext(HIST_A), ext(HIST_C), ext(HIST_D), ext(HIST_D), ext(HIST_D),
                        ext(HIST_D)],
        compiler_params=pltpu.CompilerParams(
            dimension_semantics=("arbitrary", "arbitrary"), vmem_limit_bytes=VMEM_LIMIT),
        name=f"mix_prompt_{l}",
    )(*args, *c_args)
    n_out = len(out) - len(cast)
    return out[:n_out], list(out[n_out:])


def _mix_sample_kernel(x_ref, shift_ref, scale_ref, gate_ref, sa_ref, sc_ref, sd_ref, gmix_ref,
                       win_ref, wout_ref, convaw_ref, lnvg_ref, lnvb_ref, ws0_ref, bs0_ref,
                       convcw_ref, convcb_ref, lncg_ref, lncb_ref, poolw_ref, poolscale_ref,
                       gout_ref, xo_ref, na_ref, nc_ref, nd_ref, nv_ref, *, start_pos):
    x = x_ref[...]
    h = _rms(x, gmix_ref[...]) * (1.0 + scale_ref[...]) + shift_ref[...]
    proj = _dot(h.astype(BF16), win_ref[...])
    a_b, a_c, a_h, b_u, b_v, c_a, c_g, d_p = [
        proj[:, i * W_GROUP:(i + 1) * W_GROUP] for i in range(8)]

    ch = a_c * a_h
    y_a = convaw_ref[CONV_A - 1:CONV_A, :] * ch
    for k in range(CONV_A - 1):
        y_a = y_a + convaw_ref[k:k + 1, :] * sa_ref[k]
    out_a = a_b * y_a
    for k in range(CONV_A - 2):
        na_ref[k] = sa_ref[k + 1]
    na_ref[CONV_A - 2] = ch

    v_n = _ln(b_v, lnvg_ref[...], lnvb_ref[...])
    out_b = b_u * (ws0_ref[...] * v_n + bs0_ref[...])
    nv_ref[...] = v_n

    glu = c_a * jax.nn.sigmoid(c_g)
    y_c = convcw_ref[CONV_C - 1:CONV_C, :] * glu + convcb_ref[...]
    for k in range(CONV_C - 1):
        y_c = y_c + convcw_ref[k:k + 1, :] * sc_ref[k]
    out_c = _silu(_ln(y_c, lncg_ref[...], lncb_ref[...]))
    for k in range(CONV_C - 2):
        nc_ref[k] = sc_ref[k + 1]
    nc_ref[CONV_C - 2] = glu

    hist = POOL_MAX - 1
    run = d_p
    taken = 0
    sums = []
    for w in POOL_WINDOWS:
        while taken < w - 1:
            run = run + sd_ref[hist - 1 - taken]
            taken += 1
        sums.append(run / float(min(start_pos + 1, w)))
    pooled = _lane_group_select(sums, d_p.shape) - d_p
    out_d = _dot(pooled.astype(BF16), poolw_ref[...].astype(BF16)) * poolscale_ref[...]
    for k in range(hist - 1):
        nd_ref[k] = sd_ref[k + 1]
    nd_ref[hist - 1] = d_p

    xo_ref[...] = _merge_and_project(x, gate_ref[...], [out_a, out_b, out_c, out_d], gout_ref,
                                     wout_ref[...])


def _mix_sample(x, mod_s, sa_t, sc_t, sd_t, l, w, proj_bf, start_pos):
    n, d = x.shape
    assert start_pos + 1 >= POOL_MAX
    mod_spec = lambda k: pl.BlockSpec((None, None, n, d), lambda i: (l, k, 0, 0))
    st_spec = lambda r: pl.BlockSpec((None, r, n, W_GROUP), lambda i: (l, 0, 0, 0))
    names = ('gmix', 'w_in', 'w_out', 'conva', 'lnvg', 'lnvb', 'ws0', 'bs0',
             'convc', 'convcb', 'lncg', 'lncb', 'poolw', 'poolscale', 'gout')
    own = dict(w_in=proj_bf[0], w_out=proj_bf[1])
    weights = [own.get(k, w[k]) for k in names]
    w_specs = [_const_spec(own[k].shape) if k in own else _layer_spec(w[k], l) for k in names]
    full = lambda shape: pl.BlockSpec(shape, lambda i: (0,) * len(shape))
    return pl.pallas_call(
        functools.partial(_mix_sample_kernel, start_pos=start_pos),
        grid=(1,),
        in_specs=[full((n, d)), mod_spec(0), mod_spec(1), mod_spec(2),
                  st_spec(CONV_A - 1), st_spec(CONV_C - 1), st_spec(POOL_MAX - 1)] + w_specs,
        out_specs=[full((n, d)), full((CONV_A - 1, n, W_GROUP)), full((CONV_C - 1, n, W_GROUP)),
                   full((POOL_MAX - 1, n, W_GROUP)), full((n, W_GROUP))],
        out_shape=[jax.ShapeDtypeStruct((n, d), F32),
                   jax.ShapeDtypeStruct((CONV_A - 1, n, W_GROUP), F32),
                   jax.ShapeDtypeStruct((CONV_C - 1, n, W_GROUP), F32),
                   jax.ShapeDtypeStruct((POOL_MAX - 1, n, W_GROUP), F32),
                   jax.ShapeDtypeStruct((n, W_GROUP), F32)],
        compiler_params=pltpu.CompilerParams(
            dimension_semantics=("arbitrary",), vmem_limit_bytes=VMEM_LIMIT),
        name=f"mix_sample_{l}",
    )(x, mod_s, mod_s, mod_s, sa_t, sc_t, sd_t, *weights)


def _modulation(refs, rows_per_seq, tm):
    if rows_per_seq == 1:
        return [r[...] for r in refs]
    b = (pl.program_id(0) * tm) // rows_per_seq
    return [r[pl.ds(b, 1), :] for r in refs]


def _ffn_dense_kernel(x_ref, shift_ref, scale_ref, gate_ref, gffn_ref, gfin_ref, w1_ref, w3_ref,
                      w2_ref, *rest, rows_per_seq, tm, final_norm, cast_groups):
    n_src = sum(cast_groups)
    o_ref = rest[n_src]
    x = x_ref[...]
    shift, scale, gate = _modulation([shift_ref, scale_ref, gate_ref], rows_per_seq, tm)
    h = (_rms(x, gffn_ref[...]) * (1.0 + scale) + shift).astype(BF16)
    act = (_silu(_dot(h, w1_ref[...])) * _dot(h, w3_ref[...])).astype(BF16)
    y = x + gate * _dot(act, w2_ref[...])
    o_ref[...] = _rms(y, gfin_ref[...]) if final_norm else y
    _run_cast(rest[:n_src], rest[n_src + 1:], cast_groups)


def _mod_specs(mod, l, ks, rows_per_seq, tm):
    d = mod.shape[-1]
    if rows_per_seq == 1:
        return [pl.BlockSpec((None, None, tm, d), lambda t, k=k: (l, k, t, 0)) for k in ks]
    n_seq = mod.shape[2]
    return [pl.BlockSpec((None, None, n_seq, d), lambda t, k=k: (l, k, 0, 0)) for k in ks]


def _ffn_dense(x2d, mod, l, wts, w_bf, rows_per_seq, final_norm, cast=()):
    m, d = x2d.shape
    tm = min(TM_FFN, m)
    assert m % tm == 0 and (rows_per_seq == 1 or rows_per_seq % tm == 0)
    steps = m // tm
    x_spec = pl.BlockSpec((tm, d), lambda t: (t, 0))
    c_args, c_in, c_out, c_shapes, c_groups = _cast_job(cast, steps, lambda t: t)
    out = pl.pallas_call(
        functools.partial(_ffn_dense_kernel, rows_per_seq=rows_per_seq, tm=tm,
                          final_norm=final_norm, cast_groups=tuple(c_groups)),
        grid=(steps,),
        in_specs=[x_spec] + _mod_specs(mod, l, (3, 4, 5), rows_per_seq, tm)
                 + [_layer_spec(wts['g_ffn'], l), _const_spec(wts['g_final'].shape)]
                 + [_const_spec(a.shape) for a in w_bf] + c_in,
        out_specs=[x_spec] + c_out,
        out_shape=[jax.ShapeDtypeStruct((m, d), F32)] + c_shapes,
        compiler_params=pltpu.CompilerParams(
            dimension_semantics=("arbitrary",), vmem_limit_bytes=VMEM_LIMIT_DENSE),
        name=f"ffn_dense_{l}",
    )(x2d, mod, mod, mod, wts['g_ffn'], wts['g_final'], *w_bf, *c_args)
    return out[0], list(out[1:])


def _moe_route_kernel(x_ref, shift_ref, scale_ref, gffn_ref, wr_ref, br_ref, h_ref, meta_ref, *,
                      rows_per_seq, tm):
    shift, scale = _modulation([shift_ref, scale_ref], rows_per_seq, tm)
    h32 = _rms(x_ref[...], gffn_ref[...]) * (1.0 + scale) + shift
    _to_token_tiles(h_ref, h32, tm)
    meta_ref[...] = _router_meta(h32, wr_ref, br_ref)


def _moe_route(x2d, mod, l, wts, rows_per_seq):
    m, d = x2d.shape
    tm = min(TM_FFN, m)
    assert m % tm == 0 and (rows_per_seq == 1 or rows_per_seq % tm == 0)
    i = l // 2
    return pl.pallas_call(
        functools.partial(_moe_route_kernel, rows_per_seq=rows_per_seq, tm=tm),
        grid=(m // tm,),
        in_specs=[pl.BlockSpec((tm, d), lambda t: (t, 0))]
                 + _mod_specs(mod, l, (3, 4), rows_per_seq, tm)
                 + [_layer_spec(wts['g_ffn'], l), _layer_spec(wts['w_router'], i),
                    _layer_spec(wts['b_router'], i)],
        out_specs=[pl.BlockSpec((tm * SUBLANES, LANES), lambda t: (t, 0)),
                   pl.BlockSpec((tm, LANES), lambda t: (t, 0))],
        out_shape=[jax.ShapeDtypeStruct((m * SUBLANES, LANES), F32),
                   jax.ShapeDtypeStruct((m, LANES), F32)],
        compiler_params=pltpu.CompilerParams(
            dimension_semantics=("arbitrary",), vmem_limit_bytes=VMEM_LIMIT),
        name=f"moe_route_{l}_{m}",
    )(x2d, mod, mod, wts['g_ffn'], wts['w_router'], wts['b_router'])


def _super_block_pieces(m_p, m_s, s_tok):
    pieces = []
    for k in range(N_SUPER):
        lo, hi = k * s_tok, (k + 1) * s_tok
        ps = []
        if lo < m_p:
            ps.append((0, lo, 0, min(hi, m_p) - lo))
        if hi > m_p:
            s0 = max(lo, m_p)
            ps.append((1, s0 - m_p, s0 - lo, hi - s0))
        pieces.append(ps)
    return pieces


def _moe_expert_kernel(cnt_ref, off_ref, idx_hbm, g_hbm, hp_hbm, hs_hbm, w13_ref, w2_ref,
                       yp_hbm, ys_hbm, h_scr, y_scr, xbuf, obuf, idx_s, g_s, sem,
                       *, pieces, s_tok, s_pad, l_pad, tm):
    sb = pl.program_id(0)
    e = pl.program_id(1)
    seg = sb * N_EXPERTS + e
    rows = s_tok * SUBLANES

    def piece_copies(k, to_vmem):
        copies = []
        for j, (grp, src_tok, dst_tok, n) in enumerate(pieces[k]):
            hbm = ((hp_hbm, hs_hbm) if to_vmem else (yp_hbm, ys_hbm))[grp]
            hbm = hbm.at[pl.ds(src_tok * SUBLANES, n * SUBLANES)]
            if to_vmem:
                copies.append(pltpu.make_async_copy(
                    hbm, h_scr.at[pl.ds(dst_tok * SUBLANES, n * SUBLANES)], sem.at[j]))
            else:
                copies.append(pltpu.make_async_copy(
                    y_scr.at[pl.ds(dst_tok * SUBLANES, n * SUBLANES)], hbm, sem.at[2 + j]))
        return copies

    def gate_copy(s):
        slot = lax.rem(s, 2)
        src = pl.ds(pl.multiple_of(s * s_pad, IDX_ALIGN), s_pad)
        dst = pl.ds(pl.multiple_of(slot * s_pad, IDX_ALIGN), s_pad)
        return pltpu.make_async_copy(g_hbm.at[src], g_s.at[dst], sem.at[6 + slot])

    def list_copy():
        src = pl.ds(pl.multiple_of(sb * l_pad, IDX_ALIGN), l_pad)
        return pltpu.make_async_copy(idx_hbm.at[src], idx_s, sem.at[4])

    @pl.when(seg == 0)
    def _first_gates():
        gate_copy(seg).start()
        obuf[...] = jnp.zeros(obuf.shape, F32)

    @pl.when(seg + 1 < N_SUPER * N_EXPERTS)
    def _next_gates():
        gate_copy(seg + 1).start()

    for k in range(N_SUPER):
        @pl.when(jnp.logical_and(sb == k, e == 0))
        def _load_super_block(k=k):
            list_copy().start()
            for c in piece_copies(k, True):
                c.start()
            if k > 0:
                for c in piece_copies(k - 1, False):
                    c.wait()
            zrows = 256
            assert rows % zrows == 0
            def zero(i, carry):
                r0 = pl.multiple_of(i * zrows, zrows)
                y_scr[pl.ds(r0, zrows), :] = jnp.zeros((zrows, LANES), F32)
                return carry
            lax.fori_loop(0, rows // zrows, zero, 0)
            for c in piece_copies(k, True):
                c.wait()
            list_copy().wait()

    gate_copy(seg).wait()
    gts = lax.rem(seg, 2) * s_pad
    lst = off_ref[seg]

    def gather(base):
        for r in range(tm):
            t8 = pl.multiple_of(idx_s[lst + base + r] * SUBLANES, SUBLANES)
            xbuf[r * SUBLANES:(r + 1) * SUBLANES, :] = h_scr[pl.ds(t8, SUBLANES), :]

    def scatter_add(base, limit):
        for r0 in range(0, tm, SUBLANES):
            upd = []
            for r in range(r0, r0 + SUBLANES):
                tok = idx_s[lst + base + r]
                t8 = pl.multiple_of(tok * SUBLANES, SUBLANES)
                g = jnp.where(base + r < limit, g_s[gts + tok], 0.0)
                o = obuf[r * SUBLANES:(r + 1) * SUBLANES, :]
                upd.append((t8, y_scr[pl.ds(t8, SUBLANES), :] + g * o))
            for t8, v in upd:
                y_scr[pl.ds(t8, SUBLANES), :] = v

    n_sel = cnt_ref[seg]
    n_tiles = (n_sel + tm - 1) // tm

    def tile(i, carry):
        x = _from_token_tiles(xbuf, tm).astype(BF16)
        gather((i + 1) * tm)
        scatter_add(jnp.maximum(i - 1, 0) * tm, jnp.where(i > 0, n_sel, 0))
        up = _dot(x, w13_ref[...])
        ff = up.shape[1] // 2
        act = (_silu(up[:, :ff]) * up[:, ff:]).astype(BF16)
        _to_token_tiles(obuf, _dot(act, w2_ref[...]), tm)
        return carry

    gather(0)
    lax.fori_loop(0, n_tiles, tile, 0)

    @pl.when(n_tiles > 0)
    def _last_scatter():
        scatter_add((n_tiles - 1) * tm, n_sel)

    for k in range(N_SUPER):
        @pl.when(jnp.logical_and(sb == k, e == N_EXPERTS - 1))
        def _store_super_block(k=k):
            for c in piece_copies(k, False):
                c.start()
            if k == N_SUPER - 1:
                for c in piece_copies(k, False):
                    c.wait()


def _moe_experts(counts, offs, idx, gates, h_p, h_s, l, w_bf, s_tok, s_pad, l_pad):
    tm = TM_EXPERT
    w13, w2 = w_bf
    w_spec = lambda a: pl.BlockSpec((None,) + a.shape[1:], lambda sb, e, cnt, off: (e, 0, 0))
    any_spec = pl.BlockSpec(memory_space=pl.ANY)
    rows = s_tok * SUBLANES
    pieces = _super_block_pieces(h_p.shape[0] // SUBLANES, h_s.shape[0] // SUBLANES, s_tok)
    return pl.pallas_call(
        functools.partial(_moe_expert_kernel, pieces=pieces, s_tok=s_tok, s_pad=s_pad,
                          l_pad=l_pad, tm=tm),
        grid_spec=pltpu.PrefetchScalarGridSpec(
            num_scalar_prefetch=2,
            grid=(N_SUPER, N_EXPERTS),
            in_specs=[any_spec] * 4 + [w_spec(w13), w_spec(w2)],
            out_specs=[any_spec, any_spec],
            scratch_shapes=[pltpu.VMEM((rows, LANES), F32), pltpu.VMEM((rows, LANES), F32),
                            pltpu.VMEM((tm * SUBLANES, LANES), F32),
                            pltpu.VMEM((tm * SUBLANES, LANES), F32),
                            pltpu.SMEM((l_pad,), jnp.int32), pltpu.SMEM((2 * s_pad,), F32),
                            pltpu.SemaphoreType.DMA((8,))]),
        out_shape=[jax.ShapeDtypeStruct(h_p.shape, F32), jax.ShapeDtypeStruct(h_s.shape, F32)],
        compiler_params=pltpu.CompilerParams(
            dimension_semantics=("arbitrary", "arbitrary"), vmem_limit_bytes=VMEM_LIMIT_EXPERT),
        name=f"moe_experts_{l}",
    )(counts, offs, idx, gates, h_p, h_s, w13, w2)


def _moe_residual_kernel(x_ref, gate_ref, gfin_ref, y_ref, o_ref, *, rows_per_seq, tm, final_norm):
    (gate,) = _modulation([gate_ref], rows_per_seq, tm)
    y = x_ref[...] + gate * _from_token_tiles(y_ref, tm)
    o_ref[...] = _rms(y, gfin_ref[...]) if final_norm else y


def _moe_residual(x2d, mod, y, l, wts, rows_per_seq, final_norm):
    m, d = x2d.shape
    tm = min(TM_FFN, m)
    assert m % tm == 0
    x_spec = pl.BlockSpec((tm, d), lambda t: (t, 0))
    return pl.pallas_call(
        functools.partial(_moe_residual_kernel, rows_per_seq=rows_per_seq, tm=tm,
                          final_norm=final_norm),
        grid=(m // tm,),
        in_specs=[x_spec] + _mod_specs(mod, l, (5,), rows_per_seq, tm)
                 + [_const_spec(wts['g_final'].shape),
                    pl.BlockSpec((tm * SUBLANES, LANES), lambda t: (t, 0))],
        out_specs=x_spec,
        out_shape=jax.ShapeDtypeStruct((m, d), F32),
        compiler_params=pltpu.CompilerParams(
            dimension_semantics=("arbitrary",), vmem_limit_bytes=VMEM_LIMIT),
        name=f"moe_residual_{l}_{m}",
    )(x2d, mod, wts['g_final'], y)


def _ffn_moe(routed_p, xs, mod_s, l, wts, w_bf):
    h_p, meta_p = routed_p
    m_p, m_s = meta_p.shape[0], xs.shape[0]
    n_tok = m_p + m_s
    s_tok = n_tok // N_SUPER
    assert s_tok * N_SUPER == n_tok and s_tok % SUBLANES == 0
    s_pad = -(-s_tok // IDX_ALIGN) * IDX_ALIGN

    h_s, meta_s = _moe_route(xs, mod_s, l, wts, 1)

    n_meta = 2 * N_EXPERTS + 2
    meta = jnp.concatenate([meta_p[:, :n_meta], meta_s[:, :n_meta]], axis=0)
    per_seg = lambda a: a.reshape(N_SUPER, s_tok, N_EXPERTS).transpose(0, 2, 1)
    gate = per_seg(meta[:, :N_EXPERTS])
    counts = jnp.sum(per_seg(meta[:, N_EXPERTS:2 * N_EXPERTS]).astype(jnp.int32), axis=2)
    offs = jnp.cumsum(counts, axis=1) - counts
    tok = lax.broadcasted_iota(jnp.int32, (N_SUPER, s_tok), 1)
    tok_bits = (s_tok - 1).bit_length()
    keys = jnp.concatenate(
        [meta[:, 2 * N_EXPERTS + k].astype(jnp.int32).reshape(N_SUPER, s_tok) * (1 << tok_bits) + tok
         for k in range(2)], axis=1)
    idx = lax.sort(keys, dimension=1) & ((1 << tok_bits) - 1)
    l_pad = -(-(2 * s_tok + 2 * TM_EXPERT) // IDX_ALIGN) * IDX_ALIGN
    idx = jnp.pad(idx, ((0, 0), (0, l_pad - 2 * s_tok))).reshape(-1)
    gate = jnp.pad(gate, ((0, 0), (0, 0), (0, s_pad - s_tok))).reshape(-1)

    return _moe_experts(counts.reshape(-1), offs.reshape(-1), idx, gate, h_p, h_s, l, w_bf,
                        s_tok, s_pad, l_pad)


def kernel(x_prompt, x_sample, state_conv_a, state_conv_c, state_pool_d, c_prompt, c_sample,
           w_ada, b_ada, g_mix, w_in, conv_a_w, ln_v_g, ln_v_b, w_s, b_s, conv_c_w, conv_c_b,
           ln_c_g, ln_c_b, pool_w, pool_scale, g_out, w_out, g_ffn, w1_dense, w3_dense, w2_dense,
           w_router, b_router, w1_moe, w3_moe, w2_moe, g_final):
    p = dict(g_mix=g_mix, w_in=w_in, conv_a_w=conv_a_w, ln_v_g=ln_v_g, ln_v_b=ln_v_b, w_s=w_s,
             b_s=b_s, conv_c_w=conv_c_w, conv_c_b=conv_c_b, ln_c_g=ln_c_g, ln_c_b=ln_c_b,
             pool_w=pool_w, pool_scale=pool_scale, g_out=g_out, w_out=w_out, g_ffn=g_ffn,
             w1_dense=w1_dense, w3_dense=w3_dense, w2_dense=w2_dense, w_router=w_router,
             b_router=b_router, w1_moe=w1_moe, w3_moe=w3_moe, w2_moe=w2_moe, g_final=g_final)
    n_p, seq, d = x_prompt.shape
    n_s, dec_seq, _ = x_sample.shape
    assert dec_seq == 1 and d == D_MODEL

    mod_p, mod_s = _ada(c_prompt, c_sample, w_ada, b_ada)
    kmajor = lambda s: jnp.transpose(s, (0, 2, 1, 3))
    sa_t, sc_t, sd_t = kmajor(state_conv_a), kmajor(state_conv_c), kmajor(state_pool_d)
    mixw = _mix_params(p)
    pad_e = LANES - N_EXPERTS
    wts = dict(
        g_ffn=g_ffn.reshape(DEPTH, 1, d), g_final=g_final.reshape(1, d),
        w1_dense=w1_dense, w3_dense=w3_dense, w2_dense=w2_dense,
        w_router=jnp.pad(w_router, ((0, 0), (0, 0), (0, pad_e))),
        b_router=jnp.pad(b_router, ((0, 0), (0, pad_e))).reshape(-1, 1, LANES),
        w1_moe=w1_moe, w3_moe=w3_moe, w2_moe=w2_moe)

    xp = x_prompt
    xs = x_sample.reshape(n_s, d)
    states_p = [[], [], [], []]
    states_s = [[], [], [], []]
    pending = None
    moe_bf = None
    for l in range(DEPTH):
        last = l == DEPTH - 1
        moe = l % 2 == 1
        cast = [((mixw[k],), l) for k in ('w_in', 'w_out')]
        if not moe:
            cast += [((wts[k],), l // 2) for k in ('w1_dense', 'w3_dense', 'w2_dense')]
        st_p, cast_bf = _mix_prompt(xp, mod_p, l, mixw, 0, pending, wts if moe else None, cast)
        xp, *st_p = st_p
        proj_bf, dense_bf = cast_bf[:2], cast_bf[2:]
        pending = None
        xs, *st_s = _mix_sample(xs, mod_s, sa_t, sc_t, sd_t, l, mixw, proj_bf, PAST_LEN)
        for acc, s in zip(states_p, st_p[:4]):
            acc.append(s)
        for acc, s in zip(states_s, st_s):
            acc.append(s)
        if moe:
            if moe_bf is None:
                w1, w3, w2 = (wts[k][l // 2].astype(BF16) for k in ('w1_moe', 'w3_moe', 'w2_moe'))
                moe_bf = [jnp.concatenate([w1, w3], axis=-1), w2]
            y_p, y_s = _ffn_moe(st_p[4:], xs, mod_s, l, wts, moe_bf)
            moe_bf = None
            xs = _moe_residual(xs, mod_s, y_s, l, wts, 1, last)
            if last:
                xp = _moe_residual(xp.reshape(n_p * seq, d), mod_p, y_p, l, wts, seq,
                                   True).reshape(n_p, seq, d)
            else:
                pending = (y_p, l)
        else:
            nxt = (l + 1) // 2
            cast = ([((wts['w1_moe'], wts['w3_moe']), nxt), ((wts['w2_moe'],), nxt)]
                    if l + 1 < DEPTH else [])
            xp2d, moe_bf = _ffn_dense(xp.reshape(n_p * seq, d), mod_p, l, wts, dense_bf, seq, last,
                                      cast)
            xp = xp2d.reshape(n_p, seq, d)
            xs, _ = _ffn_dense(xs, mod_s, l, wts, dense_bf, 1, last)
            moe_bf = moe_bf or None

    a_p, c_p, d_p, v_p = (jnp.stack(s) for s in states_p)
    a_s, c_s, d_s = (kmajor(jnp.stack(s)) for s in states_s[:3])
    v_s = jnp.stack(states_s[3]).reshape(DEPTH, n_s, 1, W_GROUP)
    return (xp, xs.reshape(n_s, 1, d), a_p, c_p, d_p, v_p, a_s, c_s, d_s, v_s)
```
